```python
import math
import jax, jax.numpy as jnp
from jax import lax
import numpy as np


D_MODEL = 1024
BATCH = 32
SEQ = 2048
DEPTH = 1

ATT_GROUPS = ((128, 1), (512, 4), (2048, 16))
N_ATT_GROUPS = 3
ATT_HEADS_PER_GROUP = 4
ATT_HEAD_DIM = 128
N_ATT_HEADS = N_ATT_GROUPS * ATT_HEADS_PER_GROUP
ATT_GROUP_WIDTH = ATT_HEADS_PER_GROUP * ATT_HEAD_DIM
ATT_BLOCK = 128
ML_HEADS = 8
ML_QK_DIM = 64
ML_V_DIM = 128
ML_QK_WIDTH = ML_HEADS * ML_QK_DIM
ML_V_WIDTH = ML_HEADS * ML_V_DIM
ML_CHUNK = 64
CONV_WIDTH = 4
D_FF = 4 * D_MODEL
N_BUCKETS = 32
MAX_DISTANCE = 2048
N_BRANCHES = 2
EPS = 1e-6

IN_SPLITS = (
    N_ATT_GROUPS * ATT_GROUP_WIDTH,
    N_ATT_GROUPS * ATT_GROUP_WIDTH,
    N_ATT_GROUPS * ATT_GROUP_WIDTH,
    ML_QK_WIDTH,
    ML_QK_WIDTH,
    ML_V_WIDTH,
    ML_V_WIDTH,
    ML_HEADS,
    ML_HEADS,
    N_BRANCHES * D_MODEL,
)
D_IN = sum(IN_SPLITS)

kernel_name = 'hybrid_dilated_attn_mlstm_block'


def _rms_norm(x, gain):
    x32 = x.astype(jnp.float32)
    y = x32 * lax.rsqrt(jnp.mean(x32 * x32, axis=-1, keepdims=True) + EPS)
    return (y * gain.astype(jnp.float32)).astype(x.dtype)


def _t5_bucket(dist):
    max_exact = N_BUCKETS // 2
    d = jnp.maximum(dist, max_exact).astype(jnp.float32)
    large = max_exact + (jnp.log(d / max_exact) / math.log(MAX_DISTANCE / max_exact)
                         * (N_BUCKETS - max_exact)).astype(jnp.int32)
    large = jnp.minimum(large, N_BUCKETS - 1)
    return jnp.where(dist < max_exact, dist, large)


def _causal_conv(x, w, b):
    S = x.shape[1]
    xp = jnp.pad(x, ((0, 0), (CONV_WIDTH - 1, 0), (0, 0)))
    y = b
    for j in range(CONV_WIDTH):
        y = y + w[j] * xp[:, j:j + S]
    return y


def _dilated_attention(q, k, v, bias_table, window, dilation):
    B, S, H, dh = q.shape
    n_look = window // dilation
    n_sub = S // dilation
    nb = -(-n_sub // ATT_BLOCK)
    n_pad = nb * ATT_BLOCK

    def to_blocks(t):
        t = t.reshape(B, n_sub, dilation, H, dh).transpose(0, 2, 3, 1, 4)
        t = jnp.pad(t, ((0, 0), (0, 0), (0, 0), (0, n_pad - n_sub), (0, 0)))
        return t.astype(jnp.float32).reshape(B, dilation, H, nb, ATT_BLOCK, dh)

    def with_prev(t):
        prev = jnp.pad(t, ((0, 0), (0, 0), (0, 0), (1, 0), (0, 0), (0, 0)))[:, :, :, :-1]
        return jnp.concatenate([prev, t], axis=4)

    qb = to_blocks(q)
    kb = with_prev(to_blocks(k))
    vb = with_prev(to_blocks(v))

    i = jnp.arange(ATT_BLOCK)[:, None]
    j = jnp.arange(2 * ATT_BLOCK)[None, :]
    delta = ATT_BLOCK + i - j
    key_idx = jnp.arange(nb)[:, None, None] * ATT_BLOCK - ATT_BLOCK + j
    valid = (delta >= 0) & (delta <= n_look) & (key_idx >= 0)
    bucket = _t5_bucket(jnp.maximum(delta, 0) * dilation)
    bias = bias_table[bucket].astype(jnp.float32).transpose(2, 0, 1)

    s = jnp.einsum('brhnqd,brhnkd->brhnqk', qb, kb) * (dh ** -0.5) + bias[None, None, :, None]
    s = jnp.where(valid[None, None, None], s, -jnp.inf)
    m = jnp.max(s, axis=-1, keepdims=True)
    p = jnp.exp(s - m)
    l = jnp.sum(p, axis=-1, keepdims=True)
    o = jnp.einsum('brhnqk,brhnkd->brhnqd', p, vb) / l
    lse = (m + jnp.log(l))[..., 0]

    o = o.reshape(B, dilation, H, n_pad, dh)[:, :, :, :n_sub]
    o = o.transpose(0, 3, 1, 2, 4).reshape(B, S, H, dh)
    lse = lse.reshape(B, dilation, H, n_pad)[..., :n_sub]
    lse = lse.transpose(0, 3, 1, 2).reshape(B, S, H)
    return o, lse


def _mlstm_chunkwise(q, k, v, log_i, log_f):
    B, S, H, dk = q.shape
    dv = v.shape[-1]
    L = ML_CHUNK
    nc = S // L

    def chunks(t):
        rest = t.shape[3:]
        t = t.reshape((B, nc, L, H) + rest)
        return t.transpose((1, 0, 3, 2) + tuple(range(4, t.ndim)))

    causal = jnp.tril(jnp.ones((L, L), dtype=bool))

    def step(carry, xs):
        C, n, m = carry
        qc, kc, vc, li, lf = xs
        b = jnp.cumsum(lf, axis=-1)
        a = b + m[..., None]
        Dm = b[..., :, None] - b[..., None, :] + li[..., None, :]
        Dm = jnp.where(causal, Dm, -jnp.inf)
        m_t = jnp.maximum(a, jnp.max(Dm, axis=-1))
        w = jnp.einsum('bhtd,bhsd->bhts', qc, kc) * jnp.exp(Dm - m_t[..., None])
        inter = jnp.exp(a - m_t)
        num = jnp.einsum('bhts,bhsv->bhtv', w, vc) + inter[..., None] * jnp.einsum('bhtd,bhdv->bhtv', qc, C)
        nq = jnp.sum(w, axis=-1) + inter * jnp.einsum('bhtd,bhd->bht', qc, n)
        h = num / jnp.maximum(jnp.abs(nq), jnp.exp(-m_t))[..., None]
        b_end = b[..., -1]
        g = b_end[..., None] - b + li
        m_new = jnp.maximum(b_end + m, jnp.max(g, axis=-1))
        decay = jnp.exp(b_end + m - m_new)
        wk = jnp.exp(g - m_new[..., None])
        C = decay[..., None, None] * C + jnp.einsum('bhs,bhsd,bhsv->bhdv', wk, kc, vc)
        n = decay[..., None] * n + jnp.einsum('bhs,bhsd->bhd', wk, kc)
        return (C, n, m_new), h

    init = (jnp.zeros((B, H, dk, dv), jnp.float32),
            jnp.zeros((B, H, dk), jnp.float32),
            jnp.zeros((B, H), jnp.float32))
    _, h = lax.scan(step, init, (chunks(q), chunks(k), chunks(v), chunks(log_i), chunks(log_f)))
    return h.transpose(1, 0, 3, 2, 4).reshape(B, S, H, dv)


def setup_inputs(seed: int = 0) -> dict:
    key = jax.random.key(seed)
    ks = jax.random.split(key, 20)
    f32 = jnp.float32

    def w(k, shape, fan_in, gain=1.0):
        return jax.random.normal(k, shape, f32) * (gain * fan_in ** -0.5)

    def gain_vec(k, shape):
        return 1.0 + 0.02 * jax.random.normal(k, shape, f32)

    gate_bias_offset = jnp.stack([jnp.zeros((ML_HEADS,), f32),
                                  jnp.linspace(3.0, 6.0, ML_HEADS, dtype=f32)])
    return {
        'x': jax.random.normal(ks[0], (BATCH, SEQ, D_MODEL), f32),
        'c': jax.random.normal(ks[1], (BATCH, D_MODEL), f32),
        'w_ada': w(ks[2], (DEPTH, D_MODEL, 6 * D_MODEL), D_MODEL, 0.5),
        'b_ada': 0.02 * jax.random.normal(ks[3], (DEPTH, 6 * D_MODEL), f32),
        'norm1_g': gain_vec(ks[4], (DEPTH, D_MODEL)),
        'norm2_g': gain_vec(ks[5], (DEPTH, D_MODEL)),
        'w_in': w(ks[6], (DEPTH, D_MODEL, D_IN), D_MODEL),
        'b_if': gate_bias_offset + 0.1 * jax.random.normal(ks[7], (DEPTH, 2, ML_HEADS), f32),
        'conv_w': w(ks[8], (DEPTH, CONV_WIDTH, 2 * ML_QK_WIDTH), CONV_WIDTH),
        'conv_b': 0.02 * jax.random.normal(ks[9], (DEPTH, 2 * ML_QK_WIDTH), f32),
        'q_norm_g': gain_vec(ks[10], (DEPTH, ATT_HEAD_DIM)),
        'k_norm_g': gain_vec(ks[11], (DEPTH, ATT_HEAD_DIM)),
        'rel_bias': 0.5 * jax.random.normal(ks[12], (N_BUCKETS, N_ATT_HEADS), f32),
        'mlstm_norm_g': gain_vec(ks[13], (DEPTH, ML_V_WIDTH)),
        'w_att_out': w(ks[14], (DEPTH, ATT_GROUP_WIDTH, D_MODEL), ATT_GROUP_WIDTH),
        'w_ml_out': w(ks[15], (DEPTH, ML_V_WIDTH, D_MODEL), ML_V_WIDTH),
        'w_out': w(ks[16], (DEPTH, D_MODEL, D_MODEL), D_MODEL),
        'w_ff1': w(ks[17], (DEPTH, D_MODEL, D_FF), D_MODEL),
        'w_ff2': w(ks[18], (DEPTH, D_FF, D_MODEL), D_FF),
    }


def reference(x, c, w_ada, b_ada, norm1_g, norm2_g, w_in, b_if, conv_w, conv_b,
              q_norm_g, k_norm_g, rel_bias, mlstm_norm_g, w_att_out, w_ml_out,
              w_out, w_ff1, w_ff2):
    B, S, _ = x.shape
    split_points = [int(p) for p in np.cumsum(IN_SPLITS)[:-1]]
    for l in range(DEPTH):
        ada = jax.nn.silu(c) @ w_ada[l] + b_ada[l]
        shift1, scale1, gate1, shift2, scale2, gate2 = [t[:, None, :] for t in jnp.split(ada, 6, axis=-1)]

        u = _rms_norm(x, norm1_g[l]) * (1 + scale1) + shift1
        proj = u @ w_in[l]
        aq, ak, av, mq, mk, mv, mo, mi, mf, gates = jnp.split(proj, split_points, axis=-1)

        aq = _rms_norm(aq.reshape(B, S, N_ATT_HEADS, ATT_HEAD_DIM), q_norm_g[l])
        ak = _rms_norm(ak.reshape(B, S, N_ATT_HEADS, ATT_HEAD_DIM), k_norm_g[l])
        av = av.reshape(B, S, N_ATT_HEADS, ATT_HEAD_DIM)
        outs, lses = [], []
        for g, (window, dilation) in enumerate(ATT_GROUPS):
            sl = slice(g * ATT_HEADS_PER_GROUP, (g + 1) * ATT_HEADS_PER_GROUP)
            o, lse = _dilated_attention(aq[:, :, sl], ak[:, :, sl], av[:, :, sl],
                                        rel_bias[:, sl], window, dilation)
            outs.append(o)
            lses.append(lse)
        wts = jax.nn.softmax(jnp.stack(lses, axis=0), axis=0)
        att = jnp.sum(wts[..., None] * jnp.stack(outs, axis=0), axis=0)
        y_att = att.reshape(B, S, ATT_GROUP_WIDTH).astype(x.dtype) @ w_att_out[l]

        qk = jax.nn.silu(_causal_conv(jnp.concatenate([mq, mk], axis=-1), conv_w[l], conv_b[l]))
        mq, mk = jnp.split(qk.astype(jnp.float32), 2, axis=-1)
        mq = mq.reshape(B, S, ML_HEADS, ML_QK_DIM)
        mk = mk.reshape(B, S, ML_HEADS, ML_QK_DIM) * (ML_QK_DIM ** -0.5)
        mv = mv.astype(jnp.float32).reshape(B, S, ML_HEADS, ML_V_DIM)
        log_i = mi.astype(jnp.float32) + b_if[l, 0].astype(jnp.float32)
        log_f = jax.nn.log_sigmoid(mf.astype(jnp.float32) + b_if[l, 1].astype(jnp.float32))
        h = _mlstm_chunkwise(mq, mk, mv, log_i, log_f)
        h = _rms_norm(h, mlstm_norm_g[l].reshape(ML_HEADS, ML_V_DIM)).reshape(B, S, ML_V_WIDTH)
        h = h * jax.nn.sigmoid(mo.astype(jnp.float32))
        y_ml = h.astype(x.dtype) @ w_ml_out[l]

        g_att, g_ml = jnp.split(jax.nn.sigmoid(gates), N_BRANCHES, axis=-1)
        y = (g_att * y_att + g_ml * y_ml) @ w_out[l]
        x = x + gate1 * y

        u2 = _rms_norm(x, norm2_g[l]) * (1 + scale2) + shift2
        hdn = jnp.square(jax.nn.relu(u2 @ w_ff1[l]))
        x = x + gate2 * (hdn @ w_ff2[l])
    return x
```

```python
import functools
import math

import numpy as np
import jax
import jax.numpy as jnp
from jax import lax
from jax.experimental import pallas as pl
from jax.experimental.pallas import tpu as pltpu

F32 = jnp.float32
BF16 = jnp.bfloat16

D_MODEL = 1024
ATT_GROUPS = ((128, 1), (512, 4), (2048, 16))
N_ATT_GROUPS = 3
ATT_HPG = 4
ATT_DH = 128
N_ATT_HEADS = 12
ATT_W = 512
ATT_BLK = 128
ML_HEADS = 8
ML_DK = 64
ML_DV = 128
ML_QK_W = 512
ML_V_W = 1024
CONV_W = 4
D_FF = 4096
N_BUCKETS = 32
MAX_DISTANCE = 2048
EPS = 1e-6
NEG = -1e30

LANE = 128
ML_L = 128

PW = 9728
CB_G = 0
CB_AQ, CB_AK, CB_AV = 16, 28, 40
CB_MQ, CB_MK, CB_MV, CB_MO = 52, 56, 60, 68
IF_LO, IF_HI = 7680, 7696

VMEM_LIMIT = 56 * 1024 * 1024


def _dot(a, b):
    return jnp.dot(a, b, preferred_element_type=F32)


def _dot_nt(a, b):
    return lax.dot_general(a, b, (((1,), (1,)), ((), ())), preferred_element_type=F32)


def _dot_tn(a, b):
    return lax.dot_general(a, b, (((0,), (0,)), ((), ())), preferred_element_type=F32)


def _split3(a):
    hi = a.astype(BF16)
    r1 = a - hi.astype(F32)
    mid = r1.astype(BF16)
    lo = (r1 - mid.astype(F32)).astype(BF16)
    return hi, mid, lo


def _dot3(a, rhs_bf16):
    hi, mid, lo = _split3(a)
    return _dot(hi, rhs_bf16) + _dot(mid, rhs_bf16) + _dot(lo, rhs_bf16)


def _ada_kernel(c_ref, w_ref, b_ref, o_ref):
    c = c_ref[...]
    s = c * jax.nn.sigmoid(c)
    o_ref[...] = _dot(s.astype(BF16), w_ref[...]) + b_ref[...]


def _ada(c, w_bf16, b):
    B = c.shape[0]
    n = w_bf16.shape[1]
    tn = 1024
    return pl.pallas_call(
        _ada_kernel,
        grid=(n // tn,),
        in_specs=[pl.BlockSpec((B, D_MODEL), lambda j: (0, 0)),
                  pl.BlockSpec((D_MODEL, tn), lambda j: (0, j)),
                  pl.BlockSpec((1, tn), lambda j: (0, j))],
        out_specs=pl.BlockSpec((B, tn), lambda j: (0, j)),
        out_shape=jax.ShapeDtypeStruct((B, n), F32),
        name="ada",
    )(c, w_bf16, b.reshape(1, n))


def _t5_bucket_np(dist):
    max_exact = N_BUCKETS // 2
    d = np.maximum(dist, max_exact).astype(np.float32)
    large = max_exact + (np.log(d / np.float32(max_exact)) / np.float32(math.log(MAX_DISTANCE / max_exact))
                         * np.float32(N_BUCKETS - max_exact)).astype(np.int32)
    large = np.minimum(large, N_BUCKETS - 1)
    return np.where(dist < max_exact, dist, large).astype(np.int32)


def _bucket_tiles():
    i = np.arange(ATT_BLK)[:, None]
    j = np.arange(2 * ATT_BLK)[None, :]
    delta = ATT_BLK + i - j
    return np.stack([_t5_bucket_np(np.maximum(delta, 0) * dil) for _, dil in ATT_GROUPS])


def _bias_kernel(tab_ref, bucket_ref, o_ref):
    hh = pl.program_id(0)
    bucket = bucket_ref[...]
    acc = jnp.zeros(bucket.shape, F32)
    for k in range(N_BUCKETS):
        acc = jnp.where(bucket == k, tab_ref[k, hh], acc)
    i = lax.broadcasted_iota(jnp.int32, bucket.shape, 0)
    j = lax.broadcasted_iota(jnp.int32, bucket.shape, 1)
    delta = ATT_BLK + i - j
    valid = (delta >= 0) & (delta <= ATT_BLK)
    o_ref[0] = jnp.where(valid, acc, NEG)
    o_ref[1] = jnp.where(valid & (j >= ATT_BLK), acc, NEG)


def _bias_tiles(rel_bias):
    buckets = jnp.asarray(_bucket_tiles())
    return pl.pallas_call(
        _bias_kernel,
        grid=(N_ATT_HEADS,),
        in_specs=[pl.BlockSpec(memory_space=pltpu.SMEM),
                  pl.BlockSpec((None, ATT_BLK, 2 * ATT_BLK), lambda h: (h // ATT_HPG, 0, 0))],
        out_specs=pl.BlockSpec((None, 2, ATT_BLK, 2 * ATT_BLK), lambda h: (h, 0, 0, 0)),
        out_shape=jax.ShapeDtypeStruct((N_ATT_HEADS, 2, ATT_BLK, 2 * ATT_BLK), F32),
        name="bias_tiles",
    )(rel_bias, buckets)


IP_TM = 1024
IP_TN = 512
IP_Q0 = CB_AQ * LANE // IP_TN
IP_K0 = CB_AK * LANE // IP_TN
IP_V0 = CB_AV * LANE // IP_TN


def _inproj_kernel(x_ref, ada_ref, g1_ref, w_ref, wif_ref, qg_ref, kg_ref, o_ref, gt_ref, u_scr):
    j = pl.program_id(1)

    @pl.when(j == 0)
    def _():
        x = x_ref[...]
        ms = jnp.mean(x * x, axis=-1, keepdims=True)
        y = x * lax.rsqrt(ms + EPS) * g1_ref[...]
        u = y * (1.0 + ada_ref[1:2, :]) + ada_ref[0:1, :]
        ub = u.astype(BF16)
        u_scr[...] = ub
        gt_ref[...] = _dot_nt(wif_ref[...], ub)

    acc = _dot(u_scr[...], w_ref[...])

    is_qk = (j >= IP_Q0) & (j < IP_V0)

    @pl.when(is_qk)
    def _():
        gain = jnp.where(j < IP_K0, qg_ref[...], kg_ref[...])
        for k in range(IP_TN // ATT_DH):
            a = acc[:, k * ATT_DH:(k + 1) * ATT_DH]
            ms = jnp.mean(a * a, axis=-1, keepdims=True)
            o_ref[:, k * ATT_DH:(k + 1) * ATT_DH] = (a * lax.rsqrt(ms + EPS) * gain).astype(o_ref.dtype)

    @pl.when(jnp.logical_not(is_qk))
    def _():
        o_ref[...] = acc.astype(o_ref.dtype)


def _inproj(x2, ada3, g1, w_main, w_if_t, qg, kg, S):
    T = x2.shape[0]
    tm, tn = IP_TM, IP_TN
    tiles_per_seq = S // tm
    return pl.pallas_call(
        _inproj_kernel,
        grid=(T // tm, PW // tn),
        in_specs=[pl.BlockSpec((tm, D_MODEL), lambda i, j: (i, 0)),
                  pl.BlockSpec((None, 6, D_MODEL), lambda i, j: (i // tiles_per_seq, 0, 0)),
                  pl.BlockSpec((1, D_MODEL), lambda i, j: (0, 0)),
                  pl.BlockSpec((D_MODEL, tn), lambda i, j: (0, j)),
                  pl.BlockSpec((16, D_MODEL), lambda i, j: (0, 0)),
                  pl.BlockSpec((1, ATT_DH), lambda i, j: (0, 0)),
                  pl.BlockSpec((1, ATT_DH), lambda i, j: (0, 0))],
        out_specs=[pl.BlockSpec((tm, tn), lambda i, j: (i, j)),
                   pl.BlockSpec((16, tm), lambda i, j: (0, i))],
        out_shape=[jax.ShapeDtypeStruct((T, PW), BF16),
                   jax.ShapeDtypeStruct((16, T), F32)],
        scratch_shapes=[pltpu.VMEM((tm, D_MODEL), BF16)],
        compiler_params=pltpu.CompilerParams(
            dimension_semantics=("parallel", "arbitrary"), vmem_limit_bytes=VMEM_LIMIT),
        name="inproj",
    )(x2, ada3, g1, w_main, w_if_t, qg, kg)


def _attn_kernel(q0_ref, k0_ref, v0_ref, q1_ref, k1_ref, v1_ref, q2_ref, k2_ref, v2_ref,
                 bias_ref, o_ref,
                 k0s, v0s, q1f, k1f, v1f, q2f, k2f, v2f, o_scr, l_scr, *, S):
    scale = ATT_DH ** -0.5
    blk = ATT_BLK

    k0s[0:blk, :] = jnp.zeros((blk, LANE), BF16)
    v0s[0:blk, :] = jnp.zeros((blk, LANE), BF16)
    k0s[blk:blk + S, :] = k0_ref[...]
    v0s[blk:blk + S, :] = v0_ref[...]
    pad1 = blk * ATT_GROUPS[1][1]
    k1f[0:pad1, :] = jnp.zeros((pad1, LANE), F32)
    v1f[0:pad1, :] = jnp.zeros((pad1, LANE), F32)
    k1f[pad1:pad1 + S, :] = k1_ref[...].astype(F32)
    v1f[pad1:pad1 + S, :] = v1_ref[...].astype(F32)
    q1f[...] = q1_ref[...].astype(F32)
    q2f[...] = q2_ref[...].astype(F32)
    k2f[...] = k2_ref[...].astype(F32)
    v2f[...] = v2_ref[...].astype(F32)

    def softmax_block(g, q, kk, vv, bias, row_start, r):
        s = _dot_nt(q, kk) * scale + bias
        m = jnp.max(s, axis=-1, keepdims=True)
        p = jnp.exp(s - m)
        l = jnp.sum(p, axis=-1, keepdims=True)
        acc = _dot(p.astype(BF16), vv)
        o = acc / l
        lse = m + jnp.log(l)
        if r == 1:
            rows = pl.ds(pl.multiple_of(row_start, blk), blk)
        else:
            rows = pl.ds(row_start, blk, stride=r)
        o_scr[g, rows, :] = o
        l_scr[g, rows, :] = jnp.broadcast_to(lse, (blk, LANE))

    def body0(n, carry):
        start = pl.multiple_of(n * blk, blk)
        q = q0_ref[pl.ds(start, blk), :]
        kk = k0s[pl.ds(start, 2 * blk), :]
        vv = v0s[pl.ds(start, 2 * blk), :]
        bias = bias_ref[0, jnp.where(n == 0, 1, 0)]
        softmax_block(0, q, kk, vv, bias, start, 1)
        return carry
    lax.fori_loop(0, S // blk, body0, 0)

    r1 = ATT_GROUPS[1][1]
    nb1 = S // r1 // blk

    def body1(idx, carry):
        rho = idx // nb1
        n = idx % nb1
        start = rho + r1 * blk * n
        q = q1f[pl.ds(start, blk, stride=r1), :].astype(BF16)
        kk = k1f[pl.ds(start, 2 * blk, stride=r1), :].astype(BF16)
        vv = v1f[pl.ds(start, 2 * blk, stride=r1), :].astype(BF16)
        bias = bias_ref[1, jnp.where(n == 0, 1, 0)]
        softmax_block(1, q, kk, vv, bias, start, r1)
        return carry
    lax.fori_loop(0, r1 * nb1, body1, 0)

    r2 = ATT_GROUPS[2][1]

    def body2(rho, carry):
        q = q2f[pl.ds(rho, blk, stride=r2), :].astype(BF16)
        kk = k2f[pl.ds(rho, blk, stride=r2), :].astype(BF16)
        vv = v2f[pl.ds(rho, blk, stride=r2), :].astype(BF16)
        bias = bias_ref[2, 1][:, blk:2 * blk]
        softmax_block(2, q, kk, vv, bias, rho, r2)
        return carry
    lax.fori_loop(0, r2, body2, 0)

    def merge(n, carry):
        rows = pl.ds(pl.multiple_of(n * blk, blk), blk)
        l0, l1, l2 = l_scr[0, rows, :], l_scr[1, rows, :], l_scr[2, rows, :]
        mx = jnp.maximum(jnp.maximum(l0, l1), l2)
        e0, e1, e2 = jnp.exp(l0 - mx), jnp.exp(l1 - mx), jnp.exp(l2 - mx)
        den = e0 + e1 + e2
        att = (e0 * o_scr[0, rows, :] + e1 * o_scr[1, rows, :] + e2 * o_scr[2, rows, :]) / den
        o_ref[rows, :] = att.astype(o_ref.dtype)
        return carry
    lax.fori_loop(0, S // blk, merge, 0)


def _attention(proj3, bias_tiles, B, S):
    def col(cb):
        return pl.BlockSpec((None, S, LANE), lambda b, h, cb=cb: (b, 0, cb + h))
    in_specs = []
    for g in range(N_ATT_GROUPS):
        for base in (CB_AQ, CB_AK, CB_AV):
            in_specs.append(col(base + g * ATT_HPG))
    in_specs.append(pl.BlockSpec((N_ATT_GROUPS, None, 2, ATT_BLK, 2 * ATT_BLK), lambda b, h: (0, h, 0, 0, 0)))
    pad1 = ATT_BLK * ATT_GROUPS[1][1]
    scratch = [pltpu.VMEM((ATT_BLK + S, LANE), BF16), pltpu.VMEM((ATT_BLK + S, LANE), BF16),
               pltpu.VMEM((S, LANE), F32), pltpu.VMEM((pad1 + S, LANE), F32), pltpu.VMEM((pad1 + S, LANE), F32),
               pltpu.VMEM((S, LANE), F32), pltpu.VMEM((S, LANE), F32), pltpu.VMEM((S, LANE), F32),
               pltpu.VMEM((N_ATT_GROUPS, S, LANE), F32), pltpu.VMEM((N_ATT_GROUPS, S, LANE), F32)]
    bias5 = bias_tiles.reshape(N_ATT_GROUPS, ATT_HPG, 2, ATT_BLK, 2 * ATT_BLK)
    args = [proj3] * 9 + [bias5]
    return pl.pallas_call(
        functools.partial(_attn_kernel, S=S),
        grid=(B, ATT_HPG),
        in_specs=in_specs,
        out_specs=pl.BlockSpec((None, S, LANE), lambda b, h: (b, 0, h)),
        out_shape=jax.ShapeDtypeStruct((B, S, ATT_W), BF16),
        scratch_shapes=scratch,
        compiler_params=pltpu.CompilerParams(
            dimension_semantics=("parallel", "parallel"), vmem_limit_bytes=VMEM_LIMIT),
        name="attn",
    )(*args)


def _mlstm_kernel(bif_ref, q2_ref, k2_ref, v_ref, mo_ref, gr_ref, cwq_ref, cwk_ref, cbq_ref, cbk_ref,
                  ng_ref, o_ref, xs, qa, ka, dcum, b1, wkb, rf, *, S):
    L = ML_L
    nc = S // L
    h = pl.program_id(1)
    odd = h % 2
    lane = lax.broadcasted_iota(jnp.int32, (1, LANE), 1)
    head_lanes = (lane >= odd * ML_DK) & (lane < (odd + 1) * ML_DK)

    def conv_silu(src_ref, w_ref, b_ref):
        xs[0:8, :] = jnp.zeros((8, LANE), F32)
        xs[8:8 + S, :] = src_ref[...].astype(F32)
        y = b_ref[...]
        for jj in range(CONV_W):
            y = y + w_ref[jj:jj + 1, :] * xs[pl.ds(8 - (CONV_W - 1) + jj, S), :]
        return y * jax.nn.sigmoid(y)

    qa[...] = jnp.where(head_lanes, conv_silu(q2_ref, cwq_ref, cbq_ref), 0.0).astype(BF16)
    ka[...] = (conv_silu(k2_ref, cwk_ref, cbk_ref) * (ML_DK ** -0.5)).astype(BF16)

    li = gr_ref[0] + bif_ref[0, h]
    zf = gr_ref[1] + bif_ref[1, h]
    lf = jnp.minimum(zf, 0.0) - jnp.log1p(jnp.exp(-jnp.abs(zf)))

    r_i = lax.broadcasted_iota(jnp.int32, (L, L), 0)
    c_i = lax.broadcasted_iota(jnp.int32, (L, L), 1)
    ones_m = jnp.ones((L, L), BF16)
    incl_upper = (r_i <= c_i).astype(BF16)
    strict_lower = (r_i > c_i).astype(BF16)
    causal = c_i <= r_i

    brow = _dot3(lf, incl_upper)
    bend = _dot3(lf, ones_m)
    g = bend - brow + li
    maxg = jnp.max(g, axis=-1, keepdims=True)

    m = jnp.zeros((1, L), F32)
    for c in range(nc):
        rf[1, c:c + 1, :] = m
        m = jnp.maximum(bend[c:c + 1, :] + m, maxg[c:c + 1, :])
        rf[2, c:c + 1, :] = m
    m_cur = rf[1, 0:nc, :]
    m_nxt = rf[2, 0:nc, :]
    wk_row = jnp.exp(g - m_nxt)
    decay = jnp.exp(bend + m_cur - m_nxt)
    rf[0, 0:nc, :] = li
    rf[3, 0:nc, :] = decay

    rhs_cum = jnp.concatenate([strict_lower, ones_m], axis=1)
    for c in range(nc):
        rows = slice(c * L, (c + 1) * L)
        a_c = jnp.where(causal, jnp.broadcast_to(lf[c:c + 1, :], (L, L)), 0.0)
        both = _dot3(a_c, rhs_cum)
        dcum[rows, :] = both[:, 0:L]
        b1[rows, :] = both[:, L:2 * L]
        w_c = jnp.where(r_i == c_i, jnp.broadcast_to(wk_row[c:c + 1, :], (L, L)), 0.0)
        wkb[rows, :] = _dot3(w_c, ones_m)

    ones_v = jnp.ones((L, LANE), BF16)
    gain = ng_ref[...]
    cst = jnp.zeros((LANE, 2 * LANE), F32)
    for c in range(nc):
        rows = slice(c * L, (c + 1) * L)
        qc = qa[rows, :]
        kc = ka[rows, :]
        vc = v_ref[rows, :]
        s_ = _dot_nt(qc, kc)
        dm = jnp.where(causal, dcum[rows, :] + rf[0, c:c + 1, :], NEG)
        a = b1[rows, :] + rf[1, c:c + 1, :]
        m_t = jnp.maximum(a, jnp.max(dm, axis=-1, keepdims=True))
        w = s_ * jnp.exp(dm - m_t)
        inter = jnp.exp(a - m_t)
        rowsum = jnp.sum(w, axis=-1, keepdims=True)
        p1 = _dot(w.astype(BF16), vc)
        qst = _dot(qc, cst.astype(BF16))
        num = p1 + inter * qst[:, 0:LANE]
        nq = rowsum + inter * qst[:, LANE:2 * LANE]
        hval = num / jnp.maximum(jnp.abs(nq), jnp.exp(-m_t))
        ms = jnp.mean(hval * hval, axis=-1, keepdims=True)
        hn = hval * lax.rsqrt(ms + EPS) * gain
        o_ref[rows, :] = (hn * jax.nn.sigmoid(mo_ref[rows, :].astype(F32))).astype(o_ref.dtype)
        kw = jnp.where(head_lanes, kc.astype(F32) * wkb[rows, :], 0.0).astype(BF16)
        z = _dot_tn(kw, jnp.concatenate([vc, ones_v], axis=1))
        dec = rf[3, c:c + 1, :]
        cst = jnp.concatenate([dec, dec], axis=1) * cst + z


def _mlstm(proj3, gates_rows, b_if, conv_w, conv_b, norm_g, B, S):
    nc = S // ML_L

    def col(cb, per_pair):
        if per_pair:
            return pl.BlockSpec((None, S, LANE), lambda b, h, cb=cb: (b, 0, cb + h // 2))
        return pl.BlockSpec((None, S, LANE), lambda b, h, cb=cb: (b, 0, cb + h))
    npair = ML_HEADS // 2
    in_specs = [
        pl.BlockSpec(memory_space=pltpu.SMEM),
        col(CB_MQ, True), col(CB_MK, True), col(CB_MV, False), col(CB_MO, False),
        pl.BlockSpec((None, 2, None, nc, ML_L), lambda b, h: (b, 0, h, 0, 0)),
        pl.BlockSpec((CONV_W, LANE), lambda b, h: (0, h // 2)),
        pl.BlockSpec((CONV_W, LANE), lambda b, h: (0, npair + h // 2)),
        pl.BlockSpec((1, LANE), lambda b, h: (0, h // 2)),
        pl.BlockSpec((1, LANE), lambda b, h: (0, npair + h // 2)),
        pl.BlockSpec((1, LANE), lambda b, h: (0, h)),
    ]
    scratch = [pltpu.VMEM((8 + S, LANE), F32),
               pltpu.VMEM((S, LANE), BF16), pltpu.VMEM((S, LANE), BF16),
               pltpu.VMEM((S, LANE), F32), pltpu.VMEM((S, LANE), F32), pltpu.VMEM((S, LANE), F32),
               pltpu.VMEM((4, max(nc, 8), ML_L), F32)]
    return pl.pallas_call(
        functools.partial(_mlstm_kernel, S=S),
        grid=(B, ML_HEADS),
        in_specs=in_specs,
        out_specs=pl.BlockSpec((None, S, LANE), lambda b, h: (b, 0, h)),
        out_shape=jax.ShapeDtypeStruct((B, S, ML_V_W), BF16),
        scratch_shapes=scratch,
        compiler_params=pltpu.CompilerParams(
            dimension_semantics=("parallel", "parallel"), vmem_limit_bytes=VMEM_LIMIT),
        name="mlstm",
    )(b_if, proj3, proj3, proj3, proj3, gates_rows, conv_w, conv_w, conv_b, conv_b, norm_g)


MG_TM = 512


def _merge_kernel(x_ref, ada_ref, att_ref, hml_ref, ga_ref, gm_ref, wa_ref, wm_ref, wo_ref, o_ref):
    y_att = _dot(att_ref[...], wa_ref[...])
    y_ml = _dot(hml_ref[...], wm_ref[...])
    ga = jax.nn.sigmoid(ga_ref[...].astype(F32))
    gm = jax.nn.sigmoid(gm_ref[...].astype(F32))
    mix = (ga * y_att + gm * y_ml).astype(BF16)
    y = _dot(mix, wo_ref[...])
    o_ref[...] = x_ref[...] + ada_ref[2:3, :] * y


def _merge(x2, ada3, att2, hml2, proj2, wa, wm, wo, S):
    T = x2.shape[0]
    tm = MG_TM
    tiles_per_seq = S // tm
    gcb = CB_G * LANE // D_MODEL
    const = lambda i: (0, 0)
    return pl.pallas_call(
        _merge_kernel,
        grid=(T // tm,),
        in_specs=[pl.BlockSpec((tm, D_MODEL), lambda i: (i, 0)),
                  pl.BlockSpec((None, 6, D_MODEL), lambda i: (i // tiles_per_seq, 0, 0)),
                  pl.BlockSpec((tm, ATT_W), lambda i: (i, 0)),
                  pl.BlockSpec((tm, ML_V_W), lambda i: (i, 0)),
                  pl.BlockSpec((tm, D_MODEL), lambda i: (i, gcb)),
                  pl.BlockSpec((tm, D_MODEL), lambda i: (i, gcb + 1)),
                  pl.BlockSpec((ATT_W, D_MODEL), const),
                  pl.BlockSpec((ML_V_W, D_MODEL), const),
                  pl.BlockSpec((D_MODEL, D_MODEL), const)],
        out_specs=pl.BlockSpec((tm, D_MODEL), lambda i: (i, 0)),
        out_shape=jax.ShapeDtypeStruct((T, D_MODEL), F32),
        compiler_params=pltpu.CompilerParams(
            dimension_semantics=("parallel",), vmem_limit_bytes=VMEM_LIMIT),
        name="merge",
    )(x2, ada3, att2, hml2, proj2, proj2, wa, wm, wo)


FF_TM = 512
FF_TK = 1024


def _ffn_kernel(x_ref, ada_ref, g2_ref, w1_ref, w2_ref, o_ref, u_scr, acc_scr):
    k = pl.program_id(1)

    @pl.when(k == 0)
    def _():
        x = x_ref[...]
        ms = jnp.mean(x * x, axis=-1, keepdims=True)
        y = x * lax.rsqrt(ms + EPS) * g2_ref[...]
        u_scr[...] = (y * (1.0 + ada_ref[4:5, :]) + ada_ref[3:4, :]).astype(BF16)

    hdn = jnp.maximum(_dot(u_scr[...], w1_ref[...]), 0.0)
    part = _dot((hdn * hdn).astype(BF16), w2_ref[...])

    @pl.when(k == 0)
    def _():
        acc_scr[...] = part

    @pl.when(k > 0)
    def _():
        acc_scr[...] += part

    @pl.when(k == pl.num_programs(1) - 1)
    def _():
        o_ref[...] = x_ref[...] + ada_ref[5:6, :] * acc_scr[...]


def _ffn(x2, ada3, g2, w1, w2, S):
    T = x2.shape[0]
    tm, tk = FF_TM, FF_TK
    tiles_per_seq = S // tm
    return pl.pallas_call(
        _ffn_kernel,
        grid=(T // tm, D_FF // tk),
        in_specs=[pl.BlockSpec((tm, D_MODEL), lambda i, k: (i, 0)),
                  pl.BlockSpec((None, 6, D_MODEL), lambda i, k: (i // tiles_per_seq, 0, 0)),
                  pl.BlockSpec((1, D_MODEL), lambda i, k: (0, 0)),
                  pl.BlockSpec((D_MODEL, tk), lambda i, k: (0, k)),
                  pl.BlockSpec((tk, D_MODEL), lambda i, k: (k, 0))],
        out_specs=pl.BlockSpec((tm, D_MODEL), lambda i, k: (i, 0)),
        out_shape=jax.ShapeDtypeStruct((T, D_MODEL), F32),
        scratch_shapes=[pltpu.VMEM((tm, D_MODEL), BF16), pltpu.VMEM((tm, D_MODEL), F32)],
        compiler_params=pltpu.CompilerParams(
            dimension_semantics=("parallel", "arbitrary"), vmem_limit_bytes=VMEM_LIMIT),
        name="ffn",
    )(x2, ada3, g2, w1, w2)


def kernel(x, c, w_ada, b_ada, norm1_g, norm2_g, w_in, b_if, conv_w, conv_b, q_norm_g, k_norm_g,
           rel_bias, mlstm_norm_g, w_att_out, w_ml_out, w_out, w_ff1, w_ff2):
    B, S, D = x.shape
    T = B * S
    depth = w_ada.shape[0]
    bias_tiles = _bias_tiles(rel_bias)
    x2 = x.reshape(T, D)
    for l in range(depth):
        ada3 = _ada(c, w_ada[l].astype(BF16), b_ada[l]).reshape(B, 6, D)
        w_main = jnp.concatenate([w_in[l][:, IF_HI:], w_in[l][:, :IF_LO]], axis=1).astype(BF16)
        w_if_t = w_in[l][:, IF_LO:IF_HI].T.astype(BF16)
        proj2, gates_t = _inproj(x2, ada3, norm1_g[l].reshape(1, D), w_main, w_if_t,
                                 q_norm_g[l].reshape(1, ATT_DH), k_norm_g[l].reshape(1, ATT_DH), S)
        proj3 = proj2.reshape(B, S, PW)
        att = _attention(proj3, bias_tiles, B, S)
        nc = S // ML_L
        gates_rows = gates_t.reshape(2, ML_HEADS, B, nc, ML_L).transpose(2, 0, 1, 3, 4)
        hml = _mlstm(proj3, gates_rows, b_if[l], conv_w[l], conv_b[l].reshape(1, -1),
                     mlstm_norm_g[l].reshape(1, ML_V_W), B, S)
        x2 = _merge(x2, ada3, att.reshape(T, ATT_W), hml.reshape(T, ML_V_W), proj2,
                    w_att_out[l].astype(BF16), w_ml_out[l].astype(BF16), w_out[l].astype(BF16), S)
        x2 = _ffn(x2, ada3, norm2_g[l].reshape(1, D), w_ff1[l].astype(BF16), w_ff2[l].astype(BF16), S)
    return x2.reshape(B, S, D)
```

```python
import functools
import math

import numpy as np
import jax
import jax.numpy as jnp
from jax import lax
from jax.experimental import pallas as pl
from jax.experimental.pallas import tpu as pltpu

F32 = jnp.float32
BF16 = jnp.bfloat16

D_MODEL = 1024
ATT_GROUPS = ((128, 1), (512, 4), (2048, 16))
N_ATT_GROUPS = 3
ATT_HPG = 4
ATT_DH = 128
N_ATT_HEADS = 12
ATT_W = 512
ATT_BLK = 128
ML_HEADS = 8
ML_DK = 64
ML_DV = 128
ML_QK_W = 512
ML_V_W = 1024
CONV_W = 4
D_FF = 4096
N_BUCKETS = 32
MAX_DISTANCE = 2048
EPS = 1e-6
NEG = -1e30

LANE = 128
ML_L = 128

PW = 9728
CB_G = 0
CB_AQ, CB_AK, CB_AV = 16, 28, 40
CB_MQ, CB_MK, CB_MV, CB_MO = 52, 56, 60, 68
IF_LO, IF_HI = 7680, 7696

VMEM_LIMIT = 56 * 1024 * 1024


def _dot(a, b):
    return jnp.dot(a, b, preferred_element_type=F32)


def _dot_nt(a, b):
    return lax.dot_general(a, b, (((1,), (1,)), ((), ())), preferred_element_type=F32)


def _split3(a):
    hi = a.astype(BF16)
    r1 = a - hi.astype(F32)
    mid = r1.astype(BF16)
    lo = (r1 - mid.astype(F32)).astype(BF16)
    return hi, mid, lo


def _dot3(a, rhs_bf16):
    hi, mid, lo = _split3(a)
    return _dot(hi, rhs_bf16) + _dot(mid, rhs_bf16) + _dot(lo, rhs_bf16)


def _resident(shape):
    return pl.BlockSpec(shape, lambda *_: (0,) * len(shape), pipeline_mode=pl.Buffered(1))


def _ada_kernel(c_ref, w_ref, b_ref, o_ref):
    c = c_ref[...]
    s = c * jax.nn.sigmoid(c)
    o_ref[...] = _dot(s.astype(BF16), w_ref[...]) + b_ref[...]


def _ada(c, w_bf16, b):
    B = c.shape[0]
    n = w_bf16.shape[1]
    tn = 1024
    return pl.pallas_call(
        _ada_kernel,
        grid=(n // tn,),
        in_specs=[pl.BlockSpec((B, D_MODEL), lambda j: (0, 0)),
                  pl.BlockSpec((D_MODEL, tn), lambda j: (0, j)),
                  pl.BlockSpec((1, tn), lambda j: (0, j))],
        out_specs=pl.BlockSpec((B, tn), lambda j: (0, j)),
        out_shape=jax.ShapeDtypeStruct((B, n), F32),
        name="ada",
    )(c, w_bf16, b.reshape(1, n))


def _t5_bucket_np(dist):
    max_exact = N_BUCKETS // 2
    d = np.maximum(dist, max_exact).astype(np.float32)
    large = max_exact + (np.log(d / np.float32(max_exact)) / np.float32(math.log(MAX_DISTANCE / max_exact))
                         * np.float32(N_BUCKETS - max_exact)).astype(np.int32)
    large = np.minimum(large, N_BUCKETS - 1)
    return np.where(dist < max_exact, dist, large).astype(np.int32)


def _bucket_tiles():
    i = np.arange(ATT_BLK)[:, None]
    j = np.arange(2 * ATT_BLK)[None, :]
    delta = ATT_BLK + i - j
    return np.stack([_t5_bucket_np(np.maximum(delta, 0) * dil) for _, dil in ATT_GROUPS])


def _bias_kernel(tab_ref, bucket_ref, o_ref):
    hh = pl.program_id(0)
    bucket = bucket_ref[...]
    acc = jnp.zeros(bucket.shape, F32)
    for k in range(N_BUCKETS):
        acc = jnp.where(bucket == k, tab_ref[k, hh], acc)
    i = lax.broadcasted_iota(jnp.int32, bucket.shape, 0)
    j = lax.broadcasted_iota(jnp.int32, bucket.shape, 1)
    delta = ATT_BLK + i - j
    valid = (delta >= 0) & (delta <= ATT_BLK)
    o_ref[0] = jnp.where(valid, acc, NEG)
    o_ref[1] = jnp.where(valid & (j >= ATT_BLK), acc, NEG)


def _bias_tiles(rel_bias):
    buckets = jnp.asarray(_bucket_tiles())
    return pl.pallas_call(
        _bias_kernel,
        grid=(N_ATT_HEADS,),
        in_specs=[pl.BlockSpec(memory_space=pltpu.SMEM),
                  pl.BlockSpec((None, ATT_BLK, 2 * ATT_BLK), lambda h: (h // ATT_HPG, 0, 0))],
        out_specs=pl.BlockSpec((None, 2, ATT_BLK, 2 * ATT_BLK), lambda h: (h, 0, 0, 0)),
        out_shape=jax.ShapeDtypeStruct((N_ATT_HEADS, 2, ATT_BLK, 2 * ATT_BLK), F32),
        name="bias_tiles",
    )(rel_bias, buckets)


IP_TM = 512
IP_TN = 512
IP_Q0 = CB_AQ * LANE // IP_TN
IP_K0 = CB_AK * LANE // IP_TN
IP_V0 = CB_AV * LANE // IP_TN


def _inproj_kernel(x_ref, ada_ref, g1_ref, w_ref, wif_ref, qg_ref, kg_ref, o_ref, gt_ref):
    x = x_ref[...]
    ms = jnp.mean(x * x, axis=-1, keepdims=True)
    y = x * lax.rsqrt(ms + EPS) * g1_ref[...]
    ub = (y * (1.0 + ada_ref[1:2, :]) + ada_ref[0:1, :]).astype(BF16)
    gt_ref[...] = _dot_nt(wif_ref[...], ub)
    for j in range(PW // IP_TN):
        acc = _dot(ub, w_ref[:, j * IP_TN:(j + 1) * IP_TN])
        if IP_Q0 <= j < IP_V0:
            gain = qg_ref[...] if j < IP_K0 else kg_ref[...]
            for k in range(IP_TN // ATT_DH):
                a = acc[:, k * ATT_DH:(k + 1) * ATT_DH]
                ms = jnp.mean(a * a, axis=-1, keepdims=True)
                lo = j * IP_TN + k * ATT_DH
                o_ref[:, lo:lo + ATT_DH] = (a * lax.rsqrt(ms + EPS) * gain).astype(o_ref.dtype)
        else:
            o_ref[:, j * IP_TN:(j + 1) * IP_TN] = acc.astype(o_ref.dtype)


def _inproj(x2, ada3, g1, w_main, w_if_t, qg, kg, S):
    T = x2.shape[0]
    tm = IP_TM
    tiles_per_seq = S // tm
    return pl.pallas_call(
        _inproj_kernel,
        grid=(T // tm,),
        in_specs=[pl.BlockSpec((tm, D_MODEL), lambda i: (i, 0)),
                  pl.BlockSpec((None, 6, D_MODEL), lambda i: (i // tiles_per_seq, 0, 0)),
                  _resident((1, D_MODEL)),
                  _resident((D_MODEL, PW)),
                  _resident((16, D_MODEL)),
                  _resident((1, ATT_DH)),
                  _resident((1, ATT_DH))],
        out_specs=[pl.BlockSpec((tm, PW), lambda i: (i, 0)),
                   pl.BlockSpec((16, tm), lambda i: (0, i))],
        out_shape=[jax.ShapeDtypeStruct((T, PW), BF16),
                   jax.ShapeDtypeStruct((16, T), F32)],
        compiler_params=pltpu.CompilerParams(
            dimension_semantics=("parallel",), vmem_limit_bytes=VMEM_LIMIT),
        name="inproj",
    )(x2, ada3, g1, w_main, w_if_t, qg, kg)


ATT_UNROLL = 8


def _attn_kernel(q0_ref, k0_ref, v0_ref, q1_ref, k1_ref, v1_ref, q2_ref, k2_ref, v2_ref,
                 bias_ref, o_ref,
                 k0s, v0s, q1f, k1f, v1f, q2f, k2f, v2f, o_scr, l_scr, *, S):
    scale = ATT_DH ** -0.5
    blk = ATT_BLK

    k0s[0:blk, :] = jnp.zeros((blk, LANE), BF16)
    v0s[0:blk, :] = jnp.zeros((blk, LANE), BF16)
    k0s[blk:blk + S, :] = k0_ref[...]
    v0s[blk:blk + S, :] = v0_ref[...]
    pad1 = blk * ATT_GROUPS[1][1]
    k1f[0:pad1, :] = jnp.zeros((pad1, LANE), F32)
    v1f[0:pad1, :] = jnp.zeros((pad1, LANE), F32)
    k1f[pad1:pad1 + S, :] = k1_ref[...].astype(F32)
    v1f[pad1:pad1 + S, :] = v1_ref[...].astype(F32)
    q1f[...] = q1_ref[...].astype(F32)
    q2f[...] = q2_ref[...].astype(F32)
    k2f[...] = k2_ref[...].astype(F32)
    v2f[...] = v2_ref[...].astype(F32)

    def run_group(g, r, n_iters, fetch):
        def body(it, carry):
            ops = [fetch(it, u) for u in range(ATT_UNROLL)]
            ss = [_dot_nt(q, kk) * scale + bias for (q, kk, _, bias, _) in ops]
            ms = [jnp.max(s, axis=-1, keepdims=True) for s in ss]
            ps = [jnp.exp(s - m) for s, m in zip(ss, ms)]
            ls = [jnp.sum(p, axis=-1, keepdims=True) for p in ps]
            accs = [_dot(p.astype(BF16), op[2]) for p, op in zip(ps, ops)]
            for acc, l, m, op in zip(accs, ls, ms, ops):
                row_start = op[4]
                if r == 1:
                    rows = pl.ds(pl.multiple_of(row_start, blk), blk)
                else:
                    rows = pl.ds(row_start, blk, stride=r)
                o_scr[g, rows, :] = acc / l
                l_scr[g, rows, :] = jnp.broadcast_to(m + jnp.log(l), (blk, LANE))
            return carry
        lax.fori_loop(0, n_iters, body, 0)

    def fetch0(it, u):
        n = it * ATT_UNROLL + u
        start = pl.multiple_of(n * blk, blk)
        bias = bias_ref[0, jnp.where(n == 0, 1, 0)]
        return (q0_ref[pl.ds(start, blk), :], k0s[pl.ds(start, 2 * blk), :], v0s[pl.ds(start, 2 * blk), :],
                bias, start)
    run_group(0, 1, S // blk // ATT_UNROLL, fetch0)

    r1 = ATT_GROUPS[1][1]
    nb1 = S // r1 // blk
    assert ATT_UNROLL % nb1 == 0

    def fetch1(it, u):
        rho = it * (ATT_UNROLL // nb1) + u // nb1
        n = u % nb1
        start = rho + r1 * blk * n
        return (q1f[pl.ds(start, blk, stride=r1), :].astype(BF16),
                k1f[pl.ds(start, 2 * blk, stride=r1), :].astype(BF16),
                v1f[pl.ds(start, 2 * blk, stride=r1), :].astype(BF16),
                bias_ref[1, 1 if n == 0 else 0], start)
    run_group(1, r1, r1 * nb1 // ATT_UNROLL, fetch1)

    r2 = ATT_GROUPS[2][1]
    assert S // r2 == blk

    def fetch2(it, u):
        rho = it * ATT_UNROLL + u
        return (q2f[pl.ds(rho, blk, stride=r2), :].astype(BF16),
                k2f[pl.ds(rho, blk, stride=r2), :].astype(BF16),
                v2f[pl.ds(rho, blk, stride=r2), :].astype(BF16),
                bias_ref[2, 1][:, blk:2 * blk], rho)
    run_group(2, r2, r2 // ATT_UNROLL, fetch2)

    def merge(n, carry):
        rows = pl.ds(pl.multiple_of(n * blk, blk), blk)
        l0, l1, l2 = l_scr[0, rows, :], l_scr[1, rows, :], l_scr[2, rows, :]
        mx = jnp.maximum(jnp.maximum(l0, l1), l2)
        e0, e1, e2 = jnp.exp(l0 - mx), jnp.exp(l1 - mx), jnp.exp(l2 - mx)
        den = e0 + e1 + e2
        att = (e0 * o_scr[0, rows, :] + e1 * o_scr[1, rows, :] + e2 * o_scr[2, rows, :]) / den
        o_ref[rows, :] = att.astype(o_ref.dtype)
        return carry
    lax.fori_loop(0, S // blk, merge, 0, unroll=2)


def _attention(proj3, bias_tiles, B, S):
    def col(cb):
        return pl.BlockSpec((None, S, LANE), lambda b, h, cb=cb: (b, 0, cb + h))
    in_specs = []
    for g in range(N_ATT_GROUPS):
        for base in (CB_AQ, CB_AK, CB_AV):
            in_specs.append(col(base + g * ATT_HPG))
    in_specs.append(pl.BlockSpec((N_ATT_GROUPS, None, 2, ATT_BLK, 2 * ATT_BLK), lambda b, h: (0, h, 0, 0, 0)))
    pad1 = ATT_BLK * ATT_GROUPS[1][1]
    scratch = [pltpu.VMEM((ATT_BLK + S, LANE), BF16), pltpu.VMEM((ATT_BLK + S, LANE), BF16),
               pltpu.VMEM((S, LANE), F32), pltpu.VMEM((pad1 + S, LANE), F32), pltpu.VMEM((pad1 + S, LANE), F32),
               pltpu.VMEM((S, LANE), F32), pltpu.VMEM((S, LANE), F32), pltpu.VMEM((S, LANE), F32),
               pltpu.VMEM((N_ATT_GROUPS, S, LANE), F32), pltpu.VMEM((N_ATT_GROUPS, S, LANE), F32)]
    bias5 = bias_tiles.reshape(N_ATT_GROUPS, ATT_HPG, 2, ATT_BLK, 2 * ATT_BLK)
    args = [proj3] * 9 + [bias5]
    return pl.pallas_call(
        functools.partial(_attn_kernel, S=S),
        grid=(B, ATT_HPG),
        in_specs=in_specs,
        out_specs=pl.BlockSpec((None, S, LANE), lambda b, h: (b, 0, h)),
        out_shape=jax.ShapeDtypeStruct((B, S, ATT_W), BF16),
        scratch_shapes=scratch,
        compiler_params=pltpu.CompilerParams(
            dimension_semantics=("parallel", "parallel"), vmem_limit_bytes=VMEM_LIMIT),
        name="attn",
    )(*args)


ML_PAIR = 2


def _mlstm_kernel(bif_ref, q2_ref, k2_ref, v_ref, mo_ref, gr_ref, cwq_ref, cwk_ref, cbq_ref, cbk_ref,
                  ng_ref, o_ref, xs, qa, kab, kaf, dcum, b1, rf, *, S):
    L = ML_L
    nc = S // L
    pair = pl.program_id(1)
    lane = lax.broadcasted_iota(jnp.int32, (1, LANE), 1)
    srow = lax.broadcasted_iota(jnp.int32, (LANE, 1), 0)
    in_head_lane = [(lane >= hh * ML_DK) & (lane < (hh + 1) * ML_DK) for hh in range(ML_PAIR)]
    in_head_row = [(srow >= hh * ML_DK) & (srow < (hh + 1) * ML_DK) for hh in range(ML_PAIR)]

    def conv_silu(src_ref, w_ref, b_ref):
        xs[0:8, :] = jnp.zeros((8, LANE), F32)
        xs[8:8 + S, :] = src_ref[...].astype(F32)
        y = b_ref[...]
        for jj in range(CONV_W):
            y = y + w_ref[jj:jj + 1, :] * xs[pl.ds(8 - (CONV_W - 1) + jj, S), :]
        return y * jax.nn.sigmoid(y)

    yq = conv_silu(q2_ref, cwq_ref, cbq_ref)
    for hh in range(ML_PAIR):
        qa[hh] = jnp.where(in_head_lane[hh], yq, 0.0).astype(BF16)
    yk = conv_silu(k2_ref, cwk_ref, cbk_ref) * (ML_DK ** -0.5)
    kaf[...] = yk
    kab[...] = yk.astype(BF16)

    r_i = lax.broadcasted_iota(jnp.int32, (L, L), 0)
    c_i = lax.broadcasted_iota(jnp.int32, (L, L), 1)
    ones_m = jnp.ones((L, L), BF16)
    incl_upper = (r_i <= c_i).astype(BF16)
    strict_lower = (r_i > c_i).astype(BF16)
    causal = c_i <= r_i
    rhs_cum = jnp.concatenate([strict_lower, ones_m], axis=1)

    for hh in range(ML_PAIR):
        h = pair * ML_PAIR + hh
        li = gr_ref[0, hh] + bif_ref[0, h]
        zf = gr_ref[1, hh] + bif_ref[1, h]
        lf = jnp.minimum(zf, 0.0) - jnp.log1p(jnp.exp(-jnp.abs(zf)))
        brow = _dot3(lf, incl_upper)
        bend = _dot3(lf, ones_m)
        g = bend - brow + li
        maxg = jnp.max(g, axis=-1, keepdims=True)
        m = jnp.zeros((1, L), F32)
        for c in range(nc):
            rf[hh, 1, c:c + 1, :] = m
            m = jnp.maximum(bend[c:c + 1, :] + m, maxg[c:c + 1, :])
            rf[hh, 2, c:c + 1, :] = m
        m_cur = rf[hh, 1, 0:nc, :]
        m_nxt = rf[hh, 2, 0:nc, :]
        rf[hh, 0, 0:nc, :] = li
        rf[hh, 3, 0:nc, :] = jnp.exp(bend + m_cur - m_nxt)
        rf[hh, 4, 0:nc, :] = jnp.exp(g - m_nxt)
        for c in range(nc):
            rows = slice(c * L, (c + 1) * L)
            a_c = jnp.where(causal, jnp.broadcast_to(lf[c:c + 1, :], (L, L)), 0.0)
            both = _dot3(a_c, rhs_cum)
            dcum[hh, rows, :] = both[:, 0:L]
            b1[hh, rows, :] = both[:, L:2 * L]

    ones_v = jnp.ones((L, LANE), BF16)

    def state_free(c):
        rows = slice(c * L, (c + 1) * L)
        kc = kab[rows, :]
        kt = kaf[rows, :].T
        out = []
        for hh in range(ML_PAIR):
            s_ = _dot_nt(qa[hh, rows, :], kc)
            ktw = jnp.where(in_head_row[hh], kt * rf[hh, 4, c:c + 1, :], 0.0).astype(BF16)
            vext = jnp.concatenate([v_ref[rows, hh * LANE:(hh + 1) * LANE], ones_v], axis=1)
            out.append((s_, _dot(ktw, vext), vext))
        return out

    cst = [jnp.zeros((LANE, 2 * LANE), F32) for _ in range(ML_PAIR)]
    nxt = state_free(0)
    for c in range(nc):
        rows = slice(c * L, (c + 1) * L)
        cur = nxt
        if c + 1 < nc:
            nxt = state_free(c + 1)
        for hh in range(ML_PAIR):
            s_, z, vext = cur[hh]
            cols = slice(hh * LANE, (hh + 1) * LANE)
            dm = jnp.where(causal, dcum[hh, rows, :] + rf[hh, 0, c:c + 1, :], NEG)
            a = b1[hh, rows, :] + rf[hh, 1, c:c + 1, :]
            m_t = jnp.maximum(a, jnp.max(dm, axis=-1, keepdims=True))
            w = s_ * jnp.exp(dm - m_t)
            qi = qa[hh, rows, :].astype(F32) * jnp.exp(a - m_t)
            lhs = jnp.concatenate([w.astype(BF16), qi.astype(BF16)], axis=1)
            rhs = jnp.concatenate([vext, cst[hh].astype(BF16)], axis=0)
            tot = _dot(lhs, rhs)
            hval = tot[:, 0:LANE] / jnp.maximum(jnp.abs(tot[:, LANE:2 * LANE]), jnp.exp(-m_t))
            ms = jnp.mean(hval * hval, axis=-1, keepdims=True)
            hn = hval * lax.rsqrt(ms + EPS) * ng_ref[:, cols]
            o_ref[rows, cols] = (hn * jax.nn.sigmoid(mo_ref[rows, cols].astype(F32))).astype(o_ref.dtype)
            dec = rf[hh, 3, c:c + 1, :]
            cst[hh] = jnp.concatenate([dec, dec], axis=1) * cst[hh] + z


def _mlstm(proj3, gates_rows, b_if, conv_w, conv_b, norm_g, B, S):
    nc = S // ML_L
    npair = ML_HEADS // ML_PAIR
    pw = ML_PAIR * LANE

    def col(cb, width):
        return pl.BlockSpec((None, S, width), lambda b, p, cb=cb: (b, 0, cb * LANE // width + p))
    in_specs = [
        pl.BlockSpec(memory_space=pltpu.SMEM),
        col(CB_MQ, LANE), col(CB_MK, LANE), col(CB_MV, pw), col(CB_MO, pw),
        pl.BlockSpec((None, 2, ML_PAIR, nc, ML_L), lambda b, p: (b, 0, p, 0, 0)),
        pl.BlockSpec((CONV_W, LANE), lambda b, p: (0, p)),
        pl.BlockSpec((CONV_W, LANE), lambda b, p: (0, npair + p)),
        pl.BlockSpec((1, LANE), lambda b, p: (0, p)),
        pl.BlockSpec((1, LANE), lambda b, p: (0, npair + p)),
        pl.BlockSpec((1, pw), lambda b, p: (0, p)),
    ]
    scratch = [pltpu.VMEM((8 + S, LANE), F32),
               pltpu.VMEM((ML_PAIR, S, LANE), BF16), pltpu.VMEM((S, LANE), BF16), pltpu.VMEM((S, LANE), F32),
               pltpu.VMEM((ML_PAIR, S, LANE), F32), pltpu.VMEM((ML_PAIR, S, LANE), F32),
               pltpu.VMEM((ML_PAIR, 5, max(nc, 8), ML_L), F32)]
    return pl.pallas_call(
        functools.partial(_mlstm_kernel, S=S),
        grid=(B, npair),
        in_specs=in_specs,
        out_specs=pl.BlockSpec((None, S, pw), lambda b, p: (b, 0, p)),
        out_shape=jax.ShapeDtypeStruct((B, S, ML_V_W), BF16),
        scratch_shapes=scratch,
        compiler_params=pltpu.CompilerParams(
            dimension_semantics=("parallel", "parallel"), vmem_limit_bytes=VMEM_LIMIT),
        name="mlstm",
    )(b_if, proj3, proj3, proj3, proj3, gates_rows, conv_w, conv_w, conv_b, conv_b, norm_g)


TL_TM = 512
TL_TK = 1024


def _tail_kernel(x_ref, ada_ref, att_ref, hml_ref, ga_ref, gm_ref, g2_ref,
                 wa_ref, wm_ref, wo_ref, w1_ref, w2_ref, o_ref):
    y_att = _dot(att_ref[...], wa_ref[...])
    y_ml = _dot(hml_ref[...], wm_ref[...])
    ga = jax.nn.sigmoid(ga_ref[...].astype(F32))
    gm = jax.nn.sigmoid(gm_ref[...].astype(F32))
    mix = (ga * y_att + gm * y_ml).astype(BF16)
    x1 = x_ref[...] + ada_ref[2:3, :] * _dot(mix, wo_ref[...])
    ms = jnp.mean(x1 * x1, axis=-1, keepdims=True)
    y = x1 * lax.rsqrt(ms + EPS) * g2_ref[...]
    u2 = (y * (1.0 + ada_ref[4:5, :]) + ada_ref[3:4, :]).astype(BF16)
    acc = None
    for k in range(D_FF // TL_TK):
        hdn = jnp.maximum(_dot(u2, w1_ref[:, k * TL_TK:(k + 1) * TL_TK]), 0.0)
        part = _dot((hdn * hdn).astype(BF16), w2_ref[k * TL_TK:(k + 1) * TL_TK, :])
        acc = part if acc is None else acc + part
    o_ref[...] = x1 + ada_ref[5:6, :] * acc


def _tail(x2, ada3, att2, hml2, proj2, g2, wa, wm, wo, w1, w2, S):
    T = x2.shape[0]
    tm = TL_TM
    tiles_per_seq = S // tm
    gcb = CB_G * LANE // D_MODEL
    return pl.pallas_call(
        _tail_kernel,
        grid=(T // tm,),
        in_specs=[pl.BlockSpec((tm, D_MODEL), lambda i: (i, 0)),
                  pl.BlockSpec((None, 6, D_MODEL), lambda i: (i // tiles_per_seq, 0, 0)),
                  pl.BlockSpec((tm, ATT_W), lambda i: (i, 0)),
                  pl.BlockSpec((tm, ML_V_W), lambda i: (i, 0)),
                  pl.BlockSpec((tm, D_MODEL), lambda i: (i, gcb)),
                  pl.BlockSpec((tm, D_MODEL), lambda i: (i, gcb + 1)),
                  _resident((1, D_MODEL)),
                  _resident((ATT_W, D_MODEL)),
                  _resident((ML_V_W, D_MODEL)),
                  _resident((D_MODEL, D_MODEL)),
                  _resident((D_MODEL, D_FF)),
                  _resident((D_FF, D_MODEL))],
        out_specs=pl.BlockSpec((tm, D_MODEL), lambda i: (i, 0)),
        out_shape=jax.ShapeDtypeStruct((T, D_MODEL), F32),
        compiler_params=pltpu.CompilerParams(
            dimension_semantics=("parallel",), vmem_limit_bytes=VMEM_LIMIT),
        name="tail",
    )(x2, ada3, att2, hml2, proj2, proj2, g2, wa, wm, wo, w1, w2)


def kernel(x, c, w_ada, b_ada, norm1_g, norm2_g, w_in, b_if, conv_w, conv_b, q_norm_g, k_norm_g,
           rel_bias, mlstm_norm_g, w_att_out, w_ml_out, w_out, w_ff1, w_ff2):
    B, S, D = x.shape
    T = B * S
    depth = w_ada.shape[0]
    bias_tiles = _bias_tiles(rel_bias)
    x2 = x.reshape(T, D)
    for l in range(depth):
        ada3 = _ada(c, w_ada[l].astype(BF16), b_ada[l]).reshape(B, 6, D)
        w_main = jnp.concatenate([w_in[l][:, IF_HI:], w_in[l][:, :IF_LO]], axis=1).astype(BF16)
        w_if_t = w_in[l][:, IF_LO:IF_HI].T.astype(BF16)
        proj2, gates_t = _inproj(x2, ada3, norm1_g[l].reshape(1, D), w_main, w_if_t,
                                 q_norm_g[l].reshape(1, ATT_DH), k_norm_g[l].reshape(1, ATT_DH), S)
        proj3 = proj2.reshape(B, S, PW)
        att = _attention(proj3, bias_tiles, B, S)
        nc = S // ML_L
        gates_rows = gates_t.reshape(2, ML_HEADS, B, nc, ML_L).transpose(2, 0, 1, 3, 4)
        hml = _mlstm(proj3, gates_rows, b_if[l], conv_w[l], conv_b[l].reshape(1, -1),
                     mlstm_norm_g[l].reshape(1, ML_V_W), B, S)
        x2 = _tail(x2, ada3, att.reshape(T, ATT_W), hml.reshape(T, ML_V_W), proj2, norm2_g[l].reshape(1, D),
                   w_att_out[l].astype(BF16), w_ml_out[l].astype(BF16), w_out[l].astype(BF16),
                   w_ff1[l].astype(BF16), w_ff2[l].astype(BF16), S)
    return x2.reshape(B, S, D)
```

```python
import functools
import math

import numpy as np
import jax
import jax.numpy as jnp
from jax import lax
from jax.experimental import pallas as pl
from jax.experimental.pallas import tpu as pltpu

F32 = jnp.float32
BF16 = jnp.bfloat16

D_MODEL = 1024
ATT_GROUPS = ((128, 1), (512, 4), (2048, 16))
N_ATT_GROUPS = 3
ATT_HPG = 4
ATT_DH = 128
N_ATT_HEADS = 12
ATT_W = 512
ATT_BLK = 128
ML_HEADS = 8
ML_DK = 64
ML_DV = 128
ML_QK_W = 512
ML_V_W = 1024
CONV_W = 4
D_FF = 4096
N_BUCKETS = 32
MAX_DISTANCE = 2048
EPS = 1e-6
NEG = -1e30

LANE = 128
ML_L = 128

PW = 9728
CB_G = 0
CB_AQ, CB_AK, CB_AV = 16, 28, 40
CB_MQ, CB_MK, CB_MV, CB_MO = 52, 56, 60, 68
IF_LO, IF_HI = 7680, 7696

VMEM_LIMIT = 56 * 1024 * 1024


def _dot(a, b):
    return jnp.dot(a, b, preferred_element_type=F32)


def _dot_nt(a, b):
    return lax.dot_general(a, b, (((1,), (1,)), ((), ())), preferred_element_type=F32)


def _split3(a):
    hi = a.astype(BF16)
    r1 = a - hi.astype(F32)
    mid = r1.astype(BF16)
    lo = (r1 - mid.astype(F32)).astype(BF16)
    return hi, mid, lo


def _dot3(a, rhs_bf16):
    hi, mid, lo = _split3(a)
    return _dot(hi, rhs_bf16) + _dot(mid, rhs_bf16) + _dot(lo, rhs_bf16)


def _resident(shape):
    return pl.BlockSpec(shape, lambda *_: (0,) * len(shape), pipeline_mode=pl.Buffered(1))


def _ada_kernel(c_ref, w_ref, b_ref, o_ref):
    c = c_ref[...]
    s = c * jax.nn.sigmoid(c)
    o_ref[...] = _dot(s.astype(BF16), w_ref[...]) + b_ref[...]


def _ada(c, w_bf16, b):
    B = c.shape[0]
    n = w_bf16.shape[1]
    tn = 1024
    return pl.pallas_call(
        _ada_kernel,
        grid=(n // tn,),
        in_specs=[pl.BlockSpec((B, D_MODEL), lambda j: (0, 0)),
                  pl.BlockSpec((D_MODEL, tn), lambda j: (0, j)),
                  pl.BlockSpec((1, tn), lambda j: (0, j))],
        out_specs=pl.BlockSpec((B, tn), lambda j: (0, j)),
        out_shape=jax.ShapeDtypeStruct((B, n), F32),
        name="ada",
    )(c, w_bf16, b.reshape(1, n))


def _t5_bucket_np(dist):
    max_exact = N_BUCKETS // 2
    d = np.maximum(dist, max_exact).astype(np.float32)
    large = max_exact + (np.log(d / np.float32(max_exact)) / np.float32(math.log(MAX_DISTANCE / max_exact))
                         * np.float32(N_BUCKETS - max_exact)).astype(np.int32)
    large = np.minimum(large, N_BUCKETS - 1)
    return np.where(dist < max_exact, dist, large).astype(np.int32)


def _bucket_tiles():
    i = np.arange(ATT_BLK)[:, None]
    j = np.arange(2 * ATT_BLK)[None, :]
    delta = ATT_BLK + i - j
    return np.stack([_t5_bucket_np(np.maximum(delta, 0) * dil) for _, dil in ATT_GROUPS])


def _bias_kernel(tab_ref, bucket_ref, o_ref):
    hh = pl.program_id(0)
    bucket = bucket_ref[...]
    acc = jnp.zeros(bucket.shape, F32)
    for k in range(N_BUCKETS):
        acc = jnp.where(bucket == k, tab_ref[k, hh], acc)
    i = lax.broadcasted_iota(jnp.int32, bucket.shape, 0)
    j = lax.broadcasted_iota(jnp.int32, bucket.shape, 1)
    delta = ATT_BLK + i - j
    valid = (delta >= 0) & (delta <= ATT_BLK)
    o_ref[0] = jnp.where(valid, acc, NEG)
    o_ref[1] = jnp.where(valid & (j >= ATT_BLK), acc, NEG)


def _bias_tiles(rel_bias):
    buckets = jnp.asarray(_bucket_tiles())
    return pl.pallas_call(
        _bias_kernel,
        grid=(N_ATT_HEADS,),
        in_specs=[pl.BlockSpec(memory_space=pltpu.SMEM),
                  pl.BlockSpec((None, ATT_BLK, 2 * ATT_BLK), lambda h: (h // ATT_HPG, 0, 0))],
        out_specs=pl.BlockSpec((None, 2, ATT_BLK, 2 * ATT_BLK), lambda h: (h, 0, 0, 0)),
        out_shape=jax.ShapeDtypeStruct((N_ATT_HEADS, 2, ATT_BLK, 2 * ATT_BLK), F32),
        name="bias_tiles",
    )(rel_bias, buckets)


IP_TM = 512
IP_TN = 512
IP_Q0 = CB_AQ * LANE // IP_TN
IP_K0 = CB_AK * LANE // IP_TN
IP_V0 = CB_AV * LANE // IP_TN


def _inproj_kernel(x_ref, ada_ref, g1_ref, w_ref, wif_ref, qg_ref, kg_ref, o_ref, gt_ref):
    x = x_ref[...]
    ms = jnp.mean(x * x, axis=-1, keepdims=True)
    y = x * lax.rsqrt(ms + EPS) * g1_ref[...]
    ub = (y * (1.0 + ada_ref[1:2, :]) + ada_ref[0:1, :]).astype(BF16)
    gt_ref[...] = _dot_nt(wif_ref[...], ub)
    for j in range(PW // IP_TN):
        acc = _dot(ub, w_ref[:, j * IP_TN:(j + 1) * IP_TN])
        if IP_Q0 <= j < IP_V0:
            gain = qg_ref[...] if j < IP_K0 else kg_ref[...]
            for k in range(IP_TN // ATT_DH):
                a = acc[:, k * ATT_DH:(k + 1) * ATT_DH]
                ms = jnp.mean(a * a, axis=-1, keepdims=True)
                lo = j * IP_TN + k * ATT_DH
                o_ref[:, lo:lo + ATT_DH] = (a * lax.rsqrt(ms + EPS) * gain).astype(o_ref.dtype)
        else:
            o_ref[:, j * IP_TN:(j + 1) * IP_TN] = acc.astype(o_ref.dtype)


def _inproj(x2, ada3, g1, w_main, w_if_t, qg, kg, S):
    T = x2.shape[0]
    tm = IP_TM
    tiles_per_seq = S // tm
    return pl.pallas_call(
        _inproj_kernel,
        grid=(T // tm,),
        in_specs=[pl.BlockSpec((tm, D_MODEL), lambda i: (i, 0)),
                  pl.BlockSpec((None, 6, D_MODEL), lambda i: (i // tiles_per_seq, 0, 0)),
                  _resident((1, D_MODEL)),
                  _resident((D_MODEL, PW)),
                  _resident((16, D_MODEL)),
                  _resident((1, ATT_DH)),
                  _resident((1, ATT_DH))],
        out_specs=[pl.BlockSpec((tm, PW), lambda i: (i, 0)),
                   pl.BlockSpec((None, 16, tm), lambda i: (i // tiles_per_seq, 0, i % tiles_per_seq))],
        out_shape=[jax.ShapeDtypeStruct((T, PW), BF16),
                   jax.ShapeDtypeStruct((T // S, 16, S), F32)],
        compiler_params=pltpu.CompilerParams(
            dimension_semantics=("parallel",), vmem_limit_bytes=VMEM_LIMIT),
        name="inproj",
    )(x2, ada3, g1, w_main, w_if_t, qg, kg)


ATT_UNROLL = 8


def _attn_kernel(q0_ref, k0_ref, v0_ref, q1_ref, k1_ref, v1_ref, q2_ref, k2_ref, v2_ref,
                 bias_ref, o_ref,
                 q1f, k1f, v1f, q2f, k2f, v2f, o_scr, l_scr, *, S):
    scale = ATT_DH ** -0.5
    blk = ATT_BLK

    pad1 = blk * ATT_GROUPS[1][1]
    k1f[0:pad1, :] = jnp.zeros((pad1, LANE), F32)
    v1f[0:pad1, :] = jnp.zeros((pad1, LANE), F32)
    k1f[pad1:pad1 + S, :] = k1_ref[...].astype(F32)
    v1f[pad1:pad1 + S, :] = v1_ref[...].astype(F32)
    q1f[...] = q1_ref[...].astype(F32)
    q2f[...] = q2_ref[...].astype(F32)
    k2f[...] = k2_ref[...].astype(F32)
    v2f[...] = v2_ref[...].astype(F32)

    def softmax_blocks(ops):
        ss = [_dot_nt(q, kk) * scale + bias for (q, kk, _, bias) in ops]
        ms = [jnp.max(s, axis=-1, keepdims=True) for s in ss]
        ps = [jnp.exp(s - m) for s, m in zip(ss, ms)]
        ls = [jnp.sum(p, axis=-1, keepdims=True) for p in ps]
        accs = [_dot(p.astype(BF16), op[2]) for p, op in zip(ps, ops)]
        return [(acc / l, m + jnp.log(l)) for acc, l, m in zip(accs, ls, ms)]

    def run_dilated(g, r, n_iters, fetch):
        def body(it, carry):
            fetched = [fetch(it, u) for u in range(ATT_UNROLL)]
            res = softmax_blocks([f[:4] for f in fetched])
            for (o, lse), f in zip(res, fetched):
                rows = pl.ds(f[4], blk, stride=r)
                o_scr[g - 1, rows, :] = o
                l_scr[g - 1, rows, :] = jnp.broadcast_to(lse, (blk, LANE))
            return carry
        lax.fori_loop(0, n_iters, body, 0)

    r1 = ATT_GROUPS[1][1]
    nb1 = S // r1 // blk
    assert ATT_UNROLL % nb1 == 0

    def fetch1(it, u):
        rho = it * (ATT_UNROLL // nb1) + u // nb1
        n = u % nb1
        start = rho + r1 * blk * n
        return (q1f[pl.ds(start, blk, stride=r1), :].astype(BF16),
                k1f[pl.ds(start, 2 * blk, stride=r1), :].astype(BF16),
                v1f[pl.ds(start, 2 * blk, stride=r1), :].astype(BF16),
                bias_ref[1, 1 if n == 0 else 0], start)
    run_dilated(1, r1, r1 * nb1 // ATT_UNROLL, fetch1)

    r2 = ATT_GROUPS[2][1]
    assert S // r2 == blk

    def fetch2(it, u):
        rho = it * ATT_UNROLL + u
        return (q2f[pl.ds(rho, blk, stride=r2), :].astype(BF16),
                k2f[pl.ds(rho, blk, stride=r2), :].astype(BF16),
                v2f[pl.ds(rho, blk, stride=r2), :].astype(BF16),
                bias_ref[2, 1][:, blk:2 * blk], rho)
    run_dilated(2, r2, r2 // ATT_UNROLL, fetch2)

    for it in range(S // blk // ATT_UNROLL):
        ops = []
        for u in range(ATT_UNROLL):
            n = it * ATT_UNROLL + u
            q = q0_ref[n * blk:(n + 1) * blk, :]
            if n == 0:
                ops.append((q, k0_ref[0:blk, :], v0_ref[0:blk, :], bias_ref[0, 1][:, blk:2 * blk]))
            else:
                ops.append((q, k0_ref[(n - 1) * blk:(n + 1) * blk, :], v0_ref[(n - 1) * blk:(n + 1) * blk, :],
                            bias_ref[0, 0]))
        for u, (o0, lse0) in enumerate(softmax_blocks(ops)):
            n = it * ATT_UNROLL + u
            rows = slice(n * blk, (n + 1) * blk)
            l1, l2 = l_scr[0, rows, :], l_scr[1, rows, :]
            mx = jnp.maximum(jnp.maximum(lse0, l1), l2)
            e0, e1, e2 = jnp.exp(lse0 - mx), jnp.exp(l1 - mx), jnp.exp(l2 - mx)
            att = (e0 * o0 + e1 * o_scr[0, rows, :] + e2 * o_scr[1, rows, :]) / (e0 + e1 + e2)
            o_ref[rows, :] = att.astype(o_ref.dtype)


def _attention(proj3, bias_tiles, B, S):
    def col(cb):
        return pl.BlockSpec((None, S, LANE), lambda b, h, cb=cb: (b, 0, cb + h))
    in_specs = []
    for g in range(N_ATT_GROUPS):
        for base in (CB_AQ, CB_AK, CB_AV):
            in_specs.append(col(base + g * ATT_HPG))
    in_specs.append(pl.BlockSpec((N_ATT_GROUPS, None, 2, ATT_BLK, 2 * ATT_BLK), lambda b, h: (0, h, 0, 0, 0)))
    pad1 = ATT_BLK * ATT_GROUPS[1][1]
    scratch = [pltpu.VMEM((S, LANE), F32), pltpu.VMEM((pad1 + S, LANE), F32), pltpu.VMEM((pad1 + S, LANE), F32),
               pltpu.VMEM((S, LANE), F32), pltpu.VMEM((S, LANE), F32), pltpu.VMEM((S, LANE), F32),
               pltpu.VMEM((N_ATT_GROUPS - 1, S, LANE), F32), pltpu.VMEM((N_ATT_GROUPS - 1, S, LANE), F32)]
    bias5 = bias_tiles.reshape(N_ATT_GROUPS, ATT_HPG, 2, ATT_BLK, 2 * ATT_BLK)
    args = [proj3] * 9 + [bias5]
    return pl.pallas_call(
        functools.partial(_attn_kernel, S=S),
        grid=(B, ATT_HPG),
        in_specs=in_specs,
        out_specs=pl.BlockSpec((None, S, LANE), lambda b, h: (b, 0, h)),
        out_shape=jax.ShapeDtypeStruct((B, S, ATT_W), BF16),
        scratch_shapes=scratch,
        compiler_params=pltpu.CompilerParams(
            dimension_semantics=("parallel", "parallel"), vmem_limit_bytes=VMEM_LIMIT),
        name="attn",
    )(*args)


ML_PAIR = 2


def _mlstm_kernel(bif_ref, q2_ref, k2_ref, v_ref, mo_ref, gr_ref, cwq_ref, cwk_ref, cbq_ref, cbk_ref,
                  ng_ref, o_ref, xs, qa, kab, kaf, rf, *, S):
    L = ML_L
    nc = S // L
    pair = pl.program_id(1)
    lane = lax.broadcasted_iota(jnp.int32, (1, LANE), 1)
    srow = lax.broadcasted_iota(jnp.int32, (LANE, 1), 0)
    in_head_lane = [(lane >= hh * ML_DK) & (lane < (hh + 1) * ML_DK) for hh in range(ML_PAIR)]
    in_head_row = [(srow >= hh * ML_DK) & (srow < (hh + 1) * ML_DK) for hh in range(ML_PAIR)]

    def conv_silu(src_ref, w_ref, b_ref):
        xs[0:8, :] = jnp.zeros((8, LANE), F32)
        xs[8:8 + S, :] = src_ref[...].astype(F32)
        y = b_ref[...]
        for jj in range(CONV_W):
            y = y + w_ref[jj:jj + 1, :] * xs[pl.ds(8 - (CONV_W - 1) + jj, S), :]
        return y * jax.nn.sigmoid(y)

    yq = conv_silu(q2_ref, cwq_ref, cbq_ref)
    for hh in range(ML_PAIR):
        qa[hh] = jnp.where(in_head_lane[hh], yq, 0.0).astype(BF16)
    yk = conv_silu(k2_ref, cwk_ref, cbk_ref) * (ML_DK ** -0.5)
    kaf[...] = yk
    kab[...] = yk.astype(BF16)

    r_i = lax.broadcasted_iota(jnp.int32, (L, L), 0)
    c_i = lax.broadcasted_iota(jnp.int32, (L, L), 1)
    ones_m = jnp.ones((L, L), BF16)
    incl_upper = (r_i <= c_i).astype(BF16)
    causal = c_i <= r_i
    nr = ML_PAIR * nc
    lane_nr = lax.broadcasted_iota(jnp.int32, (nr, L), 1)
    pad_rows = jnp.zeros((L - nr, L), F32)

    def col_form(row_form):
        return jnp.concatenate([row_form, pad_rows], axis=0).T

    h0 = pair * ML_PAIR
    li = jnp.concatenate([gr_ref[0, hh] + bif_ref[0, h0 + hh] for hh in range(ML_PAIR)], axis=0)
    zf = jnp.concatenate([gr_ref[1, hh] + bif_ref[1, h0 + hh] for hh in range(ML_PAIR)], axis=0)
    lf = jnp.minimum(zf, 0.0) - jnp.log1p(jnp.exp(-jnp.abs(zf)))
    brow = _dot3(lf, incl_upper)
    bend = _dot3(lf, ones_m)
    u = brow - li
    g = bend - u
    maxg = jnp.max(g, axis=-1, keepdims=True)
    m = [jnp.zeros((1, L), F32) for _ in range(ML_PAIR)]
    for c in range(nc):
        for hh in range(ML_PAIR):
            r = hh * nc + c
            rf[1, r:r + 1, :] = m[hh]
            m[hh] = jnp.maximum(bend[r:r + 1, :] + m[hh], maxg[r:r + 1, :])
            rf[2, r:r + 1, :] = m[hh]
    m_cur = rf[1, 0:nr, :]
    m_nxt = rf[2, 0:nr, :]
    rf[0, 0:nr, :] = u
    rf[3, 0:nr, :] = jnp.exp(bend + m_cur - m_nxt)
    rf[4, 0:nr, :] = jnp.exp(g - m_nxt)
    pm = -u
    sh = 1
    while sh < L:
        pm = jnp.maximum(pm, jnp.where(lane_nr >= sh, pltpu.roll(pm, sh, axis=1), NEG))
        sh *= 2
    d1 = -jnp.maximum(m_cur, pm)
    dcol = col_form(d1)
    ecol = col_form(jnp.exp(d1 - brow))

    ones_v = jnp.ones((L, LANE), BF16)
    head0_rows = in_head_row[0]

    def prepare(c):
        rows = slice(c * L, (c + 1) * L)
        kc = kab[rows, :]
        kt = kaf[rows, :].T
        wk = jnp.where(head0_rows, rf[4, c:c + 1, :], rf[4, nc + c:nc + c + 1, :])
        ktw = (kt * wk).astype(BF16)
        ktw2 = jnp.concatenate([jnp.where(in_head_row[hh], ktw, jnp.zeros_like(ktw)) for hh in range(ML_PAIR)],
                               axis=1)
        vexts = [jnp.concatenate([v_ref[rows, hh * LANE:(hh + 1) * LANE], ones_v], axis=1)
                 for hh in range(ML_PAIR)]
        z = _dot(ktw2, jnp.concatenate(vexts, axis=0))
        ss = [_dot_nt(qa[hh, rows, :], kc) for hh in range(ML_PAIR)]
        drow = [jnp.broadcast_to(dcol[:, hh * nc + c:hh * nc + c + 1], (L, L)) for hh in range(ML_PAIR)]
        erow = [jnp.broadcast_to(ecol[:, hh * nc + c:hh * nc + c + 1], (L, L)) for hh in range(ML_PAIR)]
        return ss, z, vexts, drow, erow

    cst = jnp.zeros((LANE, 2 * LANE), F32)
    nxt = prepare(0)
    for c in range(nc):
        rows = slice(c * L, (c + 1) * L)
        ss, z, vexts, drow, erow = nxt
        if c + 1 < nc:
            nxt = prepare(c + 1)
        cst_b = cst.astype(BF16)
        for hh in range(ML_PAIR):
            r = hh * nc + c
            cols = slice(hh * LANE, (hh + 1) * LANE)
            w = ss[hh] * jnp.exp(jnp.where(causal, drow[hh] - rf[0, r:r + 1, :], NEG))
            qi = qa[hh, rows, :].astype(F32) * jnp.exp(drow[hh] + rf[1, r:r + 1, :])
            lhs = jnp.concatenate([w.astype(BF16), qi.astype(BF16)], axis=1)
            tot = _dot(lhs, jnp.concatenate([vexts[hh], cst_b], axis=0))
            hval = tot[:, 0:LANE] / jnp.maximum(jnp.abs(tot[:, LANE:2 * LANE]), erow[hh])
            ms = jnp.mean(hval * hval, axis=-1, keepdims=True)
            hn = hval * lax.rsqrt(ms + EPS) * ng_ref[:, cols]
            o_ref[rows, cols] = (hn * jax.nn.sigmoid(mo_ref[rows, cols].astype(F32))).astype(o_ref.dtype)
        dec = jnp.where(head0_rows, rf[3, c:c + 1, :], rf[3, nc + c:nc + c + 1, :])
        cst = jnp.concatenate([dec, dec], axis=1) * cst + z


def _mlstm(proj3, gates_rows, b_if, conv_w, conv_b, norm_g, B, S):
    nc = S // ML_L
    npair = ML_HEADS // ML_PAIR
    pw = ML_PAIR * LANE

    def col(cb, width):
        return pl.BlockSpec((None, S, width), lambda b, p, cb=cb: (b, 0, cb * LANE // width + p))
    in_specs = [
        pl.BlockSpec(memory_space=pltpu.SMEM),
        col(CB_MQ, LANE), col(CB_MK, LANE), col(CB_MV, pw), col(CB_MO, pw),
        pl.BlockSpec((None, 2, ML_PAIR, nc, ML_L), lambda b, p: (b, 0, p, 0, 0)),
        pl.BlockSpec((CONV_W, LANE), lambda b, p: (0, p)),
        pl.BlockSpec((CONV_W, LANE), lambda b, p: (0, npair + p)),
        pl.BlockSpec((1, LANE), lambda b, p: (0, p)),
        pl.BlockSpec((1, LANE), lambda b, p: (0, npair + p)),
        pl.BlockSpec((1, pw), lambda b, p: (0, p)),
    ]
    scratch = [pltpu.VMEM((8 + S, LANE), F32),
               pltpu.VMEM((ML_PAIR, S, LANE), BF16), pltpu.VMEM((S, LANE), BF16), pltpu.VMEM((S, LANE), F32),
               pltpu.VMEM((5, ML_PAIR * nc, ML_L), F32)]
    return pl.pallas_call(
        functools.partial(_mlstm_kernel, S=S),
        grid=(B, npair),
        in_specs=in_specs,
        out_specs=pl.BlockSpec((None, S, pw), lambda b, p: (b, 0, p)),
        out_shape=jax.ShapeDtypeStruct((B, S, ML_V_W), BF16),
        scratch_shapes=scratch,
        compiler_params=pltpu.CompilerParams(
            dimension_semantics=("parallel", "parallel"), vmem_limit_bytes=VMEM_LIMIT),
        name="mlstm",
    )(b_if, proj3, proj3, proj3, proj3, gates_rows, conv_w, conv_w, conv_b, conv_b, norm_g)


TL_TM = 512
TL_TK = 1024


def _tail_kernel(x_ref, ada_ref, att_ref, hml_ref, ga_ref, gm_ref, g2_ref,
                 wa_ref, wm_ref, wo_ref, w1_ref, w2_ref, o_ref):
    y_att = _dot(att_ref[...], wa_ref[...])
    y_ml = _dot(hml_ref[...], wm_ref[...])
    ga = jax.nn.sigmoid(ga_ref[...].astype(F32))
    gm = jax.nn.sigmoid(gm_ref[...].astype(F32))
    mix = (ga * y_att + gm * y_ml).astype(BF16)
    x1 = x_ref[...] + ada_ref[2:3, :] * _dot(mix, wo_ref[...])
    ms = jnp.mean(x1 * x1, axis=-1, keepdims=True)
    y = x1 * lax.rsqrt(ms + EPS) * g2_ref[...]
    u2 = (y * (1.0 + ada_ref[4:5, :]) + ada_ref[3:4, :]).astype(BF16)
    acc = None
    for k in range(D_FF // TL_TK):
        hdn = jnp.maximum(_dot(u2, w1_ref[:, k * TL_TK:(k + 1) * TL_TK]), 0.0)
        part = _dot((hdn * hdn).astype(BF16), w2_ref[k * TL_TK:(k + 1) * TL_TK, :])
        acc = part if acc is None else acc + part
    o_ref[...] = x1 + ada_ref[5:6, :] * acc


def _tail(x2, ada3, att2, hml2, proj2, g2, wa, wm, wo, w1, w2, S):
    T = x2.shape[0]
    tm = TL_TM
    tiles_per_seq = S // tm
    gcb = CB_G * LANE // D_MODEL
    return pl.pallas_call(
        _tail_kernel,
        grid=(T // tm,),
        in_specs=[pl.BlockSpec((tm, D_MODEL), lambda i: (i, 0)),
                  pl.BlockSpec((None, 6, D_MODEL), lambda i: (i // tiles_per_seq, 0, 0)),
                  pl.BlockSpec((tm, ATT_W), lambda i: (i, 0)),
                  pl.BlockSpec((tm, ML_V_W), lambda i: (i, 0)),
                  pl.BlockSpec((tm, D_MODEL), lambda i: (i, gcb)),
                  pl.BlockSpec((tm, D_MODEL), lambda i: (i, gcb + 1)),
                  _resident((1, D_MODEL)),
                  _resident((ATT_W, D_MODEL)),
                  _resident((ML_V_W, D_MODEL)),
                  _resident((D_MODEL, D_MODEL)),
                  _resident((D_MODEL, D_FF)),
                  _resident((D_FF, D_MODEL))],
        out_specs=pl.BlockSpec((tm, D_MODEL), lambda i: (i, 0)),
        out_shape=jax.ShapeDtypeStruct((T, D_MODEL), F32),
        compiler_params=pltpu.CompilerParams(
            dimension_semantics=("parallel",), vmem_limit_bytes=VMEM_LIMIT),
        name="tail",
    )(x2, ada3, att2, hml2, proj2, proj2, g2, wa, wm, wo, w1, w2)


def kernel(x, c, w_ada, b_ada, norm1_g, norm2_g, w_in, b_if, conv_w, conv_b, q_norm_g, k_norm_g,
           rel_bias, mlstm_norm_g, w_att_out, w_ml_out, w_out, w_ff1, w_ff2):
    B, S, D = x.shape
    T = B * S
    depth = w_ada.shape[0]
    bias_tiles = _bias_tiles(rel_bias)
    x2 = x.reshape(T, D)
    for l in range(depth):
        ada3 = _ada(c, w_ada[l].astype(BF16), b_ada[l]).reshape(B, 6, D)
        w_main = jnp.concatenate([w_in[l][:, IF_HI:], w_in[l][:, :IF_LO]], axis=1).astype(BF16)
        w_if_t = w_in[l][:, IF_LO:IF_HI].T.astype(BF16)
        proj2, gates_t = _inproj(x2, ada3, norm1_g[l].reshape(1, D), w_main, w_if_t,
                                 q_norm_g[l].reshape(1, ATT_DH), k_norm_g[l].reshape(1, ATT_DH), S)
        proj3 = proj2.reshape(B, S, PW)
        att = _attention(proj3, bias_tiles, B, S)
        nc = S // ML_L
        gates_rows = gates_t.reshape(B, 2, ML_HEADS, nc, ML_L)
        hml = _mlstm(proj3, gates_rows, b_if[l], conv_w[l], conv_b[l].reshape(1, -1),
                     mlstm_norm_g[l].reshape(1, ML_V_W), B, S)
        x2 = _tail(x2, ada3, att.reshape(T, ATT_W), hml.reshape(T, ML_V_W), proj2, norm2_g[l].reshape(1, D),
                   w_att_out[l].astype(BF16), w_ml_out[l].astype(BF16), w_out[l].astype(BF16),
                   w_ff1[l].astype(BF16), w_ff2[l].astype(BF16), S)
    return x2.reshape(B, S, D)
```

```python
import functools
import math

import numpy as np
import jax
import jax.numpy as jnp
from jax import lax
from jax.experimental import pallas as pl
from jax.experimental.pallas import tpu as pltpu

F32 = jnp.float32
BF16 = jnp.bfloat16

D_MODEL = 1024
ATT_GROUPS = ((128, 1), (512, 4), (2048, 16))
N_ATT_GROUPS = 3
ATT_HPG = 4
ATT_DH = 128
N_ATT_HEADS = 12
ATT_W = 512
ATT_BLK = 128
ML_HEADS = 8
ML_DK = 64
ML_DV = 128
ML_QK_W = 512
ML_V_W = 1024
CONV_W = 4
D_FF = 4096
N_BUCKETS = 32
MAX_DISTANCE = 2048
EPS = 1e-6
NEG = -1e30

LANE = 128
ML_L = 128

PW = 9728
CB_G, CB_MV, CB_MO = 0, 16, 24
CB_AQ, CB_AK, CB_AV = 32, 44, 56
CB_MQ, CB_MK = 68, 72
WIN_ATT, WIN_MQK, WIN_MVO, WIN_IF, WIN_G = (0, 4608), (4608, 5632), (5632, 7680), (7680, 7696), (7696, 9744)

VMEM_LIMIT = 56 * 1024 * 1024


def _dot(a, b):
    return jnp.dot(a, b, preferred_element_type=F32)


def _dot_nt(a, b):
    return lax.dot_general(a, b, (((1,), (1,)), ((), ())), preferred_element_type=F32)


def _split3(a):
    hi = a.astype(BF16)
    r1 = a - hi.astype(F32)
    mid = r1.astype(BF16)
    lo = (r1 - mid.astype(F32)).astype(BF16)
    return hi, mid, lo


def _dot3(a, rhs_bf16):
    hi, mid, lo = _split3(a)
    return _dot(hi, rhs_bf16) + _dot(mid, rhs_bf16) + _dot(lo, rhs_bf16)


def _resident(shape):
    return pl.BlockSpec(shape, lambda *_: (0,) * len(shape), pipeline_mode=pl.Buffered(1))


def _ada_kernel(c_ref, w_ref, b_ref, o_ref):
    c = c_ref[...]
    s = c * jax.nn.sigmoid(c)
    o_ref[...] = _dot(s.astype(BF16), w_ref[...]) + b_ref[...]


def _ada(c, w_bf16, b):
    B = c.shape[0]
    n = w_bf16.shape[1]
    tn = 1024
    return pl.pallas_call(
        _ada_kernel,
        grid=(n // tn,),
        in_specs=[pl.BlockSpec((B, D_MODEL), lambda j: (0, 0)),
                  pl.BlockSpec((D_MODEL, tn), lambda j: (0, j)),
                  pl.BlockSpec((1, tn), lambda j: (0, j))],
        out_specs=pl.BlockSpec((B, tn), lambda j: (0, j)),
        out_shape=jax.ShapeDtypeStruct((B, n), F32),
        name="ada",
    )(c, w_bf16, b.reshape(1, n))


def _t5_bucket_np(dist):
    max_exact = N_BUCKETS // 2
    d = np.maximum(dist, max_exact).astype(np.float32)
    large = max_exact + (np.log(d / np.float32(max_exact)) / np.float32(math.log(MAX_DISTANCE / max_exact))
                         * np.float32(N_BUCKETS - max_exact)).astype(np.int32)
    large = np.minimum(large, N_BUCKETS - 1)
    return np.where(dist < max_exact, dist, large).astype(np.int32)


def _bucket_tiles():
    i = np.arange(ATT_BLK)[:, None]
    j = np.arange(2 * ATT_BLK)[None, :]
    delta = ATT_BLK + i - j
    return np.stack([_t5_bucket_np(np.maximum(delta, 0) * dil) for _, dil in ATT_GROUPS])


def _bias_kernel(tab_ref, bucket_ref, o_ref):
    hh = pl.program_id(0)
    bucket = bucket_ref[...]
    acc = jnp.zeros(bucket.shape, F32)
    for k in range(N_BUCKETS):
        acc = jnp.where(bucket == k, tab_ref[k, hh], acc)
    i = lax.broadcasted_iota(jnp.int32, bucket.shape, 0)
    j = lax.broadcasted_iota(jnp.int32, bucket.shape, 1)
    delta = ATT_BLK + i - j
    valid = (delta >= 0) & (delta <= ATT_BLK)
    o_ref[0] = jnp.where(valid, acc, NEG)
    o_ref[1] = jnp.where(valid & (j >= ATT_BLK), acc, NEG)


def _bias_tiles(rel_bias):
    buckets = jnp.asarray(_bucket_tiles())
    return pl.pallas_call(
        _bias_kernel,
        grid=(N_ATT_HEADS,),
        in_specs=[pl.BlockSpec(memory_space=pltpu.SMEM),
                  pl.BlockSpec((None, ATT_BLK, 2 * ATT_BLK), lambda h: (h // ATT_HPG, 0, 0))],
        out_specs=pl.BlockSpec((None, 2, ATT_BLK, 2 * ATT_BLK), lambda h: (h, 0, 0, 0)),
        out_shape=jax.ShapeDtypeStruct((N_ATT_HEADS, 2, ATT_BLK, 2 * ATT_BLK), F32),
        name="bias_tiles",
    )(rel_bias, buckets)


IP_TM = 512
IP_TN = 512
IP_Q0 = CB_AQ * LANE // IP_TN
IP_K0 = CB_AK * LANE // IP_TN
IP_V0 = CB_AV * LANE // IP_TN


def _inproj_kernel(x_ref, ada_ref, g1_ref, w_ref, wif_ref, qg_ref, kg_ref, o_ref, gt_ref):
    x = x_ref[...]
    ms = jnp.mean(x * x, axis=-1, keepdims=True)
    y = x * lax.rsqrt(ms + EPS) * g1_ref[...]
    ub = (y * (1.0 + ada_ref[1:2, :]) + ada_ref[0:1, :]).astype(BF16)
    gt_ref[...] = _dot_nt(wif_ref[...], ub)
    for j in range(PW // IP_TN):
        acc = _dot(ub, w_ref[:, j * IP_TN:(j + 1) * IP_TN])
        if IP_Q0 <= j < IP_V0:
            gain = qg_ref[...] if j < IP_K0 else kg_ref[...]
            for k in range(IP_TN // ATT_DH):
                a = acc[:, k * ATT_DH:(k + 1) * ATT_DH]
                ms = jnp.mean(a * a, axis=-1, keepdims=True)
                lo = j * IP_TN + k * ATT_DH
                o_ref[:, lo:lo + ATT_DH] = (a * lax.rsqrt(ms + EPS) * gain).astype(o_ref.dtype)
        else:
            o_ref[:, j * IP_TN:(j + 1) * IP_TN] = acc.astype(o_ref.dtype)


def _inproj(x2, ada3, g1, w_main, w_if_t, qg, kg, S):
    T = x2.shape[0]
    tm = IP_TM
    tiles_per_seq = S // tm
    return pl.pallas_call(
        _inproj_kernel,
        grid=(T // tm,),
        in_specs=[pl.BlockSpec((tm, D_MODEL), lambda i: (i, 0)),
                  pl.BlockSpec((None, 6, D_MODEL), lambda i: (i // tiles_per_seq, 0, 0)),
                  _resident((1, D_MODEL)),
                  _resident((D_MODEL, PW)),
                  _resident((16, D_MODEL)),
                  _resident((1, ATT_DH)),
                  _resident((1, ATT_DH))],
        out_specs=[pl.BlockSpec((tm, PW), lambda i: (i, 0)),
                   pl.BlockSpec((None, 16, tm), lambda i: (i // tiles_per_seq, 0, i % tiles_per_seq))],
        out_shape=[jax.ShapeDtypeStruct((T, PW), BF16),
                   jax.ShapeDtypeStruct((T // S, 16, S), F32)],
        compiler_params=pltpu.CompilerParams(
            dimension_semantics=("parallel",), vmem_limit_bytes=VMEM_LIMIT),
        name="inproj",
    )(x2, ada3, g1, w_main, w_if_t, qg, kg)


ATT_UNROLL = 8


def _attn_kernel(q0_ref, k0_ref, v0_ref, q1_ref, k1_ref, v1_ref, q2_ref, k2_ref, v2_ref,
                 bias_ref, o_ref,
                 q1f, k1f, v1f, q2f, k2f, v2f, o_scr, l_scr, *, S):
    scale = ATT_DH ** -0.5
    blk = ATT_BLK

    pad1 = blk * ATT_GROUPS[1][1]
    k1f[0:pad1, :] = jnp.zeros((pad1, LANE), F32)
    v1f[0:pad1, :] = jnp.zeros((pad1, LANE), F32)
    k1f[pad1:pad1 + S, :] = k1_ref[...].astype(F32)
    v1f[pad1:pad1 + S, :] = v1_ref[...].astype(F32)
    q1f[...] = q1_ref[...].astype(F32)
    q2f[...] = q2_ref[...].astype(F32)
    k2f[...] = k2_ref[...].astype(F32)
    v2f[...] = v2_ref[...].astype(F32)

    def softmax_blocks(ops):
        ss = [_dot_nt(q, kk) * scale + bias for (q, kk, _, bias) in ops]
        ms = [jnp.max(s, axis=-1, keepdims=True) for s in ss]
        ps = [jnp.exp(s - m) for s, m in zip(ss, ms)]
        ls = [jnp.sum(p, axis=-1, keepdims=True) for p in ps]
        accs = [_dot(p.astype(BF16), op[2]) for p, op in zip(ps, ops)]
        return [(acc / l, m + jnp.log(l)) for acc, l, m in zip(accs, ls, ms)]

    def run_dilated(g, r, n_iters, fetch):
        def body(it, carry):
            fetched = [fetch(it, u) for u in range(ATT_UNROLL)]
            res = softmax_blocks([f[:4] for f in fetched])
            for (o, lse), f in zip(res, fetched):
                rows = pl.ds(f[4], blk, stride=r)
                o_scr[g - 1, rows, :] = o
                l_scr[g - 1, rows, :] = jnp.broadcast_to(lse, (blk, LANE))
            return carry
        lax.fori_loop(0, n_iters, body, 0)

    r1 = ATT_GROUPS[1][1]
    nb1 = S // r1 // blk
    assert ATT_UNROLL % nb1 == 0

    def fetch1(it, u):
        rho = it * (ATT_UNROLL // nb1) + u // nb1
        n = u % nb1
        start = rho + r1 * blk * n
        return (q1f[pl.ds(start, blk, stride=r1), :].astype(BF16),
                k1f[pl.ds(start, 2 * blk, stride=r1), :].astype(BF16),
                v1f[pl.ds(start, 2 * blk, stride=r1), :].astype(BF16),
                bias_ref[1, 1 if n == 0 else 0], start)
    run_dilated(1, r1, r1 * nb1 // ATT_UNROLL, fetch1)

    r2 = ATT_GROUPS[2][1]
    assert S // r2 == blk

    def fetch2(it, u):
        rho = it * ATT_UNROLL + u
        return (q2f[pl.ds(rho, blk, stride=r2), :].astype(BF16),
                k2f[pl.ds(rho, blk, stride=r2), :].astype(BF16),
                v2f[pl.ds(rho, blk, stride=r2), :].astype(BF16),
                bias_ref[2, 1][:, blk:2 * blk], rho)
    run_dilated(2, r2, r2 // ATT_UNROLL, fetch2)

    for it in range(S // blk // ATT_UNROLL):
        ops = []
        for u in range(ATT_UNROLL):
            n = it * ATT_UNROLL + u
            q = q0_ref[n * blk:(n + 1) * blk, :]
            if n == 0:
                ops.append((q, k0_ref[0:blk, :], v0_ref[0:blk, :], bias_ref[0, 1][:, blk:2 * blk]))
            else:
                ops.append((q, k0_ref[(n - 1) * blk:(n + 1) * blk, :], v0_ref[(n - 1) * blk:(n + 1) * blk, :],
                            bias_ref[0, 0]))
        for u, (o0, lse0) in enumerate(softmax_blocks(ops)):
            n = it * ATT_UNROLL + u
            rows = slice(n * blk, (n + 1) * blk)
            l1, l2 = l_scr[0, rows, :], l_scr[1, rows, :]
            mx = jnp.maximum(jnp.maximum(lse0, l1), l2)
            e0, e1, e2 = jnp.exp(lse0 - mx), jnp.exp(l1 - mx), jnp.exp(l2 - mx)
            att = (e0 * o0 + e1 * o_scr[0, rows, :] + e2 * o_scr[1, rows, :]) / (e0 + e1 + e2)
            o_ref[rows, :] = att.astype(o_ref.dtype)


def _attention(proj3, bias_tiles, B, S):
    def col(cb):
        return pl.BlockSpec((None, S, LANE), lambda b, h, cb=cb: (b, 0, cb + h))
    in_specs = []
    for g in range(N_ATT_GROUPS):
        for base in (CB_AQ, CB_AK, CB_AV):
            in_specs.append(col(base + g * ATT_HPG))
    in_specs.append(pl.BlockSpec((N_ATT_GROUPS, None, 2, ATT_BLK, 2 * ATT_BLK), lambda b, h: (0, h, 0, 0, 0)))
    pad1 = ATT_BLK * ATT_GROUPS[1][1]
    scratch = [pltpu.VMEM((S, LANE), F32), pltpu.VMEM((pad1 + S, LANE), F32), pltpu.VMEM((pad1 + S, LANE), F32),
               pltpu.VMEM((S, LANE), F32), pltpu.VMEM((S, LANE), F32), pltpu.VMEM((S, LANE), F32),
               pltpu.VMEM((N_ATT_GROUPS - 1, S, LANE), F32), pltpu.VMEM((N_ATT_GROUPS - 1, S, LANE), F32)]
    bias5 = bias_tiles.reshape(N_ATT_GROUPS, ATT_HPG, 2, ATT_BLK, 2 * ATT_BLK)
    args = [proj3] * 9 + [bias5]
    return pl.pallas_call(
        functools.partial(_attn_kernel, S=S),
        grid=(B, ATT_HPG),
        in_specs=in_specs,
        out_specs=pl.BlockSpec((None, S, LANE), lambda b, h: (b, 0, h)),
        out_shape=jax.ShapeDtypeStruct((B, S, ATT_W), BF16),
        scratch_shapes=scratch,
        compiler_params=pltpu.CompilerParams(
            dimension_semantics=("parallel", "parallel"), vmem_limit_bytes=VMEM_LIMIT),
        name="attn",
    )(*args)


TL_TM = 512
TL_TK = 512
ML_PAIR = 2
ML_NPAIR = ML_HEADS // ML_PAIR
ML_CPT = TL_TM // ML_L
ML_NR = ML_CPT * ML_HEADS
XS_HDR = 8


def _tail_kernel(x_ref, ada_ref, att_ref, ga_ref, gm_ref, g2_ref, wa_ref, wm_ref, wo_ref, w1_ref, w2_ref,
                 mq_ref, mk_ref, mv_ref, mo_ref, gt_ref, bif_ref, cw_ref, cb_ref, ng_ref,
                 o_ref,
                 hml_scr, xq, xk, qa, kab, kaf, cst_scr, mch_scr, rf, *, n_tiles, tiles_per_seq):
    i = pl.program_id(0)
    L = ML_L
    tm = TL_TM
    im = jnp.minimum(i, n_tiles - 1)

    @pl.when(i == 0)
    def _():
        hml_scr[...] = jnp.zeros(hml_scr.shape, hml_scr.dtype)

    @pl.when(im % tiles_per_seq == 0)
    def _():
        cst_scr[...] = jnp.zeros(cst_scr.shape, F32)
        mch_scr[...] = jnp.zeros(mch_scr.shape, F32)
        xq[tm:tm + XS_HDR, :] = jnp.zeros((XS_HDR, ML_QK_W), F32)
        xk[tm:tm + XS_HDR, :] = jnp.zeros((XS_HDR, ML_QK_W), F32)


    def merge_stage():
        y_att = _dot(att_ref[...], wa_ref[...])
        y_ml = _dot(hml_scr[...], wm_ref[...])
        ga = jax.nn.sigmoid(ga_ref[...].astype(F32))
        gm = jax.nn.sigmoid(gm_ref[...].astype(F32))
        return (ga * y_att + gm * y_ml).astype(BF16)

    def out_proj_stage(mix):
        x1 = x_ref[...] + ada_ref[2:3, :] * _dot(mix, wo_ref[...])
        ms = jnp.mean(x1 * x1, axis=-1, keepdims=True)
        y = x1 * lax.rsqrt(ms + EPS) * g2_ref[...]
        return x1, (y * (1.0 + ada_ref[4:5, :]) + ada_ref[3:4, :]).astype(BF16)

    def mlp_up(u2, k):
        hdn = jnp.maximum(_dot(u2, w1_ref[:, k * TL_TK:(k + 1) * TL_TK]), 0.0)
        return (hdn * hdn).astype(BF16)

    def mlp_down(hsq, acc, k):
        part = _dot(hsq, w2_ref[k * TL_TK:(k + 1) * TL_TK, :])
        return part if acc is None else acc + part

    mix = merge_stage()

    lane = lax.broadcasted_iota(jnp.int32, (1, LANE), 1)
    srow = lax.broadcasted_iota(jnp.int32, (LANE, 1), 0)
    in_head_lane = [(lane >= hh * ML_DK) & (lane < (hh + 1) * ML_DK) for hh in range(ML_PAIR)]
    in_head_row = [(srow >= hh * ML_DK) & (srow < (hh + 1) * ML_DK) for hh in range(ML_PAIR)]
    head0_rows = in_head_row[0]

    def conv_silu(src_ref, xs, c0):
        xs[0:XS_HDR, :] = xs[tm:tm + XS_HDR, :]
        xs[XS_HDR:XS_HDR + tm, :] = src_ref[...].astype(F32)
        yv = cb_ref[:, c0:c0 + ML_QK_W]
        for jj in range(CONV_W):
            yv = yv + cw_ref[jj:jj + 1, c0:c0 + ML_QK_W] * xs[pl.ds(XS_HDR - (CONV_W - 1) + jj, tm), :]
        return yv * jax.nn.sigmoid(yv)

    yq = conv_silu(mq_ref, xq, 0)
    for p in range(ML_NPAIR):
        for hh in range(ML_PAIR):
            qa[hh, :, p * LANE:(p + 1) * LANE] = jnp.where(in_head_lane[hh], yq[:, p * LANE:(p + 1) * LANE],
                                                           0.0).astype(BF16)
    yk = conv_silu(mk_ref, xk, ML_QK_W) * (ML_DK ** -0.5)
    kaf[...] = yk
    kab[...] = yk.astype(BF16)

    x1, u2 = out_proj_stage(mix)

    r_i = lax.broadcasted_iota(jnp.int32, (L, L), 0)
    c_i = lax.broadcasted_iota(jnp.int32, (L, L), 1)
    ones_m = jnp.ones((L, L), BF16)
    incl_upper = (r_i <= c_i).astype(BF16)
    causal = c_i <= r_i
    lane_nr = lax.broadcasted_iota(jnp.int32, (ML_NR, L), 1)
    pad_rows = jnp.zeros((L - ML_NR, L), F32)

    def col_form(row_form):
        return jnp.concatenate([row_form, pad_rows], axis=0).T

    li = jnp.concatenate([gt_ref[0:ML_HEADS, c * L:(c + 1) * L] + bif_ref[0:ML_HEADS, :]
                          for c in range(ML_CPT)], axis=0)
    zf = jnp.concatenate([gt_ref[ML_HEADS:2 * ML_HEADS, c * L:(c + 1) * L] + bif_ref[ML_HEADS:2 * ML_HEADS, :]
                          for c in range(ML_CPT)], axis=0)
    lf = jnp.minimum(zf, 0.0) - jnp.log1p(jnp.exp(-jnp.abs(zf)))
    brow = _dot3(lf, incl_upper)
    bend = _dot3(lf, ones_m)
    u = brow - li
    g = bend - u
    maxg = jnp.max(g, axis=-1, keepdims=True)
    m = mch_scr[...]
    for c in range(ML_CPT):
        rs = slice(c * ML_HEADS, (c + 1) * ML_HEADS)
        rf[1, rs, :] = m
        m = jnp.maximum(bend[rs, :] + m, maxg[rs, :])
        rf[2, rs, :] = m
    mch_scr[...] = m
    m_cur = rf[1]
    m_nxt = rf[2]
    rf[0] = u
    rf[3] = jnp.exp(bend + m_cur - m_nxt)
    rf[4] = jnp.exp(g - m_nxt)
    pm = -u
    sh = 1
    while sh < L:
        pm = jnp.maximum(pm, jnp.where(lane_nr >= sh, pltpu.roll(pm, sh, axis=1), NEG))
        sh *= 2
    d1 = -jnp.maximum(m_cur, pm)
    dcol = col_form(d1)
    ecol = col_form(jnp.exp(d1 - brow))

    ones_v = jnp.ones((L, LANE), BF16)

    def prepare(c, p):
        rows = slice(c * L, (c + 1) * L)
        pl_ = slice(p * LANE, (p + 1) * LANE)
        r0 = c * ML_HEADS + p * ML_PAIR
        kc = kab[rows, pl_]
        kt = kaf[rows, pl_].T
        wk = jnp.where(head0_rows, rf[4, r0:r0 + 1, :], rf[4, r0 + 1:r0 + 2, :])
        ktw = (kt * wk).astype(BF16)
        ktw2 = jnp.concatenate([jnp.where(in_head_row[hh], ktw, jnp.zeros_like(ktw)) for hh in range(ML_PAIR)],
                               axis=1)
        vexts = [jnp.concatenate([mv_ref[rows, (r0 % ML_HEADS + hh) * LANE:(r0 % ML_HEADS + hh + 1) * LANE],
                                  ones_v], axis=1) for hh in range(ML_PAIR)]
        z = _dot(ktw2, jnp.concatenate(vexts, axis=0))
        ss = [_dot_nt(qa[hh, rows, pl_], kc) for hh in range(ML_PAIR)]
        drow = [jnp.broadcast_to(dcol[:, r0 + hh:r0 + hh + 1], (L, L)) for hh in range(ML_PAIR)]
        erow = [jnp.broadcast_to(ecol[:, r0 + hh:r0 + hh + 1], (L, L)) for hh in range(ML_PAIR)]
        return ss, z, vexts, drow, erow

    order = [(c, p) for c in range(ML_CPT) for p in range(ML_NPAIR)]
    assert 2 * (D_FF // TL_TK) == len(order)
    cst = [cst_scr[p] for p in range(ML_NPAIR)]
    nxt = prepare(*order[0])
    acc = None
    hsq = None
    for n, (c, p) in enumerate(order):
        rows = slice(c * L, (c + 1) * L)
        pl_ = slice(p * LANE, (p + 1) * LANE)
        r0 = c * ML_HEADS + p * ML_PAIR
        ss, z, vexts, drow, erow = nxt
        if n + 1 < len(order):
            nxt = prepare(*order[n + 1])
        cst_b = cst[p].astype(BF16)
        lhs = []
        for hh in range(ML_PAIR):
            r = r0 + hh
            w = ss[hh] * jnp.exp(jnp.where(causal, drow[hh] - rf[0, r:r + 1, :], NEG))
            qi = qa[hh, rows, pl_].astype(F32) * jnp.exp(drow[hh] + rf[1, r:r + 1, :])
            lhs.append(jnp.concatenate([w.astype(BF16), qi.astype(BF16)], axis=1))
        if n % 2 == 0:
            hsq = mlp_up(u2, n // 2)
        else:
            acc = mlp_down(hsq, acc, n // 2)
        for hh in range(ML_PAIR):
            cols = slice((p * ML_PAIR + hh) * LANE, (p * ML_PAIR + hh + 1) * LANE)
            tot = _dot(lhs[hh], jnp.concatenate([vexts[hh], cst_b], axis=0))
            hval = tot[:, 0:LANE] / jnp.maximum(jnp.abs(tot[:, LANE:2 * LANE]), erow[hh])
            msq = jnp.mean(hval * hval, axis=-1, keepdims=True)
            hn = hval * lax.rsqrt(msq + EPS) * ng_ref[:, cols]
            hml_scr[rows, cols] = (hn * jax.nn.sigmoid(mo_ref[rows, cols].astype(F32))).astype(hml_scr.dtype)
        dec = jnp.where(head0_rows, rf[3, r0:r0 + 1, :], rf[3, r0 + 1:r0 + 2, :])
        cst[p] = jnp.concatenate([dec, dec], axis=1) * cst[p] + z
    for p in range(ML_NPAIR):
        cst_scr[p] = cst[p]
    o_ref[...] = x1 + ada_ref[5:6, :] * acc


def _tail(x2, ada3, att2, proj2, gates_t, bif_b, conv_w, conv_b, norm_g, g2, wa, wm, wo, w1, w2, S):
    T = x2.shape[0]
    tm = TL_TM
    tps = S // tm
    n_tiles = T // tm
    cur = lambda i: jnp.minimum(i, n_tiles - 1)
    prv = lambda i: jnp.maximum(i - 1, 0)
    gcb = CB_G * LANE // D_MODEL
    in_specs = [
        pl.BlockSpec((tm, D_MODEL), lambda i: (prv(i), 0)),
        pl.BlockSpec((None, 6, D_MODEL), lambda i: (prv(i) // tps, 0, 0)),
        pl.BlockSpec((tm, ATT_W), lambda i: (prv(i), 0)),
        pl.BlockSpec((tm, D_MODEL), lambda i: (prv(i), gcb)),
        pl.BlockSpec((tm, D_MODEL), lambda i: (prv(i), gcb + 1)),
        _resident((1, D_MODEL)),
        _resident((ATT_W, D_MODEL)), _resident((ML_V_W, D_MODEL)), _resident((D_MODEL, D_MODEL)),
        _resident((D_MODEL, D_FF)), _resident((D_FF, D_MODEL)),
        pl.BlockSpec((tm, ML_QK_W), lambda i: (cur(i), CB_MQ * LANE // ML_QK_W)),
        pl.BlockSpec((tm, ML_QK_W), lambda i: (cur(i), CB_MK * LANE // ML_QK_W)),
        pl.BlockSpec((tm, ML_V_W), lambda i: (cur(i), CB_MV * LANE // ML_V_W)),
        pl.BlockSpec((tm, ML_V_W), lambda i: (cur(i), CB_MO * LANE // ML_V_W)),
        pl.BlockSpec((None, 2 * ML_HEADS, tm), lambda i: (cur(i) // tps, 0, cur(i) % tps)),
        _resident((2 * ML_HEADS, LANE)),
        _resident((CONV_W, 2 * ML_QK_W)), _resident((1, 2 * ML_QK_W)), _resident((1, ML_V_W)),
    ]
    scratch = [pltpu.VMEM((tm, ML_V_W), BF16),
               pltpu.VMEM((tm + XS_HDR, ML_QK_W), F32), pltpu.VMEM((tm + XS_HDR, ML_QK_W), F32),
               pltpu.VMEM((ML_PAIR, tm, ML_QK_W), BF16), pltpu.VMEM((tm, ML_QK_W), BF16),
               pltpu.VMEM((tm, ML_QK_W), F32),
               pltpu.VMEM((ML_NPAIR, LANE, 2 * LANE), F32), pltpu.VMEM((ML_HEADS, LANE), F32),
               pltpu.VMEM((5, ML_NR, ML_L), F32)]
    return pl.pallas_call(
        functools.partial(_tail_kernel, n_tiles=n_tiles, tiles_per_seq=tps),
        grid=(n_tiles + 1,),
        in_specs=in_specs,
        out_specs=pl.BlockSpec((tm, D_MODEL), lambda i: (prv(i), 0)),
        out_shape=jax.ShapeDtypeStruct((T, D_MODEL), F32),
        scratch_shapes=scratch,
        compiler_params=pltpu.CompilerParams(
            dimension_semantics=("arbitrary",), vmem_limit_bytes=VMEM_LIMIT),
        name="tail",
    )(x2, ada3, att2, proj2, proj2, g2, wa, wm, wo, w1, w2,
      proj2, proj2, proj2, proj2, gates_t, bif_b, conv_w, conv_b, norm_g)


def kernel(x, c, w_ada, b_ada, norm1_g, norm2_g, w_in, b_if, conv_w, conv_b, q_norm_g, k_norm_g,
           rel_bias, mlstm_norm_g, w_att_out, w_ml_out, w_out, w_ff1, w_ff2):
    B, S, D = x.shape
    T = B * S
    depth = w_ada.shape[0]
    bias_tiles = _bias_tiles(rel_bias)
    x2 = x.reshape(T, D)
    for l in range(depth):
        ada3 = _ada(c, w_ada[l].astype(BF16), b_ada[l]).reshape(B, 6, D)
        w_main = jnp.concatenate([w_in[l][:, lo:hi] for lo, hi in (WIN_G, WIN_MVO, WIN_ATT, WIN_MQK)],
                                 axis=1).astype(BF16)
        w_if_t = w_in[l][:, WIN_IF[0]:WIN_IF[1]].T.astype(BF16)
        proj2, gates_t = _inproj(x2, ada3, norm1_g[l].reshape(1, D), w_main, w_if_t,
                                 q_norm_g[l].reshape(1, ATT_DH), k_norm_g[l].reshape(1, ATT_DH), S)
        proj3 = proj2.reshape(B, S, PW)
        att = _attention(proj3, bias_tiles, B, S)
        bif_b = jnp.broadcast_to(b_if[l].reshape(2 * ML_HEADS, 1), (2 * ML_HEADS, LANE))
        x2 = _tail(x2, ada3, att.reshape(T, ATT_W), proj2, gates_t, bif_b, conv_w[l], conv_b[l].reshape(1, -1),
                   mlstm_norm_g[l].reshape(1, ML_V_W), norm2_g[l].reshape(1, D),
                   w_att_out[l].astype(BF16), w_ml_out[l].astype(BF16), w_out[l].astype(BF16),
                   w_ff1[l].astype(BF16), w_ff2[l].astype(BF16), S)
    return x2.reshape(B, S, D)
```

```python
import functools
import math

import numpy as np
import jax
import jax.numpy as jnp
from jax import lax
from jax.experimental import pallas as pl
from jax.experimental.pallas import tpu as pltpu

F32 = jnp.float32
BF16 = jnp.bfloat16

D_MODEL = 1024
ATT_GROUPS = ((128, 1), (512, 4), (2048, 16))
N_ATT_GROUPS = 3
ATT_HPG = 4
ATT_DH = 128
N_ATT_HEADS = 12
ATT_W = 512
ATT_BLK = 128
ML_HEADS = 8
ML_DK = 64
ML_DV = 128
ML_QK_W = 512
ML_V_W = 1024
CONV_W = 4
D_FF = 4096
N_BUCKETS = 32
MAX_DISTANCE = 2048
EPS = 1e-6
NEG = -1e30

LANE = 128
ML_L = 128

PW = 9728
CB_G, CB_MV, CB_MO = 0, 16, 24
CB_AQ, CB_AK, CB_AV = 32, 44, 56
CB_MQ, CB_MK = 68, 72
WIN_ATT, WIN_MQK, WIN_MVO, WIN_IF, WIN_G = (0, 4608), (4608, 5632), (5632, 7680), (7680, 7696), (7696, 9744)

VMEM_LIMIT = 56 * 1024 * 1024


def _dot(a, b):
    return jnp.dot(a, b, preferred_element_type=F32)


def _dot_nt(a, b):
    return lax.dot_general(a, b, (((1,), (1,)), ((), ())), preferred_element_type=F32)


def _split3(a):
    hi = a.astype(BF16)
    r1 = a - hi.astype(F32)
    mid = r1.astype(BF16)
    lo = (r1 - mid.astype(F32)).astype(BF16)
    return hi, mid, lo


def _dot3(a, rhs_bf16):
    hi, mid, lo = _split3(a)
    return _dot(hi, rhs_bf16) + _dot(mid, rhs_bf16) + _dot(lo, rhs_bf16)


def _resident(shape):
    return pl.BlockSpec(shape, lambda *_: (0,) * len(shape), pipeline_mode=pl.Buffered(1))


def _ada_kernel(c_ref, w_ref, b_ref, o_ref):
    c = c_ref[...]
    s = c * jax.nn.sigmoid(c)
    o_ref[...] = _dot(s.astype(BF16), w_ref[...]) + b_ref[...]


def _ada(c, w_bf16, b):
    B = c.shape[0]
    n = w_bf16.shape[1]
    tn = 1024
    return pl.pallas_call(
        _ada_kernel,
        grid=(n // tn,),
        in_specs=[pl.BlockSpec((B, D_MODEL), lambda j: (0, 0)),
                  pl.BlockSpec((D_MODEL, tn), lambda j: (0, j)),
                  pl.BlockSpec((1, tn), lambda j: (0, j))],
        out_specs=pl.BlockSpec((B, tn), lambda j: (0, j)),
        out_shape=jax.ShapeDtypeStruct((B, n), F32),
        name="ada",
    )(c, w_bf16, b.reshape(1, n))


def _t5_bucket_np(dist):
    max_exact = N_BUCKETS // 2
    d = np.maximum(dist, max_exact).astype(np.float32)
    large = max_exact + (np.log(d / np.float32(max_exact)) / np.float32(math.log(MAX_DISTANCE / max_exact))
                         * np.float32(N_BUCKETS - max_exact)).astype(np.int32)
    large = np.minimum(large, N_BUCKETS - 1)
    return np.where(dist < max_exact, dist, large).astype(np.int32)


def _bucket_tiles():
    i = np.arange(ATT_BLK)[:, None]
    j = np.arange(2 * ATT_BLK)[None, :]
    delta = ATT_BLK + i - j
    return np.stack([_t5_bucket_np(np.maximum(delta, 0) * dil) for _, dil in ATT_GROUPS])


def _bias_kernel(tab_ref, bucket_ref, o_ref):
    hh = pl.program_id(0)
    bucket = bucket_ref[...]
    acc = jnp.zeros(bucket.shape, F32)
    for k in range(N_BUCKETS):
        acc = jnp.where(bucket == k, tab_ref[k, hh], acc)
    i = lax.broadcasted_iota(jnp.int32, bucket.shape, 0)
    j = lax.broadcasted_iota(jnp.int32, bucket.shape, 1)
    delta = ATT_BLK + i - j
    valid = (delta >= 0) & (delta <= ATT_BLK)
    o_ref[0] = jnp.where(valid, acc, NEG)
    o_ref[1] = jnp.where(valid & (j >= ATT_BLK), acc, NEG)


def _bias_tiles(rel_bias):
    buckets = jnp.asarray(_bucket_tiles())
    return pl.pallas_call(
        _bias_kernel,
        grid=(N_ATT_HEADS,),
        in_specs=[pl.BlockSpec(memory_space=pltpu.SMEM),
                  pl.BlockSpec((None, ATT_BLK, 2 * ATT_BLK), lambda h: (h // ATT_HPG, 0, 0))],
        out_specs=pl.BlockSpec((None, 2, ATT_BLK, 2 * ATT_BLK), lambda h: (h, 0, 0, 0)),
        out_shape=jax.ShapeDtypeStruct((N_ATT_HEADS, 2, ATT_BLK, 2 * ATT_BLK), F32),
        name="bias_tiles",
    )(rel_bias, buckets)


IP_TM = 512
IP_TN = 512
IP_Q0 = CB_AQ * LANE // IP_TN
IP_K0 = CB_AK * LANE // IP_TN
IP_V0 = CB_AV * LANE // IP_TN


def _inproj_kernel(x_ref, ada_ref, g1_ref, w_ref, wif_ref, qg_ref, kg_ref, o_ref, gt_ref):
    x = x_ref[...]
    ms = jnp.mean(x * x, axis=-1, keepdims=True)
    y = x * lax.rsqrt(ms + EPS) * g1_ref[...]
    ub = (y * (1.0 + ada_ref[1:2, :]) + ada_ref[0:1, :]).astype(BF16)
    gt_ref[...] = _dot_nt(wif_ref[...], ub)
    for j in range(PW // IP_TN):
        acc = _dot(ub, w_ref[:, j * IP_TN:(j + 1) * IP_TN])
        if IP_Q0 <= j < IP_V0:
            gain = qg_ref[...] if j < IP_K0 else kg_ref[...]
            for k in range(IP_TN // ATT_DH):
                a = acc[:, k * ATT_DH:(k + 1) * ATT_DH]
                ms = jnp.mean(a * a, axis=-1, keepdims=True)
                lo = j * IP_TN + k * ATT_DH
                o_ref[:, lo:lo + ATT_DH] = (a * lax.rsqrt(ms + EPS) * gain).astype(o_ref.dtype)
        else:
            o_ref[:, j * IP_TN:(j + 1) * IP_TN] = acc.astype(o_ref.dtype)


def _inproj(x2, ada3, g1, w_main, w_if_t, qg, kg, S):
    T = x2.shape[0]
    tm = IP_TM
    tiles_per_seq = S // tm
    return pl.pallas_call(
        _inproj_kernel,
        grid=(T // tm,),
        in_specs=[pl.BlockSpec((tm, D_MODEL), lambda i: (i, 0)),
                  pl.BlockSpec((None, 6, D_MODEL), lambda i: (i // tiles_per_seq, 0, 0)),
                  _resident((1, D_MODEL)),
                  _resident((D_MODEL, PW)),
                  _resident((16, D_MODEL)),
                  _resident((1, ATT_DH)),
                  _resident((1, ATT_DH))],
        out_specs=[pl.BlockSpec((tm, PW), lambda i: (i, 0)),
                   pl.BlockSpec((None, 16, tm), lambda i: (i // tiles_per_seq, 0, i % tiles_per_seq))],
        out_shape=[jax.ShapeDtypeStruct((T, PW), BF16),
                   jax.ShapeDtypeStruct((T // S, 16, S), F32)],
        compiler_params=pltpu.CompilerParams(
            dimension_semantics=("parallel",), vmem_limit_bytes=VMEM_LIMIT),
        name="inproj",
    )(x2, ada3, g1, w_main, w_if_t, qg, kg)


ATT_UNROLL = 16


def _attn_kernel(q0_ref, k0_ref, v0_ref, q1_ref, k1_ref, v1_ref, q2_ref, k2_ref, v2_ref,
                 bias_ref, o_ref,
                 q1f, k1f, v1f, q2f, k2f, v2f, o_scr, l_scr, *, S):
    scale = ATT_DH ** -0.5
    blk = ATT_BLK

    pad1 = blk * ATT_GROUPS[1][1]
    k1f[0:pad1, :] = jnp.zeros((pad1, LANE), F32)
    v1f[0:pad1, :] = jnp.zeros((pad1, LANE), F32)
    k1f[pad1:pad1 + S, :] = k1_ref[...].astype(F32)
    v1f[pad1:pad1 + S, :] = v1_ref[...].astype(F32)
    q1f[...] = q1_ref[...].astype(F32)
    q2f[...] = q2_ref[...].astype(F32)
    k2f[...] = k2_ref[...].astype(F32)
    v2f[...] = v2_ref[...].astype(F32)

    def softmax_blocks(ops):
        ss = [_dot_nt(q, kk) * scale + bias for (q, kk, _, bias) in ops]
        ms = [jnp.max(s, axis=-1, keepdims=True) for s in ss]
        ps = [jnp.exp(s - m) for s, m in zip(ss, ms)]
        ls = [jnp.sum(p, axis=-1, keepdims=True) for p in ps]
        accs = [_dot(p.astype(BF16), op[2]) for p, op in zip(ps, ops)]
        return [(acc / l, m + jnp.log(l)) for acc, l, m in zip(accs, ls, ms)]

    def run_dilated(g, r, n_iters, fetch):
        def body(it, carry):
            fetched = [fetch(it, u) for u in range(ATT_UNROLL)]
            res = softmax_blocks([f[:4] for f in fetched])
            for (o, lse), f in zip(res, fetched):
                rows = pl.ds(f[4], blk, stride=r)
                o_scr[g - 1, rows, :] = o
                l_scr[g - 1, rows, :] = jnp.broadcast_to(lse, (blk, LANE))
            return carry
        lax.fori_loop(0, n_iters, body, 0)

    r1 = ATT_GROUPS[1][1]
    nb1 = S // r1 // blk
    assert ATT_UNROLL % nb1 == 0

    def fetch1(it, u):
        rho = it * (ATT_UNROLL // nb1) + u // nb1
        n = u % nb1
        start = rho + r1 * blk * n
        return (q1f[pl.ds(start, blk, stride=r1), :].astype(BF16),
                k1f[pl.ds(start, 2 * blk, stride=r1), :].astype(BF16),
                v1f[pl.ds(start, 2 * blk, stride=r1), :].astype(BF16),
                bias_ref[1, 1 if n == 0 else 0], start)
    run_dilated(1, r1, r1 * nb1 // ATT_UNROLL, fetch1)

    r2 = ATT_GROUPS[2][1]
    assert S // r2 == blk

    def fetch2(it, u):
        rho = it * ATT_UNROLL + u
        return (q2f[pl.ds(rho, blk, stride=r2), :].astype(BF16),
                k2f[pl.ds(rho, blk, stride=r2), :].astype(BF16),
                v2f[pl.ds(rho, blk, stride=r2), :].astype(BF16),
                bias_ref[2, 1][:, blk:2 * blk], rho)
    run_dilated(2, r2, r2 // ATT_UNROLL, fetch2)

    for it in range(S // blk // ATT_UNROLL):
        ops = []
        for u in range(ATT_UNROLL):
            n = it * ATT_UNROLL + u
            q = q0_ref[n * blk:(n + 1) * blk, :]
            if n == 0:
                ops.append((q, k0_ref[0:blk, :], v0_ref[0:blk, :], bias_ref[0, 1][:, blk:2 * blk]))
            else:
                ops.append((q, k0_ref[(n - 1) * blk:(n + 1) * blk, :], v0_ref[(n - 1) * blk:(n + 1) * blk, :],
                            bias_ref[0, 0]))
        for u, (o0, lse0) in enumerate(softmax_blocks(ops)):
            n = it * ATT_UNROLL + u
            rows = slice(n * blk, (n + 1) * blk)
            l1, l2 = l_scr[0, rows, :], l_scr[1, rows, :]
            mx = jnp.maximum(jnp.maximum(lse0, l1), l2)
            e0, e1, e2 = jnp.exp(lse0 - mx), jnp.exp(l1 - mx), jnp.exp(l2 - mx)
            att = (e0 * o0 + e1 * o_scr[0, rows, :] + e2 * o_scr[1, rows, :]) / (e0 + e1 + e2)
            o_ref[rows, :] = att.astype(o_ref.dtype)


def _attention(proj3, bias_tiles, B, S):
    def col(cb):
        return pl.BlockSpec((None, S, LANE), lambda b, h, cb=cb: (b, 0, cb + h))
    in_specs = []
    for g in range(N_ATT_GROUPS):
        for base in (CB_AQ, CB_AK, CB_AV):
            in_specs.append(col(base + g * ATT_HPG))
    in_specs.append(pl.BlockSpec((N_ATT_GROUPS, None, 2, ATT_BLK, 2 * ATT_BLK), lambda b, h: (0, h, 0, 0, 0)))
    pad1 = ATT_BLK * ATT_GROUPS[1][1]
    scratch = [pltpu.VMEM((S, LANE), F32), pltpu.VMEM((pad1 + S, LANE), F32), pltpu.VMEM((pad1 + S, LANE), F32),
               pltpu.VMEM((S, LANE), F32), pltpu.VMEM((S, LANE), F32), pltpu.VMEM((S, LANE), F32),
               pltpu.VMEM((N_ATT_GROUPS - 1, S, LANE), F32), pltpu.VMEM((N_ATT_GROUPS - 1, S, LANE), F32)]
    bias5 = bias_tiles.reshape(N_ATT_GROUPS, ATT_HPG, 2, ATT_BLK, 2 * ATT_BLK)
    args = [proj3] * 9 + [bias5]
    return pl.pallas_call(
        functools.partial(_attn_kernel, S=S),
        grid=(B, ATT_HPG),
        in_specs=in_specs,
        out_specs=pl.BlockSpec((None, S, LANE), lambda b, h: (b, 0, h)),
        out_shape=jax.ShapeDtypeStruct((B, S, ATT_W), BF16),
        scratch_shapes=scratch,
        compiler_params=pltpu.CompilerParams(
            dimension_semantics=("parallel", "parallel"), vmem_limit_bytes=VMEM_LIMIT),
        name="attn",
    )(*args)


TL_TM = 512
TL_TK = 512
ML_PAIR = 2
ML_NPAIR = ML_HEADS // ML_PAIR
ML_CPT = TL_TM // ML_L
ML_NR = ML_CPT * ML_HEADS
XS_HDR = 8


def _tail_kernel(x_ref, ada_ref, att_ref, ga_ref, gm_ref, g2_ref, wa_ref, wm_ref, wo_ref, w1_ref, w2_ref,
                 mq_ref, mk_ref, mv_ref, mo_ref, gt_ref, bif_ref, cw_ref, cb_ref, ng_ref,
                 o_ref,
                 hml_scr, xq, xk, qa, kab, kaf, cst_scr, mch_scr, rf, *, n_tiles, tiles_per_seq):
    i = pl.program_id(0)
    L = ML_L
    tm = TL_TM
    im = jnp.minimum(i, n_tiles - 1)

    @pl.when(i == 0)
    def _():
        hml_scr[...] = jnp.zeros(hml_scr.shape, hml_scr.dtype)

    @pl.when(im % tiles_per_seq == 0)
    def _():
        cst_scr[...] = jnp.zeros(cst_scr.shape, F32)
        mch_scr[...] = jnp.zeros(mch_scr.shape, F32)
        xq[tm:tm + XS_HDR, :] = jnp.zeros((XS_HDR, ML_QK_W), F32)
        xk[tm:tm + XS_HDR, :] = jnp.zeros((XS_HDR, ML_QK_W), F32)


    def merge_stage():
        y_att = _dot(att_ref[...], wa_ref[...])
        y_ml = _dot(hml_scr[...], wm_ref[...])
        ga = jax.nn.sigmoid(ga_ref[...].astype(F32))
        gm = jax.nn.sigmoid(gm_ref[...].astype(F32))
        return (ga * y_att + gm * y_ml).astype(BF16)

    def out_proj_stage(mix):
        x1 = x_ref[...] + ada_ref[2:3, :] * _dot(mix, wo_ref[...])
        ms = jnp.mean(x1 * x1, axis=-1, keepdims=True)
        y = x1 * lax.rsqrt(ms + EPS) * g2_ref[...]
        return x1, (y * (1.0 + ada_ref[4:5, :]) + ada_ref[3:4, :]).astype(BF16)

    def mlp_up(u2, k):
        hdn = jnp.maximum(_dot(u2, w1_ref[:, k * TL_TK:(k + 1) * TL_TK]), 0.0)
        return (hdn * hdn).astype(BF16)

    def mlp_down(hsq, acc, k):
        part = _dot(hsq, w2_ref[k * TL_TK:(k + 1) * TL_TK, :])
        return part if acc is None else acc + part

    mix = merge_stage()

    lane = lax.broadcasted_iota(jnp.int32, (1, LANE), 1)
    srow = lax.broadcasted_iota(jnp.int32, (LANE, 1), 0)
    in_head_lane = [(lane >= hh * ML_DK) & (lane < (hh + 1) * ML_DK) for hh in range(ML_PAIR)]
    in_head_row = [(srow >= hh * ML_DK) & (srow < (hh + 1) * ML_DK) for hh in range(ML_PAIR)]
    head0_rows = in_head_row[0]

    def conv_silu(src_ref, xs, c0):
        xs[0:XS_HDR, :] = xs[tm:tm + XS_HDR, :]
        xs[XS_HDR:XS_HDR + tm, :] = src_ref[...].astype(F32)
        yv = cb_ref[:, c0:c0 + ML_QK_W]
        for jj in range(CONV_W):
            yv = yv + cw_ref[jj:jj + 1, c0:c0 + ML_QK_W] * xs[pl.ds(XS_HDR - (CONV_W - 1) + jj, tm), :]
        return yv * jax.nn.sigmoid(yv)

    yq = conv_silu(mq_ref, xq, 0)
    for p in range(ML_NPAIR):
        for hh in range(ML_PAIR):
            qa[hh, :, p * LANE:(p + 1) * LANE] = jnp.where(in_head_lane[hh], yq[:, p * LANE:(p + 1) * LANE],
                                                           0.0).astype(BF16)
    yk = conv_silu(mk_ref, xk, ML_QK_W) * (ML_DK ** -0.5)
    kaf[...] = yk
    kab[...] = yk.astype(BF16)

    x1, u2 = out_proj_stage(mix)

    r_i = lax.broadcasted_iota(jnp.int32, (L, L), 0)
    c_i = lax.broadcasted_iota(jnp.int32, (L, L), 1)
    ones_m = jnp.ones((L, L), BF16)
    incl_upper = (r_i <= c_i).astype(BF16)
    causal = c_i <= r_i
    lane_nr = lax.broadcasted_iota(jnp.int32, (ML_NR, L), 1)
    pad_rows = jnp.zeros((L - ML_NR, L), F32)

    def col_form(row_form):
        return jnp.concatenate([row_form, pad_rows], axis=0).T

    li = jnp.concatenate([gt_ref[0:ML_HEADS, c * L:(c + 1) * L] + bif_ref[0:ML_HEADS, :]
                          for c in range(ML_CPT)], axis=0)
    zf = jnp.concatenate([gt_ref[ML_HEADS:2 * ML_HEADS, c * L:(c + 1) * L] + bif_ref[ML_HEADS:2 * ML_HEADS, :]
                          for c in range(ML_CPT)], axis=0)
    lf = jnp.minimum(zf, 0.0) - jnp.log1p(jnp.exp(-jnp.abs(zf)))
    brow = _dot3(lf, incl_upper)
    bend = _dot3(lf, ones_m)
    u = brow - li
    g = bend - u
    maxg = jnp.max(g, axis=-1, keepdims=True)
    m = mch_scr[...]
    for c in range(ML_CPT):
        rs = slice(c * ML_HEADS, (c + 1) * ML_HEADS)
        rf[1, rs, :] = m
        m = jnp.maximum(bend[rs, :] + m, maxg[rs, :])
        rf[2, rs, :] = m
    mch_scr[...] = m
    m_cur = rf[1]
    m_nxt = rf[2]
    rf[0] = u
    rf[3] = jnp.exp(bend + m_cur - m_nxt)
    rf[4] = jnp.exp(g - m_nxt)
    pm = -u
    sh = 1
    while sh < L:
        pm = jnp.maximum(pm, jnp.where(lane_nr >= sh, pltpu.roll(pm, sh, axis=1), NEG))
        sh *= 2
    d1 = -jnp.maximum(m_cur, pm)
    dcol = col_form(d1)
    ecol = col_form(jnp.exp(d1 - brow))

    ones_v = jnp.ones((L, LANE), BF16)

    def prepare(c, p):
        rows = slice(c * L, (c + 1) * L)
        pl_ = slice(p * LANE, (p + 1) * LANE)
        r0 = c * ML_HEADS + p * ML_PAIR
        kc = kab[rows, pl_]
        kt = kaf[rows, pl_].T
        wk = jnp.where(head0_rows, rf[4, r0:r0 + 1, :], rf[4, r0 + 1:r0 + 2, :])
        ktw = (kt * wk).astype(BF16)
        ktw2 = jnp.concatenate([jnp.where(in_head_row[hh], ktw, jnp.zeros_like(ktw)) for hh in range(ML_PAIR)],
                               axis=1)
        vexts = [jnp.concatenate([mv_ref[rows, (r0 % ML_HEADS + hh) * LANE:(r0 % ML_HEADS + hh + 1) * LANE],
                                  ones_v], axis=1) for hh in range(ML_PAIR)]
        z = _dot(ktw2, jnp.concatenate(vexts, axis=0))
        ss = [_dot_nt(qa[hh, rows, pl_], kc) for hh in range(ML_PAIR)]
        drow = [jnp.broadcast_to(dcol[:, r0 + hh:r0 + hh + 1], (L, L)) for hh in range(ML_PAIR)]
        erow = [jnp.broadcast_to(ecol[:, r0 + hh:r0 + hh + 1], (L, L)) for hh in range(ML_PAIR)]
        return ss, z, vexts, drow, erow

    order = [(c, p) for c in range(ML_CPT) for p in range(ML_NPAIR)]
    assert 2 * (D_FF // TL_TK) == len(order)
    cst = [cst_scr[p] for p in range(ML_NPAIR)]
    nxt = prepare(*order[0])
    acc = None
    hsq = None
    for n, (c, p) in enumerate(order):
        rows = slice(c * L, (c + 1) * L)
        pl_ = slice(p * LANE, (p + 1) * LANE)
        r0 = c * ML_HEADS + p * ML_PAIR
        ss, z, vexts, drow, erow = nxt
        if n + 1 < len(order):
            nxt = prepare(*order[n + 1])
        cst_b = cst[p].astype(BF16)
        lhs = []
        for hh in range(ML_PAIR):
            r = r0 + hh
            w = ss[hh] * jnp.exp(jnp.where(causal, drow[hh] - rf[0, r:r + 1, :], NEG))
            qi = qa[hh, rows, pl_].astype(F32) * jnp.exp(drow[hh] + rf[1, r:r + 1, :])
            lhs.append(jnp.concatenate([w.astype(BF16), qi.astype(BF16)], axis=1))
        if n % 2 == 0:
            hsq = mlp_up(u2, n // 2)
        else:
            acc = mlp_down(hsq, acc, n // 2)
        for hh in range(ML_PAIR):
            cols = slice((p * ML_PAIR + hh) * LANE, (p * ML_PAIR + hh + 1) * LANE)
            tot = _dot(lhs[hh], jnp.concatenate([vexts[hh], cst_b], axis=0))
            hval = tot[:, 0:LANE] / jnp.maximum(jnp.abs(tot[:, LANE:2 * LANE]), erow[hh])
            msq = jnp.mean(hval * hval, axis=-1, keepdims=True)
            hn = hval * lax.rsqrt(msq + EPS) * ng_ref[:, cols]
            hml_scr[rows, cols] = (hn * jax.nn.sigmoid(mo_ref[rows, cols].astype(F32))).astype(hml_scr.dtype)
        dec = jnp.where(head0_rows, rf[3, r0:r0 + 1, :], rf[3, r0 + 1:r0 + 2, :])
        cst[p] = jnp.concatenate([dec, dec], axis=1) * cst[p] + z
    for p in range(ML_NPAIR):
        cst_scr[p] = cst[p]
    o_ref[...] = x1 + ada_ref[5:6, :] * acc


def _tail(x2, ada3, att2, proj2, gates_t, bif_b, conv_w, conv_b, norm_g, g2, wa, wm, wo, w1, w2, S):
    T = x2.shape[0]
    tm = TL_TM
    tps = S // tm
    n_tiles = T // tm
    cur = lambda i: jnp.minimum(i, n_tiles - 1)
    prv = lambda i: jnp.maximum(i - 1, 0)
    gcb = CB_G * LANE // D_MODEL
    in_specs = [
        pl.BlockSpec((tm, D_MODEL), lambda i: (prv(i), 0)),
        pl.BlockSpec((None, 6, D_MODEL), lambda i: (prv(i) // tps, 0, 0)),
        pl.BlockSpec((tm, ATT_W), lambda i: (prv(i), 0)),
        pl.BlockSpec((tm, D_MODEL), lambda i: (prv(i), gcb)),
        pl.BlockSpec((tm, D_MODEL), lambda i: (prv(i), gcb + 1)),
        _resident((1, D_MODEL)),
        _resident((ATT_W, D_MODEL)), _resident((ML_V_W, D_MODEL)), _resident((D_MODEL, D_MODEL)),
        _resident((D_MODEL, D_FF)), _resident((D_FF, D_MODEL)),
        pl.BlockSpec((tm, ML_QK_W), lambda i: (cur(i), CB_MQ * LANE // ML_QK_W)),
        pl.BlockSpec((tm, ML_QK_W), lambda i: (cur(i), CB_MK * LANE // ML_QK_W)),
        pl.BlockSpec((tm, ML_V_W), lambda i: (cur(i), CB_MV * LANE // ML_V_W)),
        pl.BlockSpec((tm, ML_V_W), lambda i: (cur(i), CB_MO * LANE // ML_V_W)),
        pl.BlockSpec((None, 2 * ML_HEADS, tm), lambda i: (cur(i) // tps, 0, cur(i) % tps)),
        _resident((2 * ML_HEADS, LANE)),
        _resident((CONV_W, 2 * ML_QK_W)), _resident((1, 2 * ML_QK_W)), _resident((1, ML_V_W)),
    ]
    scratch = [pltpu.VMEM((tm, ML_V_W), BF16),
               pltpu.VMEM((tm + XS_HDR, ML_QK_W), F32), pltpu.VMEM((tm + XS_HDR, ML_QK_W), F32),
               pltpu.VMEM((ML_PAIR, tm, ML_QK_W), BF16), pltpu.VMEM((tm, ML_QK_W), BF16),
               pltpu.VMEM((tm, ML_QK_W), F32),
               pltpu.VMEM((ML_NPAIR, LANE, 2 * LANE), F32), pltpu.VMEM((ML_HEADS, LANE), F32),
               pltpu.VMEM((5, ML_NR, ML_L), F32)]
    return pl.pallas_call(
        functools.partial(_tail_kernel, n_tiles=n_tiles, tiles_per_seq=tps),
        grid=(n_tiles + 1,),
        in_specs=in_specs,
        out_specs=pl.BlockSpec((tm, D_MODEL), lambda i: (prv(i), 0)),
        out_shape=jax.ShapeDtypeStruct((T, D_MODEL), F32),
        scratch_shapes=scratch,
        compiler_params=pltpu.CompilerParams(
            dimension_semantics=("arbitrary",), vmem_limit_bytes=VMEM_LIMIT),
        name="tail",
    )(x2, ada3, att2, proj2, proj2, g2, wa, wm, wo, w1, w2,
      proj2, proj2, proj2, proj2, gates_t, bif_b, conv_w, conv_b, norm_g)


def kernel(x, c, w_ada, b_ada, norm1_g, norm2_g, w_in, b_if, conv_w, conv_b, q_norm_g, k_norm_g,
           rel_bias, mlstm_norm_g, w_att_out, w_ml_out, w_out, w_ff1, w_ff2):
    B, S, D = x.shape
    T = B * S
    depth = w_ada.shape[0]
    bias_tiles = _bias_tiles(rel_bias)
    x2 = x.reshape(T, D)
    for l in range(depth):
        ada3 = _ada(c, w_ada[l].astype(BF16), b_ada[l]).reshape(B, 6, D)
        w_in_b = w_in[l].astype(BF16)
        w_main = jnp.concatenate([w_in_b[:, lo:hi] for lo, hi in (WIN_G, WIN_MVO, WIN_ATT, WIN_MQK)], axis=1)
        w_if_t = w_in_b[:, WIN_IF[0]:WIN_IF[1]].T
        proj2, gates_t = _inproj(x2, ada3, norm1_g[l].reshape(1, D), w_main, w_if_t,
                                 q_norm_g[l].reshape(1, ATT_DH), k_norm_g[l].reshape(1, ATT_DH), S)
        proj3 = proj2.reshape(B, S, PW)
        att = _attention(proj3, bias_tiles, B, S)
        bif_b = jnp.broadcast_to(b_if[l].reshape(2 * ML_HEADS, 1), (2 * ML_HEADS, LANE))
        x2 = _tail(x2, ada3, att.reshape(T, ATT_W), proj2, gates_t, bif_b, conv_w[l], conv_b[l].reshape(1, -1),
                   mlstm_norm_g[l].reshape(1, ML_V_W), norm2_g[l].reshape(1, D),
                   w_att_out[l].astype(BF16), w_ml_out[l].astype(BF16), w_out[l].astype(BF16),
                   w_ff1[l].astype(BF16), w_ff2[l].astype(BF16), S)
    return x2.reshape(B, S, D)
```

```python
import functools
import math

import numpy as np
import jax
import jax.numpy as jnp
from jax import lax
from jax.experimental import pallas as pl
from jax.experimental.pallas import tpu as pltpu

F32 = jnp.float32
BF16 = jnp.bfloat16

D_MODEL = 1024
ATT_GROUPS = ((128, 1), (512, 4), (2048, 16))
N_ATT_GROUPS = 3
ATT_HPG = 4
ATT_DH = 128
N_ATT_HEADS = 12
ATT_W = 512
ATT_BLK = 128
ML_HEADS = 8
ML_DK = 64
ML_DV = 128
ML_QK_W = 512
ML_V_W = 1024
CONV_W = 4
D_FF = 4096
N_BUCKETS = 32
MAX_DISTANCE = 2048
EPS = 1e-6
NEG = -1e30

LANE = 128
ML_L = 128

AW = 3 * N_ATT_HEADS * ATT_DH
PW = 5120
CB_G, CB_MV, CB_MO = 0, 16, 24
CB_MQ, CB_MK = 32, 36
WIN_ATT, WIN_MQK, WIN_MVO, WIN_IF, WIN_G = (0, 4608), (4608, 5632), (5632, 7680), (7680, 7696), (7696, 9744)

VMEM_LIMIT = 56 * 1024 * 1024


def _dot(a, b):
    return jnp.dot(a, b, preferred_element_type=F32)


def _dot_nt(a, b):
    return lax.dot_general(a, b, (((1,), (1,)), ((), ())), preferred_element_type=F32)


def _split3(a):
    hi = a.astype(BF16)
    r1 = a - hi.astype(F32)
    mid = r1.astype(BF16)
    lo = (r1 - mid.astype(F32)).astype(BF16)
    return hi, mid, lo


def _dot3(a, rhs_bf16):
    hi, mid, lo = _split3(a)
    return _dot(hi, rhs_bf16) + _dot(mid, rhs_bf16) + _dot(lo, rhs_bf16)


def _resident(shape):
    return pl.BlockSpec(shape, lambda *_: (0,) * len(shape), pipeline_mode=pl.Buffered(1))


def _ada_kernel(c_ref, w_ref, b_ref, o_ref):
    c = c_ref[...]
    s = c * jax.nn.sigmoid(c)
    o_ref[...] = _dot(s.astype(BF16), w_ref[...]) + b_ref[...]


def _ada(c, w_bf16, b):
    B = c.shape[0]
    n = w_bf16.shape[1]
    tn = 1024
    return pl.pallas_call(
        _ada_kernel,
        grid=(n // tn,),
        in_specs=[pl.BlockSpec((B, D_MODEL), lambda j: (0, 0)),
                  pl.BlockSpec((D_MODEL, tn), lambda j: (0, j)),
                  pl.BlockSpec((1, tn), lambda j: (0, j))],
        out_specs=pl.BlockSpec((B, tn), lambda j: (0, j)),
        out_shape=jax.ShapeDtypeStruct((B, n), F32),
        name="ada",
    )(c, w_bf16, b.reshape(1, n))


def _t5_bucket_np(dist):
    max_exact = N_BUCKETS // 2
    d = np.maximum(dist, max_exact).astype(np.float32)
    large = max_exact + (np.log(d / np.float32(max_exact)) / np.float32(math.log(MAX_DISTANCE / max_exact))
                         * np.float32(N_BUCKETS - max_exact)).astype(np.int32)
    large = np.minimum(large, N_BUCKETS - 1)
    return np.where(dist < max_exact, dist, large).astype(np.int32)


def _bucket_tiles():
    i = np.arange(ATT_BLK)[:, None]
    j = np.arange(2 * ATT_BLK)[None, :]
    delta = ATT_BLK + i - j
    return np.stack([_t5_bucket_np(np.maximum(delta, 0) * dil) for _, dil in ATT_GROUPS])


def _bias_kernel(tab_ref, bucket_ref, o_ref):
    hh = pl.program_id(0)
    bucket = bucket_ref[...]
    acc = jnp.zeros(bucket.shape, F32)
    for k in range(N_BUCKETS):
        acc = jnp.where(bucket == k, tab_ref[k, hh], acc)
    i = lax.broadcasted_iota(jnp.int32, bucket.shape, 0)
    j = lax.broadcasted_iota(jnp.int32, bucket.shape, 1)
    delta = ATT_BLK + i - j
    valid = (delta >= 0) & (delta <= ATT_BLK)
    o_ref[0] = jnp.where(valid, acc, NEG)
    o_ref[1] = jnp.where(valid & (j >= ATT_BLK), acc, NEG)


def _bias_tiles(rel_bias):
    buckets = jnp.asarray(_bucket_tiles())
    return pl.pallas_call(
        _bias_kernel,
        grid=(N_ATT_HEADS,),
        in_specs=[pl.BlockSpec(memory_space=pltpu.SMEM),
                  pl.BlockSpec((None, ATT_BLK, 2 * ATT_BLK), lambda h: (h // ATT_HPG, 0, 0))],
        out_specs=pl.BlockSpec((None, 2, ATT_BLK, 2 * ATT_BLK), lambda h: (h, 0, 0, 0)),
        out_shape=jax.ShapeDtypeStruct((N_ATT_HEADS, 2, ATT_BLK, 2 * ATT_BLK), F32),
        name="bias_tiles",
    )(rel_bias, buckets)


IP_TM = 512
IP_TN = 512
IP_NQ = N_ATT_HEADS * ATT_DH // IP_TN


def _inproj_att_kernel(x_ref, ada_ref, g1_ref, w_ref, qg_ref, kg_ref, u_ref, o_ref):
    x = x_ref[...]
    ms = jnp.mean(x * x, axis=-1, keepdims=True)
    y = x * lax.rsqrt(ms + EPS) * g1_ref[...]
    ub = (y * (1.0 + ada_ref[1:2, :]) + ada_ref[0:1, :]).astype(BF16)
    u_ref[...] = ub
    for j in range(AW // IP_TN):
        acc = _dot(ub, w_ref[:, j * IP_TN:(j + 1) * IP_TN])
        if j < 2 * IP_NQ:
            gain = qg_ref[...] if j < IP_NQ else kg_ref[...]
            for k in range(IP_TN // ATT_DH):
                a = acc[:, k * ATT_DH:(k + 1) * ATT_DH]
                ms = jnp.mean(a * a, axis=-1, keepdims=True)
                lo = j * IP_TN + k * ATT_DH
                o_ref[:, lo:lo + ATT_DH] = (a * lax.rsqrt(ms + EPS) * gain).astype(o_ref.dtype)
        else:
            o_ref[:, j * IP_TN:(j + 1) * IP_TN] = acc.astype(o_ref.dtype)


def _inproj_att(x2, ada3, g1, w_att, qg, kg, S):
    T = x2.shape[0]
    tm = IP_TM
    tiles_per_seq = S // tm
    return pl.pallas_call(
        _inproj_att_kernel,
        grid=(T // tm,),
        in_specs=[pl.BlockSpec((tm, D_MODEL), lambda i: (i, 0)),
                  pl.BlockSpec((None, 6, D_MODEL), lambda i: (i // tiles_per_seq, 0, 0)),
                  _resident((1, D_MODEL)),
                  _resident((D_MODEL, AW)),
                  _resident((1, ATT_DH)),
                  _resident((1, ATT_DH))],
        out_specs=[pl.BlockSpec((tm, D_MODEL), lambda i: (i, 0)),
                   pl.BlockSpec((tm, AW), lambda i: (i, 0))],
        out_shape=[jax.ShapeDtypeStruct((T, D_MODEL), BF16),
                   jax.ShapeDtypeStruct((T, AW), BF16)],
        compiler_params=pltpu.CompilerParams(
            dimension_semantics=("parallel",), vmem_limit_bytes=VMEM_LIMIT),
        name="inproj_att",
    )(x2, ada3, g1, w_att, qg, kg)


ATT_UNROLL = 8


def _mix_kernel(u_ref, w_ref, wif_ref,
                q0_ref, k0_ref, v0_ref, q1_ref, k1_ref, v1_ref, q2_ref, k2_ref, v2_ref, bias_ref,
                p_ref, gt_ref, o_ref,
                q1f, k1f, v1f, q2f, k2f, v2f, o_scr, l_scr, *, S):
    scale = ATT_DH ** -0.5
    blk = ATT_BLK

    ub = u_ref[...]

    def proj_chunk(j):
        p_ref[:, j * IP_TN:(j + 1) * IP_TN] = _dot(ub, w_ref[:, j * IP_TN:(j + 1) * IP_TN]).astype(p_ref.dtype)
    pending = [functools.partial(proj_chunk, j) for j in range(PW // IP_TN)]

    def emit_proj(n):
        for _ in range(min(n, len(pending))):
            pending.pop(0)()

    plan = [(1, 0), (1, 1), (1, 0), (1, 1), (1, 1), (1, 1)]

    gt_ref[...] = _dot_nt(wif_ref[...], ub)
    emit_proj(1)

    pad1 = blk * ATT_GROUPS[1][1]
    k1f[0:pad1, :] = jnp.zeros((pad1, LANE), F32)
    v1f[0:pad1, :] = jnp.zeros((pad1, LANE), F32)
    k1f[pad1:pad1 + S, :] = k1_ref[...].astype(F32)
    v1f[pad1:pad1 + S, :] = v1_ref[...].astype(F32)
    q1f[...] = q1_ref[...].astype(F32)
    q2f[...] = q2_ref[...].astype(F32)
    k2f[...] = k2_ref[...].astype(F32)
    v2f[...] = v2_ref[...].astype(F32)

    def softmax_blocks(ops):
        n_mid, n_end = plan.pop(0)
        ss = [_dot_nt(q, kk) * scale + bias for (q, kk, _, bias) in ops]
        emit_proj(n_mid)
        ms = [jnp.max(s, axis=-1, keepdims=True) for s in ss]
        ps = [jnp.exp(s - m) for s, m in zip(ss, ms)]
        ls = [jnp.sum(p, axis=-1, keepdims=True) for p in ps]
        accs = [_dot(p.astype(BF16), op[2]) for p, op in zip(ps, ops)]
        emit_proj(n_end)
        return [(acc / l, m + jnp.log(l)) for acc, l, m in zip(accs, ls, ms)]

    def run_dilated(g, r, n_batches, fetch):
        for it in range(n_batches):
            fetched = [fetch(it, u) for u in range(ATT_UNROLL)]
            res = softmax_blocks([f[:4] for f in fetched])
            for (o, lse), f in zip(res, fetched):
                rows = pl.ds(f[4], blk, stride=r)
                o_scr[g - 1, rows, :] = o
                l_scr[g - 1, rows, :] = jnp.broadcast_to(lse, (blk, LANE))

    r1 = ATT_GROUPS[1][1]
    nb1 = S // r1 // blk
    assert ATT_UNROLL % nb1 == 0

    def fetch1(it, u):
        rho = it * (ATT_UNROLL // nb1) + u // nb1
        n = u % nb1
        start = rho + r1 * blk * n
        return (q1f[pl.ds(start, blk, stride=r1), :].astype(BF16),
                k1f[pl.ds(start, 2 * blk, stride=r1), :].astype(BF16),
                v1f[pl.ds(start, 2 * blk, stride=r1), :].astype(BF16),
                bias_ref[1, 1 if n == 0 else 0], start)
    run_dilated(1, r1, r1 * nb1 // ATT_UNROLL, fetch1)

    r2 = ATT_GROUPS[2][1]
    assert S // r2 == blk

    def fetch2(it, u):
        rho = it * ATT_UNROLL + u
        return (q2f[pl.ds(rho, blk, stride=r2), :].astype(BF16),
                k2f[pl.ds(rho, blk, stride=r2), :].astype(BF16),
                v2f[pl.ds(rho, blk, stride=r2), :].astype(BF16),
                bias_ref[2, 1][:, blk:2 * blk], rho)
    run_dilated(2, r2, r2 // ATT_UNROLL, fetch2)

    for it in range(S // blk // ATT_UNROLL):
        ops = []
        for u in range(ATT_UNROLL):
            n = it * ATT_UNROLL + u
            q = q0_ref[n * blk:(n + 1) * blk, :]
            if n == 0:
                ops.append((q, k0_ref[0:blk, :], v0_ref[0:blk, :], bias_ref[0, 1][:, blk:2 * blk]))
            else:
                ops.append((q, k0_ref[(n - 1) * blk:(n + 1) * blk, :], v0_ref[(n - 1) * blk:(n + 1) * blk, :],
                            bias_ref[0, 0]))
        for u, (o0, lse0) in enumerate(softmax_blocks(ops)):
            n = it * ATT_UNROLL + u
            rows = slice(n * blk, (n + 1) * blk)
            l1, l2 = l_scr[0, rows, :], l_scr[1, rows, :]
            mx = jnp.maximum(jnp.maximum(lse0, l1), l2)
            e0, e1, e2 = jnp.exp(lse0 - mx), jnp.exp(l1 - mx), jnp.exp(l2 - mx)
            att = (e0 * o0 + e1 * o_scr[0, rows, :] + e2 * o_scr[1, rows, :]) / (e0 + e1 + e2)
            o_ref[rows, :] = att.astype(o_ref.dtype)
    emit_proj(len(pending))


def _mix(u2d, w_rest, w_if_t, qkv3, bias_tiles, B, S):
    T = u2d.shape[0]
    tm = IP_TM
    tps = S // tm
    assert tps == ATT_HPG
    hpp = N_ATT_HEADS
    in_specs = [pl.BlockSpec((tm, D_MODEL), lambda b, j: (b * tps + j, 0)),
                _resident((D_MODEL, PW)),
                _resident((16, D_MODEL))]
    for g in range(N_ATT_GROUPS):
        for part in range(3):
            in_specs.append(pl.BlockSpec((None, S, LANE),
                                         lambda b, j, cb=part * hpp + g * ATT_HPG: (b, 0, cb + j)))
    in_specs.append(pl.BlockSpec((N_ATT_GROUPS, None, 2, ATT_BLK, 2 * ATT_BLK), lambda b, j: (0, j, 0, 0, 0)))
    pad1 = ATT_BLK * ATT_GROUPS[1][1]
    scratch = [pltpu.VMEM((S, LANE), F32), pltpu.VMEM((pad1 + S, LANE), F32), pltpu.VMEM((pad1 + S, LANE), F32),
               pltpu.VMEM((S, LANE), F32), pltpu.VMEM((S, LANE), F32), pltpu.VMEM((S, LANE), F32),
               pltpu.VMEM((N_ATT_GROUPS - 1, S, LANE), F32), pltpu.VMEM((N_ATT_GROUPS - 1, S, LANE), F32)]
    bias5 = bias_tiles.reshape(N_ATT_GROUPS, ATT_HPG, 2, ATT_BLK, 2 * ATT_BLK)
    return pl.pallas_call(
        functools.partial(_mix_kernel, S=S),
        grid=(B, tps),
        in_specs=in_specs,
        out_specs=[pl.BlockSpec((tm, PW), lambda b, j: (b * tps + j, 0)),
                   pl.BlockSpec((None, 16, tm), lambda b, j: (b, 0, j)),
                   pl.BlockSpec((None, S, LANE), lambda b, j: (b, 0, j))],
        out_shape=[jax.ShapeDtypeStruct((T, PW), BF16),
                   jax.ShapeDtypeStruct((B, 16, S), F32),
                   jax.ShapeDtypeStruct((B, S, ATT_W), BF16)],
        scratch_shapes=scratch,
        compiler_params=pltpu.CompilerParams(
            dimension_semantics=("parallel", "parallel"), vmem_limit_bytes=VMEM_LIMIT),
        name="mix",
    )(u2d, w_rest, w_if_t, *([qkv3] * 9), bias5)


TL_TM = 512
TL_TK = 512
ML_PAIR = 2
ML_NPAIR = ML_HEADS // ML_PAIR
ML_CPT = TL_TM // ML_L
ML_NR = ML_CPT * ML_HEADS
XS_HDR = 8


def _tail_kernel(x_ref, ada_ref, att_ref, ga_ref, gm_ref, g2_ref, wa_ref, wm_ref, wo_ref, w1_ref, w2_ref,
                 mq_ref, mk_ref, mv_ref, mo_ref, gt_ref, bif_ref, cw_ref, cb_ref, ng_ref,
                 o_ref,
                 hml_scr, xq, xk, qa, kab, kaf, cst_scr, mch_scr, rf, *, n_tiles, tiles_per_seq):
    i = pl.program_id(0)
    L = ML_L
    tm = TL_TM
    im = jnp.minimum(i, n_tiles - 1)

    @pl.when(i == 0)
    def _():
        hml_scr[...] = jnp.zeros(hml_scr.shape, hml_scr.dtype)

    @pl.when(im % tiles_per_seq == 0)
    def _():
        cst_scr[...] = jnp.zeros(cst_scr.shape, F32)
        mch_scr[...] = jnp.zeros(mch_scr.shape, F32)
        xq[tm:tm + XS_HDR, :] = jnp.zeros((XS_HDR, ML_QK_W), F32)
        xk[tm:tm + XS_HDR, :] = jnp.zeros((XS_HDR, ML_QK_W), F32)


    def merge_stage():
        y_att = _dot(att_ref[...], wa_ref[...])
        y_ml = _dot(hml_scr[...], wm_ref[...])
        ga = jax.nn.sigmoid(ga_ref[...].astype(F32))
        gm = jax.nn.sigmoid(gm_ref[...].astype(F32))
        return (ga * y_att + gm * y_ml).astype(BF16)

    def out_proj_stage(mix):
        x1 = x_ref[...] + ada_ref[2:3, :] * _dot(mix, wo_ref[...])
        ms = jnp.mean(x1 * x1, axis=-1, keepdims=True)
        y = x1 * lax.rsqrt(ms + EPS) * g2_ref[...]
        return x1, (y * (1.0 + ada_ref[4:5, :]) + ada_ref[3:4, :]).astype(BF16)

    def mlp_up(u2, k):
        hdn = jnp.maximum(_dot(u2, w1_ref[:, k * TL_TK:(k + 1) * TL_TK]), 0.0)
        return (hdn * hdn).astype(BF16)

    def mlp_down(hsq, acc, k):
        part = _dot(hsq, w2_ref[k * TL_TK:(k + 1) * TL_TK, :])
        return part if acc is None else acc + part

    mix = merge_stage()

    lane = lax.broadcasted_iota(jnp.int32, (1, LANE), 1)
    srow = lax.broadcasted_iota(jnp.int32, (LANE, 1), 0)
    in_head_lane = [(lane >= hh * ML_DK) & (lane < (hh + 1) * ML_DK) for hh in range(ML_PAIR)]
    in_head_row = [(srow >= hh * ML_DK) & (srow < (hh + 1) * ML_DK) for hh in range(ML_PAIR)]
    head0_rows = in_head_row[0]

    def conv_silu(src_ref, xs, c0):
        xs[0:XS_HDR, :] = xs[tm:tm + XS_HDR, :]
        xs[XS_HDR:XS_HDR + tm, :] = src_ref[...].astype(F32)
        yv = cb_ref[:, c0:c0 + ML_QK_W]
        for jj in range(CONV_W):
            yv = yv + cw_ref[jj:jj + 1, c0:c0 + ML_QK_W] * xs[pl.ds(XS_HDR - (CONV_W - 1) + jj, tm), :]
        return yv * jax.nn.sigmoid(yv)

    yq = conv_silu(mq_ref, xq, 0)
    for p in range(ML_NPAIR):
        for hh in range(ML_PAIR):
            qa[hh, :, p * LANE:(p + 1) * LANE] = jnp.where(in_head_lane[hh], yq[:, p * LANE:(p + 1) * LANE],
                                                           0.0).astype(BF16)
    yk = conv_silu(mk_ref, xk, ML_QK_W) * (ML_DK ** -0.5)
    kaf[...] = yk
    kab[...] = yk.astype(BF16)

    x1, u2 = out_proj_stage(mix)

    r_i = lax.broadcasted_iota(jnp.int32, (L, L), 0)
    c_i = lax.broadcasted_iota(jnp.int32, (L, L), 1)
    ones_m = jnp.ones((L, L), BF16)
    incl_upper = (r_i <= c_i).astype(BF16)
    causal = c_i <= r_i
    lane_nr = lax.broadcasted_iota(jnp.int32, (ML_NR, L), 1)
    pad_rows = jnp.zeros((L - ML_NR, L), F32)

    def col_form(row_form):
        return jnp.concatenate([row_form, pad_rows], axis=0).T

    li = jnp.concatenate([gt_ref[0:ML_HEADS, c * L:(c + 1) * L] + bif_ref[0:ML_HEADS, :]
                          for c in range(ML_CPT)], axis=0)
    zf = jnp.concatenate([gt_ref[ML_HEADS:2 * ML_HEADS, c * L:(c + 1) * L] + bif_ref[ML_HEADS:2 * ML_HEADS, :]
                          for c in range(ML_CPT)], axis=0)
    lf = jnp.minimum(zf, 0.0) - jnp.log1p(jnp.exp(-jnp.abs(zf)))
    brow = _dot3(lf, incl_upper)
    bend = _dot3(lf, ones_m)
    u = brow - li
    g = bend - u
    maxg = jnp.max(g, axis=-1, keepdims=True)
    m = mch_scr[...]
    for c in range(ML_CPT):
        rs = slice(c * ML_HEADS, (c + 1) * ML_HEADS)
        rf[1, rs, :] = m
        m = jnp.maximum(bend[rs, :] + m, maxg[rs, :])
        rf[2, rs, :] = m
    mch_scr[...] = m
    m_cur = rf[1]
    m_nxt = rf[2]
    rf[0] = u
    rf[3] = jnp.exp(bend + m_cur - m_nxt)
    rf[4] = jnp.exp(g - m_nxt)
    pm = -u
    sh = 1
    while sh < L:
        pm = jnp.maximum(pm, jnp.where(lane_nr >= sh, pltpu.roll(pm, sh, axis=1), NEG))
        sh *= 2
    d1 = -jnp.maximum(m_cur, pm)
    dcol = col_form(d1)
    ecol = col_form(jnp.exp(d1 - brow))

    ones_v = jnp.ones((L, LANE), BF16)

    def prepare(c, p):
        rows = slice(c * L, (c + 1) * L)
        pl_ = slice(p * LANE, (p + 1) * LANE)
        r0 = c * ML_HEADS + p * ML_PAIR
        kc = kab[rows, pl_]
        kt = kaf[rows, pl_].T
        wk = jnp.where(head0_rows, rf[4, r0:r0 + 1, :], rf[4, r0 + 1:r0 + 2, :])
        ktw = (kt * wk).astype(BF16)
        ktw2 = jnp.concatenate([jnp.where(in_head_row[hh], ktw, jnp.zeros_like(ktw)) for hh in range(ML_PAIR)],
                               axis=1)
        vexts = [jnp.concatenate([mv_ref[rows, (r0 % ML_HEADS + hh) * LANE:(r0 % ML_HEADS + hh + 1) * LANE],
                                  ones_v], axis=1) for hh in range(ML_PAIR)]
        z = _dot(ktw2, jnp.concatenate(vexts, axis=0))
        ss = [_dot_nt(qa[hh, rows, pl_], kc) for hh in range(ML_PAIR)]
        drow = [jnp.broadcast_to(dcol[:, r0 + hh:r0 + hh + 1], (L, L)) for hh in range(ML_PAIR)]
        erow = [jnp.broadcast_to(ecol[:, r0 + hh:r0 + hh + 1], (L, L)) for hh in range(ML_PAIR)]
        return ss, z, vexts, drow, erow

    order = [(c, p) for c in range(ML_CPT) for p in range(ML_NPAIR)]
    assert 2 * (D_FF // TL_TK) == len(order)
    cst = [cst_scr[p] for p in range(ML_NPAIR)]
    nxt = prepare(*order[0])
    acc = None
    hsq = None
    for n, (c, p) in enumerate(order):
        rows = slice(c * L, (c + 1) * L)
        pl_ = slice(p * LANE, (p + 1) * LANE)
        r0 = c * ML_HEADS + p * ML_PAIR
        ss, z, vexts, drow, erow = nxt
        if n + 1 < len(order):
            nxt = prepare(*order[n + 1])
        cst_b = cst[p].astype(BF16)
        lhs = []
        for hh in range(ML_PAIR):
            r = r0 + hh
            w = ss[hh] * jnp.exp(jnp.where(causal, drow[hh] - rf[0, r:r + 1, :], NEG))
            qi = qa[hh, rows, pl_].astype(F32) * jnp.exp(drow[hh] + rf[1, r:r + 1, :])
            lhs.append(jnp.concatenate([w.astype(BF16), qi.astype(BF16)], axis=1))
        if n % 2 == 0:
            hsq = mlp_up(u2, n // 2)
        else:
            acc = mlp_down(hsq, acc, n // 2)
        for hh in range(ML_PAIR):
            cols = slice((p * ML_PAIR + hh) * LANE, (p * ML_PAIR + hh + 1) * LANE)
            tot = _dot(lhs[hh], jnp.concatenate([vexts[hh], cst_b], axis=0))
            hval = tot[:, 0:LANE] / jnp.maximum(jnp.abs(tot[:, LANE:2 * LANE]), erow[hh])
            msq = jnp.mean(hval * hval, axis=-1, keepdims=True)
            hn = hval * lax.rsqrt(msq + EPS) * ng_ref[:, cols]
            hml_scr[rows, cols] = (hn * jax.nn.sigmoid(mo_ref[rows, cols].astype(F32))).astype(hml_scr.dtype)
        dec = jnp.where(head0_rows, rf[3, r0:r0 + 1, :], rf[3, r0 + 1:r0 + 2, :])
        cst[p] = jnp.concatenate([dec, dec], axis=1) * cst[p] + z
    for p in range(ML_NPAIR):
        cst_scr[p] = cst[p]
    o_ref[...] = x1 + ada_ref[5:6, :] * acc


def _tail(x2, ada3, att2, proj2, gates_t, bif_b, conv_w, conv_b, norm_g, g2, wa, wm, wo, w1, w2, S):
    T = x2.shape[0]
    tm = TL_TM
    tps = S // tm
    n_tiles = T // tm
    cur = lambda i: jnp.minimum(i, n_tiles - 1)
    prv = lambda i: jnp.maximum(i - 1, 0)
    gcb = CB_G * LANE // D_MODEL
    in_specs = [
        pl.BlockSpec((tm, D_MODEL), lambda i: (prv(i), 0)),
        pl.BlockSpec((None, 6, D_MODEL), lambda i: (prv(i) // tps, 0, 0)),
        pl.BlockSpec((tm, ATT_W), lambda i: (prv(i), 0)),
        pl.BlockSpec((tm, D_MODEL), lambda i: (prv(i), gcb)),
        pl.BlockSpec((tm, D_MODEL), lambda i: (prv(i), gcb + 1)),
        _resident((1, D_MODEL)),
        _resident((ATT_W, D_MODEL)), _resident((ML_V_W, D_MODEL)), _resident((D_MODEL, D_MODEL)),
        _resident((D_MODEL, D_FF)), _resident((D_FF, D_MODEL)),
        pl.BlockSpec((tm, ML_QK_W), lambda i: (cur(i), CB_MQ * LANE // ML_QK_W)),
        pl.BlockSpec((tm, ML_QK_W), lambda i: (cur(i), CB_MK * LANE // ML_QK_W)),
        pl.BlockSpec((tm, ML_V_W), lambda i: (cur(i), CB_MV * LANE // ML_V_W)),
        pl.BlockSpec((tm, ML_V_W), lambda i: (cur(i), CB_MO * LANE // ML_V_W)),
        pl.BlockSpec((None, 2 * ML_HEADS, tm), lambda i: (cur(i) // tps, 0, cur(i) % tps)),
        _resident((2 * ML_HEADS, LANE)),
        _resident((CONV_W, 2 * ML_QK_W)), _resident((1, 2 * ML_QK_W)), _resident((1, ML_V_W)),
    ]
    scratch = [pltpu.VMEM((tm, ML_V_W), BF16),
               pltpu.VMEM((tm + XS_HDR, ML_QK_W), F32), pltpu.VMEM((tm + XS_HDR, ML_QK_W), F32),
               pltpu.VMEM((ML_PAIR, tm, ML_QK_W), BF16), pltpu.VMEM((tm, ML_QK_W), BF16),
               pltpu.VMEM((tm, ML_QK_W), F32),
               pltpu.VMEM((ML_NPAIR, LANE, 2 * LANE), F32), pltpu.VMEM((ML_HEADS, LANE), F32),
               pltpu.VMEM((5, ML_NR, ML_L), F32)]
    return pl.pallas_call(
        functools.partial(_tail_kernel, n_tiles=n_tiles, tiles_per_seq=tps),
        grid=(n_tiles + 1,),
        in_specs=in_specs,
        out_specs=pl.BlockSpec((tm, D_MODEL), lambda i: (prv(i), 0)),
        out_shape=jax.ShapeDtypeStruct((T, D_MODEL), F32),
        scratch_shapes=scratch,
        compiler_params=pltpu.CompilerParams(
            dimension_semantics=("arbitrary",), vmem_limit_bytes=VMEM_LIMIT),
        name="tail",
    )(x2, ada3, att2, proj2, proj2, g2, wa, wm, wo, w1, w2,
      proj2, proj2, proj2, proj2, gates_t, bif_b, conv_w, conv_b, norm_g)


def kernel(x, c, w_ada, b_ada, norm1_g, norm2_g, w_in, b_if, conv_w, conv_b, q_norm_g, k_norm_g,
           rel_bias, mlstm_norm_g, w_att_out, w_ml_out, w_out, w_ff1, w_ff2):
    B, S, D = x.shape
    T = B * S
    depth = w_ada.shape[0]
    bias_tiles = _bias_tiles(rel_bias)
    x2 = x.reshape(T, D)
    for l in range(depth):
        ada3 = _ada(c, w_ada[l].astype(BF16), b_ada[l]).reshape(B, 6, D)
        w_in_b = w_in[l].astype(BF16)
        w_rest = jnp.concatenate([w_in_b[:, lo:hi] for lo, hi in (WIN_G, WIN_MVO, WIN_MQK)], axis=1)
        u2d, qkv = _inproj_att(x2, ada3, norm1_g[l].reshape(1, D), w_in_b[:, WIN_ATT[0]:WIN_ATT[1]],
                               q_norm_g[l].reshape(1, ATT_DH), k_norm_g[l].reshape(1, ATT_DH), S)
        proj2, gates_t, att = _mix(u2d, w_rest, w_in_b[:, WIN_IF[0]:WIN_IF[1]].T, qkv.reshape(B, S, AW),
                                   bias_tiles, B, S)
        bif_b = jnp.broadcast_to(b_if[l].reshape(2 * ML_HEADS, 1), (2 * ML_HEADS, LANE))
        x2 = _tail(x2, ada3, att.reshape(T, ATT_W), proj2, gates_t, bif_b, conv_w[l], conv_b[l].reshape(1, -1),
                   mlstm_norm_g[l].reshape(1, ML_V_W), norm2_g[l].reshape(1, D),
                   w_att_out[l].astype(BF16), w_ml_out[l].astype(BF16), w_out[l].astype(BF16),
                   w_ff1[l].astype(BF16), w_ff2[l].astype(BF16), S)
    return x2.reshape(B, S, D)
```

```python
import functools
import math

import numpy as np
import jax
import jax.numpy as jnp
from jax import lax
from jax.experimental import pallas as pl
from jax.experimental.pallas import tpu as pltpu

F32 = jnp.float32
BF16 = jnp.bfloat16

D_MODEL = 1024
ATT_GROUPS = ((128, 1), (512, 4), (2048, 16))
N_ATT_GROUPS = 3
ATT_HPG = 4
ATT_DH = 128
N_ATT_HEADS = 12
ATT_W = 512
ATT_BLK = 128
ML_HEADS = 8
ML_DK = 64
ML_DV = 128
ML_QK_W = 512
ML_V_W = 1024
CONV_W = 4
D_FF = 4096
N_BUCKETS = 32
MAX_DISTANCE = 2048
EPS = 1e-6
NEG = -1e30

LANE = 128
ML_L = 128

AW = 3 * N_ATT_HEADS * ATT_DH
PW = 5120
CB_G, CB_MV, CB_MO = 0, 16, 24
CB_MQ, CB_MK = 32, 36
WIN_ATT, WIN_MQK, WIN_MVO, WIN_IF, WIN_G = (0, 4608), (4608, 5632), (5632, 7680), (7680, 7696), (7696, 9744)

VMEM_LIMIT = 56 * 1024 * 1024


def _dot(a, b):
    return jnp.dot(a, b, preferred_element_type=F32)


def _dot_nt(a, b):
    return lax.dot_general(a, b, (((1,), (1,)), ((), ())), preferred_element_type=F32)


def _split3(a):
    hi = a.astype(BF16)
    r1 = a - hi.astype(F32)
    mid = r1.astype(BF16)
    lo = (r1 - mid.astype(F32)).astype(BF16)
    return hi, mid, lo


def _dot3(a, rhs_bf16):
    hi, mid, lo = _split3(a)
    return _dot(hi, rhs_bf16) + _dot(mid, rhs_bf16) + _dot(lo, rhs_bf16)


def _resident(shape):
    return pl.BlockSpec(shape, lambda *_: (0,) * len(shape), pipeline_mode=pl.Buffered(1))


def _ada_kernel(c_ref, w_ref, b_ref, o_ref):
    c = c_ref[...]
    s = c * jax.nn.sigmoid(c)
    o_ref[...] = _dot(s.astype(BF16), w_ref[...].astype(BF16)) + b_ref[...]


def _ada(c, w, b):
    B = c.shape[0]
    n = w.shape[1]
    tn = 1024
    return pl.pallas_call(
        _ada_kernel,
        grid=(n // tn,),
        in_specs=[pl.BlockSpec((B, D_MODEL), lambda j: (0, 0)),
                  pl.BlockSpec((D_MODEL, tn), lambda j: (0, j)),
                  pl.BlockSpec((1, tn), lambda j: (0, j))],
        out_specs=pl.BlockSpec((B, tn), lambda j: (0, j)),
        out_shape=jax.ShapeDtypeStruct((B, n), F32),
        name="ada",
    )(c, w, b.reshape(1, n))


def _t5_bucket_np(dist):
    max_exact = N_BUCKETS // 2
    d = np.maximum(dist, max_exact).astype(np.float32)
    large = max_exact + (np.log(d / np.float32(max_exact)) / np.float32(math.log(MAX_DISTANCE / max_exact))
                         * np.float32(N_BUCKETS - max_exact)).astype(np.int32)
    large = np.minimum(large, N_BUCKETS - 1)
    return np.where(dist < max_exact, dist, large).astype(np.int32)


def _bucket_tiles():
    i = np.arange(ATT_BLK)[:, None]
    j = np.arange(2 * ATT_BLK)[None, :]
    delta = ATT_BLK + i - j
    return np.stack([_t5_bucket_np(np.maximum(delta, 0) * dil) for _, dil in ATT_GROUPS])


def _bias_kernel(tab_ref, bucket_ref, o_ref):
    hh = pl.program_id(0)
    bucket = bucket_ref[...]
    acc = jnp.zeros(bucket.shape, F32)
    for k in range(N_BUCKETS):
        acc = jnp.where(bucket == k, tab_ref[k, hh], acc)
    i = lax.broadcasted_iota(jnp.int32, bucket.shape, 0)
    j = lax.broadcasted_iota(jnp.int32, bucket.shape, 1)
    delta = ATT_BLK + i - j
    valid = (delta >= 0) & (delta <= ATT_BLK)
    o_ref[0] = jnp.where(valid, acc, NEG)
    o_ref[1] = jnp.where(valid & (j >= ATT_BLK), acc, NEG)


def _bias_tiles(rel_bias):
    buckets = jnp.asarray(_bucket_tiles())
    return pl.pallas_call(
        _bias_kernel,
        grid=(N_ATT_HEADS,),
        in_specs=[pl.BlockSpec(memory_space=pltpu.SMEM),
                  pl.BlockSpec((None, ATT_BLK, 2 * ATT_BLK), lambda h: (h // ATT_HPG, 0, 0))],
        out_specs=pl.BlockSpec((None, 2, ATT_BLK, 2 * ATT_BLK), lambda h: (h, 0, 0, 0)),
        out_shape=jax.ShapeDtypeStruct((N_ATT_HEADS, 2, ATT_BLK, 2 * ATT_BLK), F32),
        name="bias_tiles",
    )(rel_bias, buckets)


IP_TM = 512
IP_TN = 512
IP_NQ = N_ATT_HEADS * ATT_DH // IP_TN


def _inproj_att_kernel(x_ref, ada_ref, g1_ref, w_ref, qg_ref, kg_ref, u_ref, o_ref):
    x = x_ref[...]
    ms = jnp.mean(x * x, axis=-1, keepdims=True)
    y = x * lax.rsqrt(ms + EPS) * g1_ref[...]
    ub = (y * (1.0 + ada_ref[1:2, :]) + ada_ref[0:1, :]).astype(BF16)
    u_ref[...] = ub
    for j in range(AW // IP_TN):
        acc = _dot(ub, w_ref[:, j * IP_TN:(j + 1) * IP_TN])
        if j < 2 * IP_NQ:
            gain = qg_ref[...] if j < IP_NQ else kg_ref[...]
            for k in range(IP_TN // ATT_DH):
                a = acc[:, k * ATT_DH:(k + 1) * ATT_DH]
                ms = jnp.mean(a * a, axis=-1, keepdims=True)
                lo = j * IP_TN + k * ATT_DH
                o_ref[:, lo:lo + ATT_DH] = (a * lax.rsqrt(ms + EPS) * gain).astype(o_ref.dtype)
        else:
            o_ref[:, j * IP_TN:(j + 1) * IP_TN] = acc.astype(o_ref.dtype)


def _inproj_att(x2, ada3, g1, w_att, qg, kg, S):
    T = x2.shape[0]
    tm = IP_TM
    tiles_per_seq = S // tm
    return pl.pallas_call(
        _inproj_att_kernel,
        grid=(T // tm,),
        in_specs=[pl.BlockSpec((tm, D_MODEL), lambda i: (i, 0)),
                  pl.BlockSpec((None, 6, D_MODEL), lambda i: (i // tiles_per_seq, 0, 0)),
                  _resident((1, D_MODEL)),
                  _resident((D_MODEL, AW)),
                  _resident((1, ATT_DH)),
                  _resident((1, ATT_DH))],
        out_specs=[pl.BlockSpec((tm, D_MODEL), lambda i: (i, 0)),
                   pl.BlockSpec((tm, AW), lambda i: (i, 0))],
        out_shape=[jax.ShapeDtypeStruct((T, D_MODEL), BF16),
                   jax.ShapeDtypeStruct((T, AW), BF16)],
        compiler_params=pltpu.CompilerParams(
            dimension_semantics=("parallel",), vmem_limit_bytes=VMEM_LIMIT),
        name="inproj_att",
    )(x2, ada3, g1, w_att, qg, kg)


ATT_UNROLL = 8


def _mix_kernel(u_ref, wg_ref, *rest, S):
    n_wb = len(MIX_WBLOCKS)
    _mix_body(u_ref, wg_ref, rest[:n_wb], *rest[n_wb:], S=S)


def _mix_body(u_ref, wg_ref, wb_refs, wif_ref,
                q0_ref, k0_ref, v0_ref, q1_ref, k1_ref, v1_ref, q2_ref, k2_ref, v2_ref, bias_ref,
                p_ref, gt_ref, o_ref,
                q1f, k1f, v1f, q2f, k2f, v2f, o_scr, l_scr, *, S):
    scale = ATT_DH ** -0.5
    blk = ATT_BLK

    ub = u_ref[...]

    n_g = (WIN_G[1] - WIN_G[0]) // IP_TN

    def proj_chunk(j):
        w = wg_ref[:, j * IP_TN:(j + 1) * IP_TN] if j < n_g else wb_refs[j - n_g][...]
        p_ref[:, j * IP_TN:(j + 1) * IP_TN] = _dot(ub, w).astype(p_ref.dtype)
    pending = [functools.partial(proj_chunk, j) for j in range(PW // IP_TN)]

    def emit_proj(n):
        for _ in range(min(n, len(pending))):
            pending.pop(0)()

    plan = [(1, 0), (1, 1), (1, 0), (1, 1), (1, 1), (1, 1)]

    gt_ref[...] = _dot_nt(wif_ref[...], ub)
    emit_proj(1)

    pad1 = blk * ATT_GROUPS[1][1]
    k1f[0:pad1, :] = jnp.zeros((pad1, LANE), F32)
    v1f[0:pad1, :] = jnp.zeros((pad1, LANE), F32)
    k1f[pad1:pad1 + S, :] = k1_ref[...].astype(F32)
    v1f[pad1:pad1 + S, :] = v1_ref[...].astype(F32)
    q1f[...] = q1_ref[...].astype(F32)
    q2f[...] = q2_ref[...].astype(F32)
    k2f[...] = k2_ref[...].astype(F32)
    v2f[...] = v2_ref[...].astype(F32)

    def softmax_blocks(ops):
        n_mid, n_end = plan.pop(0)
        ss = [_dot_nt(q, kk) * scale + bias for (q, kk, _, bias) in ops]
        emit_proj(n_mid)
        ms = [jnp.max(s, axis=-1, keepdims=True) for s in ss]
        ps = [jnp.exp(s - m) for s, m in zip(ss, ms)]
        ls = [jnp.sum(p, axis=-1, keepdims=True) for p in ps]
        accs = [_dot(p.astype(BF16), op[2]) for p, op in zip(ps, ops)]
        emit_proj(n_end)
        return [(acc / l, m + jnp.log(l)) for acc, l, m in zip(accs, ls, ms)]

    def run_dilated(g, r, n_batches, fetch):
        for it in range(n_batches):
            fetched = [fetch(it, u) for u in range(ATT_UNROLL)]
            res = softmax_blocks([f[:4] for f in fetched])
            for (o, lse), f in zip(res, fetched):
                rows = pl.ds(f[4], blk, stride=r)
                o_scr[g - 1, rows, :] = o
                l_scr[g - 1, rows, :] = jnp.broadcast_to(lse, (blk, LANE))

    r1 = ATT_GROUPS[1][1]
    nb1 = S // r1 // blk
    assert ATT_UNROLL % nb1 == 0

    def fetch1(it, u):
        rho = it * (ATT_UNROLL // nb1) + u // nb1
        n = u % nb1
        start = rho + r1 * blk * n
        return (q1f[pl.ds(start, blk, stride=r1), :].astype(BF16),
                k1f[pl.ds(start, 2 * blk, stride=r1), :].astype(BF16),
                v1f[pl.ds(start, 2 * blk, stride=r1), :].astype(BF16),
                bias_ref[1, 1 if n == 0 else 0], start)
    run_dilated(1, r1, r1 * nb1 // ATT_UNROLL, fetch1)

    r2 = ATT_GROUPS[2][1]
    assert S // r2 == blk

    def fetch2(it, u):
        rho = it * ATT_UNROLL + u
        return (q2f[pl.ds(rho, blk, stride=r2), :].astype(BF16),
                k2f[pl.ds(rho, blk, stride=r2), :].astype(BF16),
                v2f[pl.ds(rho, blk, stride=r2), :].astype(BF16),
                bias_ref[2, 1][:, blk:2 * blk], rho)
    run_dilated(2, r2, r2 // ATT_UNROLL, fetch2)

    for it in range(S // blk // ATT_UNROLL):
        ops = []
        for u in range(ATT_UNROLL):
            n = it * ATT_UNROLL + u
            q = q0_ref[n * blk:(n + 1) * blk, :]
            if n == 0:
                ops.append((q, k0_ref[0:blk, :], v0_ref[0:blk, :], bias_ref[0, 1][:, blk:2 * blk]))
            else:
                ops.append((q, k0_ref[(n - 1) * blk:(n + 1) * blk, :], v0_ref[(n - 1) * blk:(n + 1) * blk, :],
                            bias_ref[0, 0]))
        for u, (o0, lse0) in enumerate(softmax_blocks(ops)):
            n = it * ATT_UNROLL + u
            rows = slice(n * blk, (n + 1) * blk)
            l1, l2 = l_scr[0, rows, :], l_scr[1, rows, :]
            mx = jnp.maximum(jnp.maximum(lse0, l1), l2)
            e0, e1, e2 = jnp.exp(lse0 - mx), jnp.exp(l1 - mx), jnp.exp(l2 - mx)
            att = (e0 * o0 + e1 * o_scr[0, rows, :] + e2 * o_scr[1, rows, :]) / (e0 + e1 + e2)
            o_ref[rows, :] = att.astype(o_ref.dtype)
    emit_proj(len(pending))


MIX_WBLOCKS = tuple(range(WIN_MVO[0] // IP_TN, WIN_MVO[1] // IP_TN)) + tuple(
    range(WIN_MQK[0] // IP_TN, WIN_MQK[1] // IP_TN))


def _mix(u2d, w_gates, w_in_b, w_if_t, qkv3, bias_tiles, B, S):
    T = u2d.shape[0]
    tm = IP_TM
    tps = S // tm
    assert tps == ATT_HPG
    hpp = N_ATT_HEADS
    in_specs = [pl.BlockSpec((tm, D_MODEL), lambda b, j: (b * tps + j, 0)),
                _resident((D_MODEL, WIN_G[1] - WIN_G[0]))]
    in_specs += [pl.BlockSpec((D_MODEL, IP_TN), lambda b, j, cb=cb: (0, cb), pipeline_mode=pl.Buffered(1))
                 for cb in MIX_WBLOCKS]
    in_specs.append(_resident((16, D_MODEL)))
    for g in range(N_ATT_GROUPS):
        for part in range(3):
            in_specs.append(pl.BlockSpec((None, S, LANE),
                                         lambda b, j, cb=part * hpp + g * ATT_HPG: (b, 0, cb + j)))
    in_specs.append(pl.BlockSpec((N_ATT_GROUPS, None, 2, ATT_BLK, 2 * ATT_BLK), lambda b, j: (0, j, 0, 0, 0)))
    pad1 = ATT_BLK * ATT_GROUPS[1][1]
    scratch = [pltpu.VMEM((S, LANE), F32), pltpu.VMEM((pad1 + S, LANE), F32), pltpu.VMEM((pad1 + S, LANE), F32),
               pltpu.VMEM((S, LANE), F32), pltpu.VMEM((S, LANE), F32), pltpu.VMEM((S, LANE), F32),
               pltpu.VMEM((N_ATT_GROUPS - 1, S, LANE), F32), pltpu.VMEM((N_ATT_GROUPS - 1, S, LANE), F32)]
    bias5 = bias_tiles.reshape(N_ATT_GROUPS, ATT_HPG, 2, ATT_BLK, 2 * ATT_BLK)
    return pl.pallas_call(
        functools.partial(_mix_kernel, S=S),
        grid=(B, tps),
        in_specs=in_specs,
        out_specs=[pl.BlockSpec((tm, PW), lambda b, j: (b * tps + j, 0)),
                   pl.BlockSpec((None, 16, tm), lambda b, j: (b, 0, j)),
                   pl.BlockSpec((None, S, LANE), lambda b, j: (b, 0, j))],
        out_shape=[jax.ShapeDtypeStruct((T, PW), BF16),
                   jax.ShapeDtypeStruct((B, 16, S), F32),
                   jax.ShapeDtypeStruct((B, S, ATT_W), BF16)],
        scratch_shapes=scratch,
        compiler_params=pltpu.CompilerParams(
            dimension_semantics=("parallel", "parallel"), vmem_limit_bytes=VMEM_LIMIT),
        name="mix",
    )(u2d, w_gates, *([w_in_b] * len(MIX_WBLOCKS)), w_if_t, *([qkv3] * 9), bias5)


TL_TM = 512
TL_TK = 512
ML_PAIR = 2
ML_NPAIR = ML_HEADS // ML_PAIR
ML_CPT = TL_TM // ML_L
ML_NR = ML_CPT * ML_HEADS
XS_HDR = 8


def _tail_kernel(x_ref, ada_ref, att_ref, ga_ref, gm_ref, g2_ref, wa_ref, wm_ref, wo_ref, w1_ref, w2_ref,
                 mq_ref, mk_ref, mv_ref, mo_ref, gt_ref, bif_ref, cw_ref, cb_ref, ng_ref,
                 o_ref,
                 hml_scr, xq, xk, qa, kab, kaf, cst_scr, mch_scr, rf, *, n_tiles, tiles_per_seq):
    i = pl.program_id(0)
    L = ML_L
    tm = TL_TM
    im = jnp.minimum(i, n_tiles - 1)

    @pl.when(i == 0)
    def _():
        hml_scr[...] = jnp.zeros(hml_scr.shape, hml_scr.dtype)

    @pl.when(im % tiles_per_seq == 0)
    def _():
        cst_scr[...] = jnp.zeros(cst_scr.shape, F32)
        mch_scr[...] = jnp.zeros(mch_scr.shape, F32)
        xq[tm:tm + XS_HDR, :] = jnp.zeros((XS_HDR, ML_QK_W), F32)
        xk[tm:tm + XS_HDR, :] = jnp.zeros((XS_HDR, ML_QK_W), F32)


    def merge_stage():
        y_att = _dot(att_ref[...], wa_ref[...])
        y_ml = _dot(hml_scr[...], wm_ref[...])
        ga = jax.nn.sigmoid(ga_ref[...].astype(F32))
        gm = jax.nn.sigmoid(gm_ref[...].astype(F32))
        return (ga * y_att + gm * y_ml).astype(BF16)

    def out_proj_stage(mix):
        x1 = x_ref[...] + ada_ref[2:3, :] * _dot(mix, wo_ref[...])
        ms = jnp.mean(x1 * x1, axis=-1, keepdims=True)
        y = x1 * lax.rsqrt(ms + EPS) * g2_ref[...]
        return x1, (y * (1.0 + ada_ref[4:5, :]) + ada_ref[3:4, :]).astype(BF16)

    def mlp_up(u2, k):
        hdn = jnp.maximum(_dot(u2, w1_ref[:, k * TL_TK:(k + 1) * TL_TK]), 0.0)
        return (hdn * hdn).astype(BF16)

    def mlp_down(hsq, acc, k):
        part = _dot(hsq, w2_ref[k * TL_TK:(k + 1) * TL_TK, :])
        return part if acc is None else acc + part

    mix = merge_stage()

    lane = lax.broadcasted_iota(jnp.int32, (1, LANE), 1)
    srow = lax.broadcasted_iota(jnp.int32, (LANE, 1), 0)
    in_head_lane = [(lane >= hh * ML_DK) & (lane < (hh + 1) * ML_DK) for hh in range(ML_PAIR)]
    in_head_row = [(srow >= hh * ML_DK) & (srow < (hh + 1) * ML_DK) for hh in range(ML_PAIR)]
    head0_rows = in_head_row[0]

    def conv_silu(src_ref, xs, c0):
        xs[0:XS_HDR, :] = xs[tm:tm + XS_HDR, :]
        xs[XS_HDR:XS_HDR + tm, :] = src_ref[...].astype(F32)
        yv = cb_ref[:, c0:c0 + ML_QK_W]
        for jj in range(CONV_W):
            yv = yv + cw_ref[jj:jj + 1, c0:c0 + ML_QK_W] * xs[pl.ds(XS_HDR - (CONV_W - 1) + jj, tm), :]
        return yv * jax.nn.sigmoid(yv)

    yq = conv_silu(mq_ref, xq, 0)
    for p in range(ML_NPAIR):
        for hh in range(ML_PAIR):
            qa[hh, :, p * LANE:(p + 1) * LANE] = jnp.where(in_head_lane[hh], yq[:, p * LANE:(p + 1) * LANE],
                                                           0.0).astype(BF16)
    yk = conv_silu(mk_ref, xk, ML_QK_W) * (ML_DK ** -0.5)
    kaf[...] = yk
    kab[...] = yk.astype(BF16)

    x1, u2 = out_proj_stage(mix)

    r_i = lax.broadcasted_iota(jnp.int32, (L, L), 0)
    c_i = lax.broadcasted_iota(jnp.int32, (L, L), 1)
    ones_m = jnp.ones((L, L), BF16)
    incl_upper = (r_i <= c_i).astype(BF16)
    causal = c_i <= r_i
    lane_nr = lax.broadcasted_iota(jnp.int32, (ML_NR, L), 1)
    pad_rows = jnp.zeros((L - ML_NR, L), F32)

    def col_form(row_form):
        return jnp.concatenate([row_form, pad_rows], axis=0).T

    li = jnp.concatenate([gt_ref[0:ML_HEADS, c * L:(c + 1) * L] + bif_ref[0:ML_HEADS, :]
                          for c in range(ML_CPT)], axis=0)
    zf = jnp.concatenate([gt_ref[ML_HEADS:2 * ML_HEADS, c * L:(c + 1) * L] + bif_ref[ML_HEADS:2 * ML_HEADS, :]
                          for c in range(ML_CPT)], axis=0)
    lf = jnp.minimum(zf, 0.0) - jnp.log1p(jnp.exp(-jnp.abs(zf)))
    brow = _dot3(lf, incl_upper)
    bend = _dot3(lf, ones_m)
    u = brow - li
    g = bend - u
    maxg = jnp.max(g, axis=-1, keepdims=True)
    m = mch_scr[...]
    for c in range(ML_CPT):
        rs = slice(c * ML_HEADS, (c + 1) * ML_HEADS)
        rf[1, rs, :] = m
        m = jnp.maximum(bend[rs, :] + m, maxg[rs, :])
        rf[2, rs, :] = m
    mch_scr[...] = m
    m_cur = rf[1]
    m_nxt = rf[2]
    rf[0] = u
    rf[3] = jnp.exp(bend + m_cur - m_nxt)
    rf[4] = jnp.exp(g - m_nxt)
    pm = -u
    sh = 1
    while sh < L:
        pm = jnp.maximum(pm, jnp.where(lane_nr >= sh, pltpu.roll(pm, sh, axis=1), NEG))
        sh *= 2
    d1 = -jnp.maximum(m_cur, pm)
    dcol = col_form(d1)
    ecol = col_form(jnp.exp(d1 - brow))

    ones_v = jnp.ones((L, LANE), BF16)

    def prepare(c, p):
        rows = slice(c * L, (c + 1) * L)
        pl_ = slice(p * LANE, (p + 1) * LANE)
        r0 = c * ML_HEADS + p * ML_PAIR
        kc = kab[rows, pl_]
        kt = kaf[rows, pl_].T
        wk = jnp.where(head0_rows, rf[4, r0:r0 + 1, :], rf[4, r0 + 1:r0 + 2, :])
        ktw = (kt * wk).astype(BF16)
        ktw2 = jnp.concatenate([jnp.where(in_head_row[hh], ktw, jnp.zeros_like(ktw)) for hh in range(ML_PAIR)],
                               axis=1)
        vexts = [jnp.concatenate([mv_ref[rows, (r0 % ML_HEADS + hh) * LANE:(r0 % ML_HEADS + hh + 1) * LANE],
                                  ones_v], axis=1) for hh in range(ML_PAIR)]
        z = _dot(ktw2, jnp.concatenate(vexts, axis=0))
        ss = [_dot_nt(qa[hh, rows, pl_], kc) for hh in range(ML_PAIR)]
        drow = [jnp.broadcast_to(dcol[:, r0 + hh:r0 + hh + 1], (L, L)) for hh in range(ML_PAIR)]
        erow = [jnp.broadcast_to(ecol[:, r0 + hh:r0 + hh + 1], (L, L)) for hh in range(ML_PAIR)]
        return ss, z, vexts, drow, erow

    order = [(c, p) for c in range(ML_CPT) for p in range(ML_NPAIR)]
    assert 2 * (D_FF // TL_TK) == len(order)
    cst = [cst_scr[p] for p in range(ML_NPAIR)]
    nxt = prepare(*order[0])
    acc = None
    hsq = None
    for n, (c, p) in enumerate(order):
        rows = slice(c * L, (c + 1) * L)
        pl_ = slice(p * LANE, (p + 1) * LANE)
        r0 = c * ML_HEADS + p * ML_PAIR
        ss, z, vexts, drow, erow = nxt
        if n + 1 < len(order):
            nxt = prepare(*order[n + 1])
        cst_b = cst[p].astype(BF16)
        lhs = []
        for hh in range(ML_PAIR):
            r = r0 + hh
            w = ss[hh] * jnp.exp(jnp.where(causal, drow[hh] - rf[0, r:r + 1, :], NEG))
            qi = qa[hh, rows, pl_].astype(F32) * jnp.exp(drow[hh] + rf[1, r:r + 1, :])
            lhs.append(jnp.concatenate([w.astype(BF16), qi.astype(BF16)], axis=1))
        if n % 2 == 0:
            hsq = mlp_up(u2, n // 2)
        else:
            acc = mlp_down(hsq, acc, n // 2)
        for hh in range(ML_PAIR):
            cols = slice((p * ML_PAIR + hh) * LANE, (p * ML_PAIR + hh + 1) * LANE)
            tot = _dot(lhs[hh], jnp.concatenate([vexts[hh], cst_b], axis=0))
            hval = tot[:, 0:LANE] / jnp.maximum(jnp.abs(tot[:, LANE:2 * LANE]), erow[hh])
            msq = jnp.mean(hval * hval, axis=-1, keepdims=True)
            hn = hval * lax.rsqrt(msq + EPS) * ng_ref[:, cols]
            hml_scr[rows, cols] = (hn * jax.nn.sigmoid(mo_ref[rows, cols].astype(F32))).astype(hml_scr.dtype)
        dec = jnp.where(head0_rows, rf[3, r0:r0 + 1, :], rf[3, r0 + 1:r0 + 2, :])
        cst[p] = jnp.concatenate([dec, dec], axis=1) * cst[p] + z
    for p in range(ML_NPAIR):
        cst_scr[p] = cst[p]
    o_ref[...] = x1 + ada_ref[5:6, :] * acc


def _tail(x2, ada3, att2, proj2, gates_t, bif_b, conv_w, conv_b, norm_g, g2, wa, wm, wo, w1, w2, S):
    T = x2.shape[0]
    tm = TL_TM
    tps = S // tm
    n_tiles = T // tm
    cur = lambda i: jnp.minimum(i, n_tiles - 1)
    prv = lambda i: jnp.maximum(i - 1, 0)
    gcb = CB_G * LANE // D_MODEL
    in_specs = [
        pl.BlockSpec((tm, D_MODEL), lambda i: (prv(i), 0)),
        pl.BlockSpec((None, 6, D_MODEL), lambda i: (prv(i) // tps, 0, 0)),
        pl.BlockSpec((tm, ATT_W), lambda i: (prv(i), 0)),
        pl.BlockSpec((tm, D_MODEL), lambda i: (prv(i), gcb)),
        pl.BlockSpec((tm, D_MODEL), lambda i: (prv(i), gcb + 1)),
        _resident((1, D_MODEL)),
        _resident((ATT_W, D_MODEL)), _resident((ML_V_W, D_MODEL)), _resident((D_MODEL, D_MODEL)),
        _resident((D_MODEL, D_FF)), _resident((D_FF, D_MODEL)),
        pl.BlockSpec((tm, ML_QK_W), lambda i: (cur(i), CB_MQ * LANE // ML_QK_W)),
        pl.BlockSpec((tm, ML_QK_W), lambda i: (cur(i), CB_MK * LANE // ML_QK_W)),
        pl.BlockSpec((tm, ML_V_W), lambda i: (cur(i), CB_MV * LANE // ML_V_W)),
        pl.BlockSpec((tm, ML_V_W), lambda i: (cur(i), CB_MO * LANE // ML_V_W)),
        pl.BlockSpec((None, 2 * ML_HEADS, tm), lambda i: (cur(i) // tps, 0, cur(i) % tps)),
        _resident((2 * ML_HEADS, LANE)),
        _resident((CONV_W, 2 * ML_QK_W)), _resident((1, 2 * ML_QK_W)), _resident((1, ML_V_W)),
    ]
    scratch = [pltpu.VMEM((tm, ML_V_W), BF16),
               pltpu.VMEM((tm + XS_HDR, ML_QK_W), F32), pltpu.VMEM((tm + XS_HDR, ML_QK_W), F32),
               pltpu.VMEM((ML_PAIR, tm, ML_QK_W), BF16), pltpu.VMEM((tm, ML_QK_W), BF16),
               pltpu.VMEM((tm, ML_QK_W), F32),
               pltpu.VMEM((ML_NPAIR, LANE, 2 * LANE), F32), pltpu.VMEM((ML_HEADS, LANE), F32),
               pltpu.VMEM((5, ML_NR, ML_L), F32)]
    return pl.pallas_call(
        functools.partial(_tail_kernel, n_tiles=n_tiles, tiles_per_seq=tps),
        grid=(n_tiles + 1,),
        in_specs=in_specs,
        out_specs=pl.BlockSpec((tm, D_MODEL), lambda i: (prv(i), 0)),
        out_shape=jax.ShapeDtypeStruct((T, D_MODEL), F32),
        scratch_shapes=scratch,
        compiler_params=pltpu.CompilerParams(
            dimension_semantics=("arbitrary",), vmem_limit_bytes=VMEM_LIMIT),
        name="tail",
    )(x2, ada3, att2, proj2, proj2, g2, wa, wm, wo, w1, w2,
      proj2, proj2, proj2, proj2, gates_t, bif_b, conv_w, conv_b, norm_g)


def kernel(x, c, w_ada, b_ada, norm1_g, norm2_g, w_in, b_if, conv_w, conv_b, q_norm_g, k_norm_g,
           rel_bias, mlstm_norm_g, w_att_out, w_ml_out, w_out, w_ff1, w_ff2):
    B, S, D = x.shape
    T = B * S
    depth = w_ada.shape[0]
    bias_tiles = _bias_tiles(rel_bias)
    x2 = x.reshape(T, D)
    for l in range(depth):
        ada3 = _ada(c, w_ada[l], b_ada[l]).reshape(B, 6, D)
        w_in_b = w_in[l].astype(BF16)
        u2d, qkv = _inproj_att(x2, ada3, norm1_g[l].reshape(1, D), w_in_b,
                               q_norm_g[l].reshape(1, ATT_DH), k_norm_g[l].reshape(1, ATT_DH), S)
        proj2, gates_t, att = _mix(u2d, w_in_b[:, WIN_G[0]:WIN_G[1]], w_in_b, w_in_b[:, WIN_IF[0]:WIN_IF[1]].T,
                                   qkv.reshape(B, S, AW), bias_tiles, B, S)
        bif_b = jnp.broadcast_to(b_if[l].reshape(2 * ML_HEADS, 1), (2 * ML_HEADS, LANE))
        x2 = _tail(x2, ada3, att.reshape(T, ATT_W), proj2, gates_t, bif_b, conv_w[l], conv_b[l].reshape(1, -1),
                   mlstm_norm_g[l].reshape(1, ML_V_W), norm2_g[l].reshape(1, D),
                   w_att_out[l].astype(BF16), w_ml_out[l].astype(BF16), w_out[l].astype(BF16),
                   w_ff1[l].astype(BF16), w_ff2[l].astype(BF16), S)
    return x2.reshape(B, S, D)
```

```python
import functools
import math

import numpy as np
import jax
import jax.numpy as jnp
from jax import lax
from jax.experimental import pallas as pl
from jax.experimental.pallas import tpu as pltpu

F32 = jnp.float32
BF16 = jnp.bfloat16

D_MODEL = 1024
ATT_GROUPS = ((128, 1), (512, 4), (2048, 16))
N_ATT_GROUPS = 3
ATT_HPG = 4
ATT_DH = 128
N_ATT_HEADS = 12
ATT_W = 512
ATT_BLK = 128
ML_HEADS = 8
ML_DK = 64
ML_DV = 128
ML_QK_W = 512
ML_V_W = 1024
CONV_W = 4
D_FF = 4096
N_BUCKETS = 32
MAX_DISTANCE = 2048
EPS = 1e-6
NEG = -1e30

LANE = 128
ML_L = 128

AW = 3 * N_ATT_HEADS * ATT_DH
PW = 5120
CB_G, CB_MV, CB_MO = 0, 16, 24
CB_MQ, CB_MK = 32, 36
WIN_ATT, WIN_MQK, WIN_MVO, WIN_IF, WIN_G = (0, 4608), (4608, 5632), (5632, 7680), (7680, 7696), (7696, 9744)

VMEM_LIMIT = 56 * 1024 * 1024


def _dot(a, b):
    return jnp.dot(a, b, preferred_element_type=F32)


def _dot_nt(a, b):
    return lax.dot_general(a, b, (((1,), (1,)), ((), ())), preferred_element_type=F32)


def _split3(a):
    hi = a.astype(BF16)
    r1 = a - hi.astype(F32)
    mid = r1.astype(BF16)
    lo = (r1 - mid.astype(F32)).astype(BF16)
    return hi, mid, lo


def _dot3(a, rhs_bf16):
    hi, mid, lo = _split3(a)
    return _dot(hi, rhs_bf16) + _dot(mid, rhs_bf16) + _dot(lo, rhs_bf16)


def _resident(shape):
    return pl.BlockSpec(shape, lambda *_: (0,) * len(shape), pipeline_mode=pl.Buffered(1))


def _ada_kernel(c_ref, w_ref, b_ref, o_ref):
    c = c_ref[...]
    s = c * jax.nn.sigmoid(c)
    o_ref[...] = _dot(s.astype(BF16), w_ref[...].astype(BF16)) + b_ref[...]


def _ada(c, w, b):
    B = c.shape[0]
    n = w.shape[1]
    tn = 1024
    return pl.pallas_call(
        _ada_kernel,
        grid=(n // tn,),
        in_specs=[pl.BlockSpec((B, D_MODEL), lambda j: (0, 0)),
                  pl.BlockSpec((D_MODEL, tn), lambda j: (0, j)),
                  pl.BlockSpec((1, tn), lambda j: (0, j))],
        out_specs=pl.BlockSpec((B, tn), lambda j: (0, j)),
        out_shape=jax.ShapeDtypeStruct((B, n), F32),
        name="ada",
    )(c, w, b.reshape(1, n))


def _t5_bucket_np(dist):
    max_exact = N_BUCKETS // 2
    d = np.maximum(dist, max_exact).astype(np.float32)
    large = max_exact + (np.log(d / np.float32(max_exact)) / np.float32(math.log(MAX_DISTANCE / max_exact))
                         * np.float32(N_BUCKETS - max_exact)).astype(np.int32)
    large = np.minimum(large, N_BUCKETS - 1)
    return np.where(dist < max_exact, dist, large).astype(np.int32)


def _bucket_tiles():
    i = np.arange(ATT_BLK)[:, None]
    j = np.arange(2 * ATT_BLK)[None, :]
    delta = ATT_BLK + i - j
    return np.stack([_t5_bucket_np(np.maximum(delta, 0) * dil) for _, dil in ATT_GROUPS])


def _bias_kernel(tab_ref, bucket_ref, o_ref):
    hh = pl.program_id(0)
    bucket = bucket_ref[...]
    acc = jnp.zeros(bucket.shape, F32)
    for k in range(N_BUCKETS):
        acc = jnp.where(bucket == k, tab_ref[k, hh], acc)
    i = lax.broadcasted_iota(jnp.int32, bucket.shape, 0)
    j = lax.broadcasted_iota(jnp.int32, bucket.shape, 1)
    delta = ATT_BLK + i - j
    valid = (delta >= 0) & (delta <= ATT_BLK)
    o_ref[0] = jnp.where(valid, acc, NEG)
    o_ref[1] = jnp.where(valid & (j >= ATT_BLK), acc, NEG)


def _bias_tiles(rel_bias):
    buckets = jnp.asarray(_bucket_tiles())
    return pl.pallas_call(
        _bias_kernel,
        grid=(N_ATT_HEADS,),
        in_specs=[pl.BlockSpec(memory_space=pltpu.SMEM),
                  pl.BlockSpec((None, ATT_BLK, 2 * ATT_BLK), lambda h: (h // ATT_HPG, 0, 0))],
        out_specs=pl.BlockSpec((None, 2, ATT_BLK, 2 * ATT_BLK), lambda h: (h, 0, 0, 0)),
        out_shape=jax.ShapeDtypeStruct((N_ATT_HEADS, 2, ATT_BLK, 2 * ATT_BLK), F32),
        name="bias_tiles",
    )(rel_bias, buckets)


IP_TM = 512
IP_TN = 512
IP_NQ = N_ATT_HEADS * ATT_DH // IP_TN


def _inproj_att_kernel(x_ref, ada_ref, g1_ref, w_ref, qg_ref, kg_ref, u_ref, o_ref):
    x = x_ref[...]
    ms = jnp.mean(x * x, axis=-1, keepdims=True)
    y = x * lax.rsqrt(ms + EPS) * g1_ref[...]
    ub = (y * (1.0 + ada_ref[1:2, :]) + ada_ref[0:1, :]).astype(BF16)
    u_ref[...] = ub
    for j in range(AW // IP_TN):
        acc = _dot(ub, w_ref[:, j * IP_TN:(j + 1) * IP_TN])
        if j < 2 * IP_NQ:
            gain = qg_ref[...] if j < IP_NQ else kg_ref[...]
            for k in range(IP_TN // ATT_DH):
                a = acc[:, k * ATT_DH:(k + 1) * ATT_DH]
                ms = jnp.mean(a * a, axis=-1, keepdims=True)
                lo = j * IP_TN + k * ATT_DH
                o_ref[:, lo:lo + ATT_DH] = (a * lax.rsqrt(ms + EPS) * gain).astype(o_ref.dtype)
        else:
            o_ref[:, j * IP_TN:(j + 1) * IP_TN] = acc.astype(o_ref.dtype)


def _inproj_att(x2, ada3, g1, w_att, qg, kg, S):
    T = x2.shape[0]
    tm = IP_TM
    tiles_per_seq = S // tm
    return pl.pallas_call(
        _inproj_att_kernel,
        grid=(T // tm,),
        in_specs=[pl.BlockSpec((tm, D_MODEL), lambda i: (i, 0)),
                  pl.BlockSpec((None, 6, D_MODEL), lambda i: (i // tiles_per_seq, 0, 0)),
                  _resident((1, D_MODEL)),
                  _resident((D_MODEL, AW)),
                  _resident((1, ATT_DH)),
                  _resident((1, ATT_DH))],
        out_specs=[pl.BlockSpec((tm, D_MODEL), lambda i: (i, 0)),
                   pl.BlockSpec((tm, AW), lambda i: (i, 0))],
        out_shape=[jax.ShapeDtypeStruct((T, D_MODEL), BF16),
                   jax.ShapeDtypeStruct((T, AW), BF16)],
        compiler_params=pltpu.CompilerParams(
            dimension_semantics=("parallel",), vmem_limit_bytes=VMEM_LIMIT),
        name="inproj_att",
    )(x2, ada3, g1, w_att, qg, kg)


ATT_UNROLL = 8


def _mix_kernel(u_ref, wg_ref, *rest, S):
    n_wb = len(MIX_WBLOCKS)
    _mix_body(u_ref, wg_ref, rest[:n_wb], *rest[n_wb:], S=S)


def _mix_body(u_ref, wg_ref, wb_refs, wif_ref,
                q0_ref, k0_ref, v0_ref, q1_ref, k1_ref, v1_ref, q2_ref, k2_ref, v2_ref, bias_ref,
                p_ref, gt_ref, o_ref,
                q1f, k1f, v1f, q2f, k2f, v2f, o_scr, l_scr, *, S):
    scale = ATT_DH ** -0.5
    blk = ATT_BLK

    ub = u_ref[...]

    n_g = (WIN_G[1] - WIN_G[0]) // IP_TN

    def proj_chunk(j):
        w = wg_ref[:, j * IP_TN:(j + 1) * IP_TN] if j < n_g else wb_refs[j - n_g][...]
        p_ref[:, j * IP_TN:(j + 1) * IP_TN] = _dot(ub, w).astype(p_ref.dtype)
    pending = [functools.partial(proj_chunk, j) for j in range(PW // IP_TN)]

    def emit_proj(n):
        for _ in range(min(n, len(pending))):
            pending.pop(0)()

    plan = [(1, 0), (1, 1), (1, 0), (1, 1), (1, 1), (1, 1)]

    gt_ref[...] = _dot_nt(wif_ref[...], ub)
    emit_proj(1)

    pad1 = blk * ATT_GROUPS[1][1]
    k1f[0:pad1, :] = jnp.zeros((pad1, LANE), F32)
    v1f[0:pad1, :] = jnp.zeros((pad1, LANE), F32)
    k1f[pad1:pad1 + S, :] = k1_ref[...].astype(F32)
    v1f[pad1:pad1 + S, :] = v1_ref[...].astype(F32)
    q1f[...] = q1_ref[...].astype(F32)
    q2f[...] = q2_ref[...].astype(F32)
    k2f[...] = k2_ref[...].astype(F32)
    v2f[...] = v2_ref[...].astype(F32)

    def softmax_blocks(ops):
        n_mid, n_end = plan.pop(0)
        ss = [_dot_nt(q, kk) * scale + bias for (q, kk, _, bias) in ops]
        emit_proj(n_mid)
        ms = [jnp.max(s, axis=-1, keepdims=True) for s in ss]
        ps = [jnp.exp(s - m) for s, m in zip(ss, ms)]
        ls = [jnp.sum(p, axis=-1, keepdims=True) for p in ps]
        accs = [_dot(p.astype(BF16), op[2]) for p, op in zip(ps, ops)]
        emit_proj(n_end)
        return [(acc / l, m + jnp.log(l)) for acc, l, m in zip(accs, ls, ms)]

    def run_dilated(g, r, n_batches, fetch):
        for it in range(n_batches):
            fetched = [fetch(it, u) for u in range(ATT_UNROLL)]
            res = softmax_blocks([f[:4] for f in fetched])
            for (o, lse), f in zip(res, fetched):
                rows = pl.ds(f[4], blk, stride=r)
                o_scr[g - 1, rows, :] = o
                l_scr[g - 1, rows, :] = jnp.broadcast_to(lse, (blk, LANE))

    r1 = ATT_GROUPS[1][1]
    nb1 = S // r1 // blk
    assert ATT_UNROLL % nb1 == 0

    def fetch1(it, u):
        rho = it * (ATT_UNROLL // nb1) + u // nb1
        n = u % nb1
        start = rho + r1 * blk * n
        return (q1f[pl.ds(start, blk, stride=r1), :].astype(BF16),
                k1f[pl.ds(start, 2 * blk, stride=r1), :].astype(BF16),
                v1f[pl.ds(start, 2 * blk, stride=r1), :].astype(BF16),
                bias_ref[1, 1 if n == 0 else 0], start)
    run_dilated(1, r1, r1 * nb1 // ATT_UNROLL, fetch1)

    r2 = ATT_GROUPS[2][1]
    assert S // r2 == blk

    def fetch2(it, u):
        rho = it * ATT_UNROLL + u
        return (q2f[pl.ds(rho, blk, stride=r2), :].astype(BF16),
                k2f[pl.ds(rho, blk, stride=r2), :].astype(BF16),
                v2f[pl.ds(rho, blk, stride=r2), :].astype(BF16),
                bias_ref[2, 1][:, blk:2 * blk], rho)
    run_dilated(2, r2, r2 // ATT_UNROLL, fetch2)

    for it in range(S // blk // ATT_UNROLL):
        ops = []
        for u in range(ATT_UNROLL):
            n = it * ATT_UNROLL + u
            q = q0_ref[n * blk:(n + 1) * blk, :]
            if n == 0:
                ops.append((q, k0_ref[0:blk, :], v0_ref[0:blk, :], bias_ref[0, 1][:, blk:2 * blk]))
            else:
                ops.append((q, k0_ref[(n - 1) * blk:(n + 1) * blk, :], v0_ref[(n - 1) * blk:(n + 1) * blk, :],
                            bias_ref[0, 0]))
        for u, (o0, lse0) in enumerate(softmax_blocks(ops)):
            n = it * ATT_UNROLL + u
            rows = slice(n * blk, (n + 1) * blk)
            l1, l2 = l_scr[0, rows, :], l_scr[1, rows, :]
            mx = jnp.maximum(jnp.maximum(lse0, l1), l2)
            e0, e1, e2 = jnp.exp(lse0 - mx), jnp.exp(l1 - mx), jnp.exp(l2 - mx)
            att = (e0 * o0 + e1 * o_scr[0, rows, :] + e2 * o_scr[1, rows, :]) / (e0 + e1 + e2)
            o_ref[rows, :] = att.astype(o_ref.dtype)
    emit_proj(len(pending))


MIX_WBLOCKS = tuple(range(WIN_MVO[0] // IP_TN, WIN_MVO[1] // IP_TN)) + tuple(
    range(WIN_MQK[0] // IP_TN, WIN_MQK[1] // IP_TN))


def _mix(u2d, w_gates, w_in_b, w_if_t, qkv3, bias_tiles, B, S):
    T = u2d.shape[0]
    tm = IP_TM
    tps = S // tm
    assert tps == ATT_HPG
    hpp = N_ATT_HEADS
    in_specs = [pl.BlockSpec((tm, D_MODEL), lambda b, j: (b * tps + j, 0)),
                _resident((D_MODEL, WIN_G[1] - WIN_G[0]))]
    in_specs += [pl.BlockSpec((D_MODEL, IP_TN), lambda b, j, cb=cb: (0, cb), pipeline_mode=pl.Buffered(1))
                 for cb in MIX_WBLOCKS]
    in_specs.append(_resident((16, D_MODEL)))
    for g in range(N_ATT_GROUPS):
        for part in range(3):
            in_specs.append(pl.BlockSpec((None, S, LANE),
                                         lambda b, j, cb=part * hpp + g * ATT_HPG: (b, 0, cb + j)))
    in_specs.append(pl.BlockSpec((N_ATT_GROUPS, None, 2, ATT_BLK, 2 * ATT_BLK), lambda b, j: (0, j, 0, 0, 0)))
    pad1 = ATT_BLK * ATT_GROUPS[1][1]
    scratch = [pltpu.VMEM((S, LANE), F32), pltpu.VMEM((pad1 + S, LANE), F32), pltpu.VMEM((pad1 + S, LANE), F32),
               pltpu.VMEM((S, LANE), F32), pltpu.VMEM((S, LANE), F32), pltpu.VMEM((S, LANE), F32),
               pltpu.VMEM((N_ATT_GROUPS - 1, S, LANE), F32), pltpu.VMEM((N_ATT_GROUPS - 1, S, LANE), F32)]
    bias5 = bias_tiles.reshape(N_ATT_GROUPS, ATT_HPG, 2, ATT_BLK, 2 * ATT_BLK)
    return pl.pallas_call(
        functools.partial(_mix_kernel, S=S),
        grid=(B, tps),
        in_specs=in_specs,
        out_specs=[pl.BlockSpec((tm, PW), lambda b, j: (b * tps + j, 0)),
                   pl.BlockSpec((None, 16, tm), lambda b, j: (b, 0, j)),
                   pl.BlockSpec((None, S, LANE), lambda b, j: (b, 0, j))],
        out_shape=[jax.ShapeDtypeStruct((T, PW), BF16),
                   jax.ShapeDtypeStruct((B, 16, S), F32),
                   jax.ShapeDtypeStruct((B, S, ATT_W), BF16)],
        scratch_shapes=scratch,
        compiler_params=pltpu.CompilerParams(
            dimension_semantics=("parallel", "parallel"), vmem_limit_bytes=VMEM_LIMIT),
        name="mix",
    )(u2d, w_gates, *([w_in_b] * len(MIX_WBLOCKS)), w_if_t, *([qkv3] * 9), bias5)


TL_TM = 512
TL_TK = 512
ML_PAIR = 2
ML_NPAIR = ML_HEADS // ML_PAIR
ML_CPT = TL_TM // ML_L
ML_NR = ML_CPT * ML_HEADS
XS_HDR = 8


def _tail_kernel(x_ref, ada_ref, att_ref, ga_ref, gm_ref, g2_ref, wa_ref, wm_ref, wo_ref, w1_ref, w2_ref,
                 mq_ref, mk_ref, mv_ref, mo_ref, gt_ref, bif_ref, cw_ref, cb_ref, ng_ref,
                 o_ref,
                 hml_scr, xq, xk, qa, kab, kaf, cst_scr, mch_scr, rf, *, n_tiles, tiles_per_seq):
    i = pl.program_id(0)
    L = ML_L
    tm = TL_TM
    im = jnp.minimum(i, n_tiles - 1)

    @pl.when(i == 0)
    def _():
        hml_scr[...] = jnp.zeros(hml_scr.shape, hml_scr.dtype)

    @pl.when(im % tiles_per_seq == 0)
    def _():
        cst_scr[...] = jnp.zeros(cst_scr.shape, F32)
        mch_scr[...] = jnp.zeros(mch_scr.shape, F32)
        xq[tm:tm + XS_HDR, :] = jnp.zeros((XS_HDR, ML_QK_W), F32)
        xk[tm:tm + XS_HDR, :] = jnp.zeros((XS_HDR, ML_QK_W), F32)


    def merge_stage():
        y_att = _dot(att_ref[...], wa_ref[...])
        y_ml = _dot(hml_scr[...], wm_ref[...])
        ga = jax.nn.sigmoid(ga_ref[...].astype(F32))
        gm = jax.nn.sigmoid(gm_ref[...].astype(F32))
        return (ga * y_att + gm * y_ml).astype(BF16)

    def out_proj_stage(mix):
        x1 = x_ref[...] + ada_ref[2:3, :] * _dot(mix, wo_ref[...])
        ms = jnp.mean(x1 * x1, axis=-1, keepdims=True)
        y = x1 * lax.rsqrt(ms + EPS) * g2_ref[...]
        return x1, (y * (1.0 + ada_ref[4:5, :]) + ada_ref[3:4, :]).astype(BF16)

    def mlp_up(u2, k):
        hdn = jnp.maximum(_dot(u2, w1_ref[:, k * TL_TK:(k + 1) * TL_TK]), 0.0)
        return (hdn * hdn).astype(BF16)

    def mlp_down(hsq, acc, k):
        part = _dot(hsq, w2_ref[k * TL_TK:(k + 1) * TL_TK, :])
        return part if acc is None else acc + part

    mix = merge_stage()

    lane = lax.broadcasted_iota(jnp.int32, (1, LANE), 1)
    srow = lax.broadcasted_iota(jnp.int32, (LANE, 1), 0)
    in_head_lane = [(lane >= hh * ML_DK) & (lane < (hh + 1) * ML_DK) for hh in range(ML_PAIR)]
    in_head_row = [(srow >= hh * ML_DK) & (srow < (hh + 1) * ML_DK) for hh in range(ML_PAIR)]
    head0_rows = in_head_row[0]

    def conv_silu(src_ref, xs, c0):
        xs[0:XS_HDR, :] = xs[tm:tm + XS_HDR, :]
        xs[XS_HDR:XS_HDR + tm, :] = src_ref[...].astype(F32)
        yv = cb_ref[:, c0:c0 + ML_QK_W]
        for jj in range(CONV_W):
            yv = yv + cw_ref[jj:jj + 1, c0:c0 + ML_QK_W] * xs[pl.ds(XS_HDR - (CONV_W - 1) + jj, tm), :]
        return yv * jax.nn.sigmoid(yv)

    yq = conv_silu(mq_ref, xq, 0)
    for p in range(ML_NPAIR):
        for hh in range(ML_PAIR):
            qa[hh, :, p * LANE:(p + 1) * LANE] = jnp.where(in_head_lane[hh], yq[:, p * LANE:(p + 1) * LANE],
                                                           0.0).astype(BF16)
    yk = conv_silu(mk_ref, xk, ML_QK_W) * (ML_DK ** -0.5)
    kaf[...] = yk
    kab[...] = yk.astype(BF16)

    x1, u2 = out_proj_stage(mix)

    r_i = lax.broadcasted_iota(jnp.int32, (L, L), 0)
    c_i = lax.broadcasted_iota(jnp.int32, (L, L), 1)
    ones_m = jnp.ones((L, L), BF16)
    incl_upper = (r_i <= c_i).astype(BF16)
    causal = c_i <= r_i
    lane_nr = lax.broadcasted_iota(jnp.int32, (ML_NR, L), 1)
    pad_rows = jnp.zeros((L - ML_NR, L), F32)

    def col_form(row_form):
        return jnp.concatenate([row_form, pad_rows], axis=0).T

    li = jnp.concatenate([gt_ref[0:ML_HEADS, c * L:(c + 1) * L] + bif_ref[0:ML_HEADS, :]
                          for c in range(ML_CPT)], axis=0)
    zf = jnp.concatenate([gt_ref[ML_HEADS:2 * ML_HEADS, c * L:(c + 1) * L] + bif_ref[ML_HEADS:2 * ML_HEADS, :]
                          for c in range(ML_CPT)], axis=0)
    lf = jnp.minimum(zf, 0.0) - jnp.log1p(jnp.exp(-jnp.abs(zf)))
    brow = _dot3(lf, incl_upper)
    bend = _dot3(lf, ones_m)
    u = brow - li
    g = bend - u
    maxg = jnp.max(g, axis=-1, keepdims=True)
    m = mch_scr[...]
    for c in range(ML_CPT):
        rs = slice(c * ML_HEADS, (c + 1) * ML_HEADS)
        rf[1, rs, :] = m
        m = jnp.maximum(bend[rs, :] + m, maxg[rs, :])
        rf[2, rs, :] = m
    mch_scr[...] = m
    m_cur = rf[1]
    m_nxt = rf[2]
    rf[0] = u
    rf[3] = jnp.exp(bend + m_cur - m_nxt)
    rf[4] = jnp.exp(g - m_nxt)
    pm = -u
    sh = 1
    while sh < L:
        pm = jnp.maximum(pm, jnp.where(lane_nr >= sh, pltpu.roll(pm, sh, axis=1), NEG))
        sh *= 2
    d1 = -jnp.maximum(m_cur, pm)
    dcol = col_form(d1)
    ecol = col_form(jnp.exp(d1 - brow))

    ones_v = jnp.ones((L, LANE), BF16)

    def prepare(c, p):
        rows = slice(c * L, (c + 1) * L)
        pl_ = slice(p * LANE, (p + 1) * LANE)
        r0 = c * ML_HEADS + p * ML_PAIR
        kc = kab[rows, pl_]
        kt = kaf[rows, pl_].T
        wk = jnp.where(head0_rows, rf[4, r0:r0 + 1, :], rf[4, r0 + 1:r0 + 2, :])
        ktw = (kt * wk).astype(BF16)
        ktw2 = jnp.concatenate([jnp.where(in_head_row[hh], ktw, jnp.zeros_like(ktw)) for hh in range(ML_PAIR)],
                               axis=1)
        vexts = [jnp.concatenate([mv_ref[rows, (r0 % ML_HEADS + hh) * LANE:(r0 % ML_HEADS + hh + 1) * LANE],
                                  ones_v], axis=1) for hh in range(ML_PAIR)]
        z = _dot(ktw2, jnp.concatenate(vexts, axis=0))
        ss = [_dot_nt(qa[hh, rows, pl_], kc) for hh in range(ML_PAIR)]
        drow = [jnp.broadcast_to(dcol[:, r0 + hh:r0 + hh + 1], (L, L)) for hh in range(ML_PAIR)]
        erow = [jnp.broadcast_to(ecol[:, r0 + hh:r0 + hh + 1], (L, L)) for hh in range(ML_PAIR)]
        return ss, z, vexts, drow, erow

    order = [(c, p) for c in range(ML_CPT) for p in range(ML_NPAIR)]
    assert 2 * (D_FF // TL_TK) == len(order)
    cst = [cst_scr[p] for p in range(ML_NPAIR)]
    nxt = prepare(*order[0])
    acc = None
    hsq = None
    for n, (c, p) in enumerate(order):
        rows = slice(c * L, (c + 1) * L)
        pl_ = slice(p * LANE, (p + 1) * LANE)
        r0 = c * ML_HEADS + p * ML_PAIR
        ss, z, vexts, drow, erow = nxt
        if n + 1 < len(order):
            nxt = prepare(*order[n + 1])
        cst_b = cst[p].astype(BF16)
        lhs = []
        for hh in range(ML_PAIR):
            r = r0 + hh
            w = ss[hh] * jnp.exp(jnp.where(causal, drow[hh] - rf[0, r:r + 1, :], NEG))
            qi = qa[hh, rows, pl_].astype(F32) * jnp.exp(drow[hh] + rf[1, r:r + 1, :])
            lhs.append(jnp.concatenate([w.astype(BF16), qi.astype(BF16)], axis=1))
        if n % 2 == 0:
            hsq = mlp_up(u2, n // 2)
        else:
            acc = mlp_down(hsq, acc, n // 2)
        for hh in range(ML_PAIR):
            cols = slice((p * ML_PAIR + hh) * LANE, (p * ML_PAIR + hh + 1) * LANE)
            tot = _dot(lhs[hh], jnp.concatenate([vexts[hh], cst_b], axis=0))
            hval = tot[:, 0:LANE] / jnp.maximum(jnp.abs(tot[:, LANE:2 * LANE]), erow[hh])
            msq = jnp.mean(hval * hval, axis=-1, keepdims=True)
            hn = hval * lax.rsqrt(msq + EPS) * ng_ref[:, cols]
            hml_scr[rows, cols] = (hn * jax.nn.sigmoid(mo_ref[rows, cols].astype(F32))).astype(hml_scr.dtype)
        dec = jnp.where(head0_rows, rf[3, r0:r0 + 1, :], rf[3, r0 + 1:r0 + 2, :])
        cst[p] = jnp.concatenate([dec, dec], axis=1) * cst[p] + z
    for p in range(ML_NPAIR):
        cst_scr[p] = cst[p]
    o_ref[...] = x1 + ada_ref[5:6, :] * acc


def _tail(x2, ada3, att2, proj2, gates_t, bif_b, conv_w, conv_b, norm_g, g2, wa, wm, wo, w1, w2, S):
    T = x2.shape[0]
    tm = TL_TM
    tps = S // tm
    n_tiles = T // tm
    cur = lambda i: jnp.minimum(i, n_tiles - 1)
    prv = lambda i: jnp.maximum(i - 1, 0)
    gcb = CB_G * LANE // D_MODEL
    in_specs = [
        pl.BlockSpec((tm, D_MODEL), lambda i: (prv(i), 0)),
        pl.BlockSpec((None, 6, D_MODEL), lambda i: (prv(i) // tps, 0, 0)),
        pl.BlockSpec((tm, ATT_W), lambda i: (prv(i), 0)),
        pl.BlockSpec((tm, D_MODEL), lambda i: (prv(i), gcb)),
        pl.BlockSpec((tm, D_MODEL), lambda i: (prv(i), gcb + 1)),
        _resident((1, D_MODEL)),
        _resident((ATT_W, D_MODEL)), _resident((ML_V_W, D_MODEL)), _resident((D_MODEL, D_MODEL)),
        _resident((D_MODEL, D_FF)), _resident((D_FF, D_MODEL)),
        pl.BlockSpec((tm, ML_QK_W), lambda i: (cur(i), CB_MQ * LANE // ML_QK_W)),
        pl.BlockSpec((tm, ML_QK_W), lambda i: (cur(i), CB_MK * LANE // ML_QK_W)),
        pl.BlockSpec((tm, ML_V_W), lambda i: (cur(i), CB_MV * LANE // ML_V_W)),
        pl.BlockSpec((tm, ML_V_W), lambda i: (cur(i), CB_MO * LANE // ML_V_W)),
        pl.BlockSpec((None, 2 * ML_HEADS, tm), lambda i: (cur(i) // tps, 0, cur(i) % tps)),
        _resident((2 * ML_HEADS, LANE)),
        _resident((CONV_W, 2 * ML_QK_W)), _resident((1, 2 * ML_QK_W)), _resident((1, ML_V_W)),
    ]
    scratch = [pltpu.VMEM((tm, ML_V_W), BF16),
               pltpu.VMEM((tm + XS_HDR, ML_QK_W), F32), pltpu.VMEM((tm + XS_HDR, ML_QK_W), F32),
               pltpu.VMEM((ML_PAIR, tm, ML_QK_W), BF16), pltpu.VMEM((tm, ML_QK_W), BF16),
               pltpu.VMEM((tm, ML_QK_W), F32),
               pltpu.VMEM((ML_NPAIR, LANE, 2 * LANE), F32), pltpu.VMEM((ML_HEADS, LANE), F32),
               pltpu.VMEM((5, ML_NR, ML_L), F32)]
    return pl.pallas_call(
        functools.partial(_tail_kernel, n_tiles=n_tiles, tiles_per_seq=tps),
        grid=(n_tiles + 1,),
        in_specs=in_specs,
        out_specs=pl.BlockSpec((tm, D_MODEL), lambda i: (prv(i), 0)),
        out_shape=jax.ShapeDtypeStruct((T, D_MODEL), F32),
        scratch_shapes=scratch,
        compiler_params=pltpu.CompilerParams(
            dimension_semantics=("arbitrary",), vmem_limit_bytes=VMEM_LIMIT),
        name="tail",
    )(x2, ada3, att2, proj2, proj2, g2, wa, wm, wo, w1, w2,
      proj2, proj2, proj2, proj2, gates_t, bif_b, conv_w, conv_b, norm_g)


def kernel(x, c, w_ada, b_ada, norm1_g, norm2_g, w_in, b_if, conv_w, conv_b, q_norm_g, k_norm_g,
           rel_bias, mlstm_norm_g, w_att_out, w_ml_out, w_out, w_ff1, w_ff2):
    B, S, D = x.shape
    T = B * S
    depth = w_ada.shape[0]
    bias_tiles = _bias_tiles(rel_bias)
    x2 = x.reshape(T, D)
    for l in range(depth):
        ada3 = _ada(c, w_ada[l], b_ada[l]).reshape(B, 6, D)
        w_in_b = w_in[l][:, :WIN_IF[0]].astype(BF16)
        w_gates = w_in[l][:, WIN_G[0]:WIN_G[1]].astype(BF16)
        w_if_t = w_in[l][:, WIN_IF[0]:WIN_IF[1]].T.astype(BF16)
        u2d, qkv = _inproj_att(x2, ada3, norm1_g[l].reshape(1, D), w_in_b,
                               q_norm_g[l].reshape(1, ATT_DH), k_norm_g[l].reshape(1, ATT_DH), S)
        proj2, gates_t, att = _mix(u2d, w_gates, w_in_b, w_if_t, qkv.reshape(B, S, AW), bias_tiles, B, S)
        bif_b = jnp.broadcast_to(b_if[l].reshape(2 * ML_HEADS, 1), (2 * ML_HEADS, LANE))
        x2 = _tail(x2, ada3, att.reshape(T, ATT_W), proj2, gates_t, bif_b, conv_w[l], conv_b[l].reshape(1, -1),
                   mlstm_norm_g[l].reshape(1, ML_V_W), norm2_g[l].reshape(1, D),
                   w_att_out[l].astype(BF16), w_ml_out[l].astype(BF16), w_out[l].astype(BF16),
                   w_ff1[l].astype(BF16), w_ff2[l].astype(BF16), S)
    return x2.reshape(B, S, D)
```

```python
import functools
import math

import numpy as np
import jax
import jax.numpy as jnp
from jax import lax
from jax.experimental import pallas as pl
from jax.experimental.pallas import tpu as pltpu

F32 = jnp.float32
BF16 = jnp.bfloat16

D_MODEL = 1024
ATT_GROUPS = ((128, 1), (512, 4), (2048, 16))
N_ATT_GROUPS = 3
ATT_HPG = 4
ATT_DH = 128
N_ATT_HEADS = 12
ATT_W = 512
ATT_BLK = 128
ML_HEADS = 8
ML_DK = 64
ML_DV = 128
ML_QK_W = 512
ML_V_W = 1024
CONV_W = 4
D_FF = 4096
N_BUCKETS = 32
MAX_DISTANCE = 2048
EPS = 1e-6
NEG = -1e30

LANE = 128
ML_L = 128

AW = 3 * N_ATT_HEADS * ATT_DH
PW = 5120
CB_G, CB_MV, CB_MO = 0, 16, 24
CB_MQ, CB_MK = 32, 36
WIN_ATT, WIN_MQK, WIN_MVO, WIN_IF, WIN_G = (0, 4608), (4608, 5632), (5632, 7680), (7680, 7696), (7696, 9744)

VMEM_LIMIT = 56 * 1024 * 1024


def _dot(a, b):
    return jnp.dot(a, b, preferred_element_type=F32)


def _dot_nt(a, b):
    return lax.dot_general(a, b, (((1,), (1,)), ((), ())), preferred_element_type=F32)


def _split3(a):
    hi = a.astype(BF16)
    r1 = a - hi.astype(F32)
    mid = r1.astype(BF16)
    lo = (r1 - mid.astype(F32)).astype(BF16)
    return hi, mid, lo


def _dot3(a, rhs_bf16):
    hi, mid, lo = _split3(a)
    return _dot(hi, rhs_bf16) + _dot(mid, rhs_bf16) + _dot(lo, rhs_bf16)


def _resident(shape):
    return pl.BlockSpec(shape, lambda *_: (0,) * len(shape), pipeline_mode=pl.Buffered(1))


def _ada_kernel(c_ref, w_ref, b_ref, o_ref):
    c = c_ref[...]
    s = c * jax.nn.sigmoid(c)
    o_ref[...] = _dot(s.astype(BF16), w_ref[...].astype(BF16)) + b_ref[...]


def _ada(c, w, b):
    B = c.shape[0]
    n = w.shape[1]
    tn = 1024
    return pl.pallas_call(
        _ada_kernel,
        grid=(n // tn,),
        in_specs=[pl.BlockSpec((B, D_MODEL), lambda j: (0, 0)),
                  pl.BlockSpec((D_MODEL, tn), lambda j: (0, j)),
                  pl.BlockSpec((1, tn), lambda j: (0, j))],
        out_specs=pl.BlockSpec((B, tn), lambda j: (0, j)),
        out_shape=jax.ShapeDtypeStruct((B, n), F32),
        name="ada",
    )(c, w, b.reshape(1, n))


def _t5_bucket_np(dist):
    max_exact = N_BUCKETS // 2
    d = np.maximum(dist, max_exact).astype(np.float32)
    large = max_exact + (np.log(d / np.float32(max_exact)) / np.float32(math.log(MAX_DISTANCE / max_exact))
                         * np.float32(N_BUCKETS - max_exact)).astype(np.int32)
    large = np.minimum(large, N_BUCKETS - 1)
    return np.where(dist < max_exact, dist, large).astype(np.int32)


def _bucket_tiles():
    i = np.arange(ATT_BLK)[:, None]
    j = np.arange(2 * ATT_BLK)[None, :]
    delta = ATT_BLK + i - j
    return np.stack([_t5_bucket_np(np.maximum(delta, 0) * dil) for _, dil in ATT_GROUPS])


def _bias_kernel(tab_ref, bucket_ref, o_ref):
    hh = pl.program_id(0)
    bucket = bucket_ref[...]
    acc = jnp.zeros(bucket.shape, F32)
    for k in range(N_BUCKETS):
        acc = jnp.where(bucket == k, tab_ref[k, hh], acc)
    i = lax.broadcasted_iota(jnp.int32, bucket.shape, 0)
    j = lax.broadcasted_iota(jnp.int32, bucket.shape, 1)
    delta = ATT_BLK + i - j
    valid = (delta >= 0) & (delta <= ATT_BLK)
    o_ref[0] = jnp.where(valid, acc, NEG)
    o_ref[1] = jnp.where(valid & (j >= ATT_BLK), acc, NEG)


def _bias_tiles(rel_bias):
    buckets = jnp.asarray(_bucket_tiles())
    return pl.pallas_call(
        _bias_kernel,
        grid=(N_ATT_HEADS,),
        in_specs=[pl.BlockSpec(memory_space=pltpu.SMEM),
                  pl.BlockSpec((None, ATT_BLK, 2 * ATT_BLK), lambda h: (h // ATT_HPG, 0, 0))],
        out_specs=pl.BlockSpec((None, 2, ATT_BLK, 2 * ATT_BLK), lambda h: (h, 0, 0, 0)),
        out_shape=jax.ShapeDtypeStruct((N_ATT_HEADS, 2, ATT_BLK, 2 * ATT_BLK), F32),
        name="bias_tiles",
    )(rel_bias, buckets)


IP_TM = 512
IP_TN = 512
IP_NQ = N_ATT_HEADS * ATT_DH // IP_TN


def _inproj_att_kernel(x_ref, ada_ref, g1_ref, w_ref, qg_ref, kg_ref, u_ref, o_ref):
    x = x_ref[...]
    ms = jnp.mean(x * x, axis=-1, keepdims=True)
    y = x * lax.rsqrt(ms + EPS) * g1_ref[...]
    ub = (y * (1.0 + ada_ref[1:2, :]) + ada_ref[0:1, :]).astype(BF16)
    u_ref[...] = ub
    for j in range(AW // IP_TN):
        acc = _dot(ub, w_ref[:, j * IP_TN:(j + 1) * IP_TN])
        if j < 2 * IP_NQ:
            gain = qg_ref[...] if j < IP_NQ else kg_ref[...]
            for k in range(IP_TN // ATT_DH):
                a = acc[:, k * ATT_DH:(k + 1) * ATT_DH]
                ms = jnp.mean(a * a, axis=-1, keepdims=True)
                lo = j * IP_TN + k * ATT_DH
                o_ref[:, lo:lo + ATT_DH] = (a * lax.rsqrt(ms + EPS) * gain).astype(o_ref.dtype)
        else:
            o_ref[:, j * IP_TN:(j + 1) * IP_TN] = acc.astype(o_ref.dtype)


def _inproj_att(x2, ada3, g1, w_att, qg, kg, S):
    T = x2.shape[0]
    tm = IP_TM
    tiles_per_seq = S // tm
    return pl.pallas_call(
        _inproj_att_kernel,
        grid=(T // tm,),
        in_specs=[pl.BlockSpec((tm, D_MODEL), lambda i: (i, 0)),
                  pl.BlockSpec((None, 6, D_MODEL), lambda i: (i // tiles_per_seq, 0, 0)),
                  _resident((1, D_MODEL)),
                  _resident((D_MODEL, AW)),
                  _resident((1, ATT_DH)),
                  _resident((1, ATT_DH))],
        out_specs=[pl.BlockSpec((tm, D_MODEL), lambda i: (i, 0)),
                   pl.BlockSpec((tm, AW), lambda i: (i, 0))],
        out_shape=[jax.ShapeDtypeStruct((T, D_MODEL), BF16),
                   jax.ShapeDtypeStruct((T, AW), BF16)],
        compiler_params=pltpu.CompilerParams(
            dimension_semantics=("parallel",), vmem_limit_bytes=VMEM_LIMIT),
        name="inproj_att",
    )(x2, ada3, g1, w_att, qg, kg)


ATT_UNROLL = 8


def _mix_kernel(u_ref, wg_ref, *rest, S):
    n_wb = len(MIX_WBLOCKS)
    _mix_body(u_ref, wg_ref, rest[:n_wb], *rest[n_wb:], S=S)


def _mix_body(u_ref, wg_ref, wb_refs, wif_ref,
                q0_ref, k0_ref, v0_ref, q1_ref, k1_ref, v1_ref, q2_ref, k2_ref, v2_ref, bias_ref,
                p_ref, gt_ref, o_ref,
                q1f, k1f, v1f, q2f, k2f, v2f, o_scr, l_scr, o2d, l2d, *, S):
    scale = ATT_DH ** -0.5
    blk = ATT_BLK

    ub = u_ref[...]

    n_g = (WIN_G[1] - WIN_G[0]) // IP_TN

    def proj_chunk(j):
        w = wg_ref[:, j * IP_TN:(j + 1) * IP_TN] if j < n_g else wb_refs[j - n_g][...]
        p_ref[:, j * IP_TN:(j + 1) * IP_TN] = _dot(ub, w).astype(p_ref.dtype)
    pending = [functools.partial(proj_chunk, j) for j in range(PW // IP_TN)]

    def emit_proj(n):
        for _ in range(min(n, len(pending))):
            pending.pop(0)()

    plan = [(1, 0), (1, 1), (1, 0), (1, 1), (1, 1), (1, 1)]

    gt_ref[...] = _dot_nt(wif_ref[...], ub)
    emit_proj(1)

    pad1 = blk * ATT_GROUPS[1][1]
    k1f[0:pad1, :] = jnp.zeros((pad1, LANE), F32)
    v1f[0:pad1, :] = jnp.zeros((pad1, LANE), F32)
    k1f[pad1:pad1 + S, :] = k1_ref[...].astype(F32)
    v1f[pad1:pad1 + S, :] = v1_ref[...].astype(F32)
    rs = ATT_GROUPS[1][1]
    for src, dst in ((q2_ref, q2f), (k2_ref, k2f), (v2_ref, v2f)):
        q1f[...] = src[...].astype(F32)
        for a in range(rs):
            dst[a] = q1f[pl.ds(a, S // rs, stride=rs), :]
    q1f[...] = q1_ref[...].astype(F32)

    def softmax_blocks(ops):
        n_mid, n_end = plan.pop(0)
        ss = [_dot_nt(q, kk) * scale + bias for (q, kk, _, bias) in ops]
        emit_proj(n_mid)
        ms = [jnp.max(s, axis=-1, keepdims=True) for s in ss]
        ps = [jnp.exp(s - m) for s, m in zip(ss, ms)]
        ls = [jnp.sum(p, axis=-1, keepdims=True) for p in ps]
        accs = [_dot(p.astype(BF16), op[2]) for p, op in zip(ps, ops)]
        emit_proj(n_end)
        return [(acc / l, m + jnp.log(l)) for acc, l, m in zip(accs, ls, ms)]

    def run_dilated(n_batches, fetch, store):
        for it in range(n_batches):
            fetched = [fetch(it, u) for u in range(ATT_UNROLL)]
            res = softmax_blocks([f[:4] for f in fetched])
            for (o, lse), f in zip(res, fetched):
                store(f[4], o, jnp.broadcast_to(lse, (blk, LANE)))

    r1 = ATT_GROUPS[1][1]
    nb1 = S // r1 // blk
    assert ATT_UNROLL % nb1 == 0

    def fetch1(it, u):
        rho = it * (ATT_UNROLL // nb1) + u // nb1
        n = u % nb1
        start = rho + r1 * blk * n
        return (q1f[pl.ds(start, blk, stride=r1), :].astype(BF16),
                k1f[pl.ds(start, 2 * blk, stride=r1), :].astype(BF16),
                v1f[pl.ds(start, 2 * blk, stride=r1), :].astype(BF16),
                bias_ref[1, 1 if n == 0 else 0], start)

    def store1(start, o, lse):
        rows = pl.ds(start, blk, stride=r1)
        o_scr[0, rows, :] = o
        l_scr[0, rows, :] = lse
    run_dilated(r1 * nb1 // ATT_UNROLL, fetch1, store1)

    r2 = ATT_GROUPS[2][1]
    assert S // r2 == blk

    assert r2 == rs * rs

    def fetch2(it, u):
        rho = it * ATT_UNROLL + u
        rows = pl.ds(rho // rs, blk, stride=rs)
        return (q2f[rho % rs, rows, :].astype(BF16), k2f[rho % rs, rows, :].astype(BF16),
                v2f[rho % rs, rows, :].astype(BF16), bias_ref[2, 1][:, blk:2 * blk], rho)

    def store2(rho, o, lse):
        rows = pl.ds(rho // rs, blk, stride=rs)
        o2d[rho % rs, rows, :] = o
        l2d[rho % rs, rows, :] = lse
    run_dilated(r2 // ATT_UNROLL, fetch2, store2)
    for a in range(rs):
        o_scr[1, pl.ds(a, S // rs, stride=rs), :] = o2d[a]
        l_scr[1, pl.ds(a, S // rs, stride=rs), :] = l2d[a]

    for it in range(S // blk // ATT_UNROLL):
        ops = []
        for u in range(ATT_UNROLL):
            n = it * ATT_UNROLL + u
            q = q0_ref[n * blk:(n + 1) * blk, :]
            if n == 0:
                ops.append((q, k0_ref[0:blk, :], v0_ref[0:blk, :], bias_ref[0, 1][:, blk:2 * blk]))
            else:
                ops.append((q, k0_ref[(n - 1) * blk:(n + 1) * blk, :], v0_ref[(n - 1) * blk:(n + 1) * blk, :],
                            bias_ref[0, 0]))
        for u, (o0, lse0) in enumerate(softmax_blocks(ops)):
            n = it * ATT_UNROLL + u
            rows = slice(n * blk, (n + 1) * blk)
            l1, l2 = l_scr[0, rows, :], l_scr[1, rows, :]
            mx = jnp.maximum(jnp.maximum(lse0, l1), l2)
            e0, e1, e2 = jnp.exp(lse0 - mx), jnp.exp(l1 - mx), jnp.exp(l2 - mx)
            att = (e0 * o0 + e1 * o_scr[0, rows, :] + e2 * o_scr[1, rows, :]) / (e0 + e1 + e2)
            o_ref[rows, :] = att.astype(o_ref.dtype)
    emit_proj(len(pending))


MIX_WBLOCKS = tuple(range(WIN_MVO[0] // IP_TN, WIN_MVO[1] // IP_TN)) + tuple(
    range(WIN_MQK[0] // IP_TN, WIN_MQK[1] // IP_TN))


def _mix(u2d, w_gates, w_in_b, w_if_t, qkv3, bias_tiles, B, S):
    T = u2d.shape[0]
    tm = IP_TM
    tps = S // tm
    assert tps == ATT_HPG
    hpp = N_ATT_HEADS
    in_specs = [pl.BlockSpec((tm, D_MODEL), lambda b, j: (b * tps + j, 0)),
                _resident((D_MODEL, WIN_G[1] - WIN_G[0]))]
    in_specs += [pl.BlockSpec((D_MODEL, IP_TN), lambda b, j, cb=cb: (0, cb), pipeline_mode=pl.Buffered(1))
                 for cb in MIX_WBLOCKS]
    in_specs.append(_resident((16, D_MODEL)))
    for g in range(N_ATT_GROUPS):
        for part in range(3):
            in_specs.append(pl.BlockSpec((None, S, LANE),
                                         lambda b, j, cb=part * hpp + g * ATT_HPG: (b, 0, cb + j)))
    in_specs.append(pl.BlockSpec((N_ATT_GROUPS, None, 2, ATT_BLK, 2 * ATT_BLK), lambda b, j: (0, j, 0, 0, 0)))
    pad1 = ATT_BLK * ATT_GROUPS[1][1]
    slab = pltpu.VMEM((ATT_GROUPS[1][1], S // ATT_GROUPS[1][1], LANE), F32)
    scratch = [pltpu.VMEM((S, LANE), F32), pltpu.VMEM((pad1 + S, LANE), F32), pltpu.VMEM((pad1 + S, LANE), F32),
               slab, slab, slab,
               pltpu.VMEM((N_ATT_GROUPS - 1, S, LANE), F32), pltpu.VMEM((N_ATT_GROUPS - 1, S, LANE), F32),
               slab, slab]
    bias5 = bias_tiles.reshape(N_ATT_GROUPS, ATT_HPG, 2, ATT_BLK, 2 * ATT_BLK)
    return pl.pallas_call(
        functools.partial(_mix_kernel, S=S),
        grid=(B, tps),
        in_specs=in_specs,
        out_specs=[pl.BlockSpec((tm, PW), lambda b, j: (b * tps + j, 0)),
                   pl.BlockSpec((None, 16, tm), lambda b, j: (b, 0, j)),
                   pl.BlockSpec((None, S, LANE), lambda b, j: (b, 0, j))],
        out_shape=[jax.ShapeDtypeStruct((T, PW), BF16),
                   jax.ShapeDtypeStruct((B, 16, S), F32),
                   jax.ShapeDtypeStruct((B, S, ATT_W), BF16)],
        scratch_shapes=scratch,
        compiler_params=pltpu.CompilerParams(
            dimension_semantics=("parallel", "parallel"), vmem_limit_bytes=VMEM_LIMIT),
        name="mix",
    )(u2d, w_gates, *([w_in_b] * len(MIX_WBLOCKS)), w_if_t, *([qkv3] * 9), bias5)


TL_TM = 512
TL_TK = 512
ML_PAIR = 2
ML_NPAIR = ML_HEADS // ML_PAIR
ML_CPT = TL_TM // ML_L
ML_NR = ML_CPT * ML_HEADS
XS_HDR = 8


def _tail_kernel(x_ref, ada_ref, att_ref, ga_ref, gm_ref, g2_ref, wa_ref, wm_ref, wo_ref, w1_ref, w2_ref,
                 mq_ref, mk_ref, mv_ref, mo_ref, gt_ref, bif_ref, cw_ref, cb_ref, ng_ref,
                 o_ref,
                 hml_scr, xq, xk, qa, kab, kaf, cst_scr, mch_scr, rf, *, n_tiles, tiles_per_seq):
    i = pl.program_id(0)
    L = ML_L
    tm = TL_TM
    im = jnp.minimum(i, n_tiles - 1)

    @pl.when(i == 0)
    def _():
        hml_scr[...] = jnp.zeros(hml_scr.shape, hml_scr.dtype)

    @pl.when(im % tiles_per_seq == 0)
    def _():
        cst_scr[...] = jnp.zeros(cst_scr.shape, F32)
        mch_scr[...] = jnp.zeros(mch_scr.shape, F32)
        xq[tm:tm + XS_HDR, :] = jnp.zeros((XS_HDR, ML_QK_W), F32)
        xk[tm:tm + XS_HDR, :] = jnp.zeros((XS_HDR, ML_QK_W), F32)


    def merge_stage():
        y_att = _dot(att_ref[...], wa_ref[...])
        y_ml = _dot(hml_scr[...], wm_ref[...])
        ga = jax.nn.sigmoid(ga_ref[...].astype(F32))
        gm = jax.nn.sigmoid(gm_ref[...].astype(F32))
        return (ga * y_att + gm * y_ml).astype(BF16)

    def out_proj_stage(mix):
        x1 = x_ref[...] + ada_ref[2:3, :] * _dot(mix, wo_ref[...])
        ms = jnp.mean(x1 * x1, axis=-1, keepdims=True)
        y = x1 * lax.rsqrt(ms + EPS) * g2_ref[...]
        return x1, (y * (1.0 + ada_ref[4:5, :]) + ada_ref[3:4, :]).astype(BF16)

    def mlp_up(u2, k):
        hdn = jnp.maximum(_dot(u2, w1_ref[:, k * TL_TK:(k + 1) * TL_TK]), 0.0)
        return (hdn * hdn).astype(BF16)

    def mlp_down(hsq, acc, k):
        part = _dot(hsq, w2_ref[k * TL_TK:(k + 1) * TL_TK, :])
        return part if acc is None else acc + part

    mix = merge_stage()

    lane = lax.broadcasted_iota(jnp.int32, (1, LANE), 1)
    srow = lax.broadcasted_iota(jnp.int32, (LANE, 1), 0)
    in_head_lane = [(lane >= hh * ML_DK) & (lane < (hh + 1) * ML_DK) for hh in range(ML_PAIR)]
    in_head_row = [(srow >= hh * ML_DK) & (srow < (hh + 1) * ML_DK) for hh in range(ML_PAIR)]
    head0_rows = in_head_row[0]

    def conv_silu(src_ref, xs, c0):
        xs[0:XS_HDR, :] = xs[tm:tm + XS_HDR, :]
        xs[XS_HDR:XS_HDR + tm, :] = src_ref[...].astype(F32)
        yv = cb_ref[:, c0:c0 + ML_QK_W]
        for jj in range(CONV_W):
            yv = yv + cw_ref[jj:jj + 1, c0:c0 + ML_QK_W] * xs[pl.ds(XS_HDR - (CONV_W - 1) + jj, tm), :]
        return yv * jax.nn.sigmoid(yv)

    yq = conv_silu(mq_ref, xq, 0)
    for p in range(ML_NPAIR):
        for hh in range(ML_PAIR):
            qa[hh, :, p * LANE:(p + 1) * LANE] = jnp.where(in_head_lane[hh], yq[:, p * LANE:(p + 1) * LANE],
                                                           0.0).astype(BF16)
    yk = conv_silu(mk_ref, xk, ML_QK_W) * (ML_DK ** -0.5)
    kaf[...] = yk
    kab[...] = yk.astype(BF16)

    x1, u2 = out_proj_stage(mix)

    r_i = lax.broadcasted_iota(jnp.int32, (L, L), 0)
    c_i = lax.broadcasted_iota(jnp.int32, (L, L), 1)
    ones_m = jnp.ones((L, L), BF16)
    incl_upper = (r_i <= c_i).astype(BF16)
    causal = c_i <= r_i
    lane_nr = lax.broadcasted_iota(jnp.int32, (ML_NR, L), 1)
    pad_rows = jnp.zeros((L - ML_NR, L), F32)

    def col_form(row_form):
        return jnp.concatenate([row_form, pad_rows], axis=0).T

    li = jnp.concatenate([gt_ref[0:ML_HEADS, c * L:(c + 1) * L] + bif_ref[0:ML_HEADS, :]
                          for c in range(ML_CPT)], axis=0)
    zf = jnp.concatenate([gt_ref[ML_HEADS:2 * ML_HEADS, c * L:(c + 1) * L] + bif_ref[ML_HEADS:2 * ML_HEADS, :]
                          for c in range(ML_CPT)], axis=0)
    lf = jnp.minimum(zf, 0.0) - jnp.log1p(jnp.exp(-jnp.abs(zf)))
    brow = _dot3(lf, incl_upper)
    bend = _dot3(lf, ones_m)
    u = brow - li
    g = bend - u
    maxg = jnp.max(g, axis=-1, keepdims=True)
    m = mch_scr[...]
    for c in range(ML_CPT):
        rs = slice(c * ML_HEADS, (c + 1) * ML_HEADS)
        rf[1, rs, :] = m
        m = jnp.maximum(bend[rs, :] + m, maxg[rs, :])
        rf[2, rs, :] = m
    mch_scr[...] = m
    m_cur = rf[1]
    m_nxt = rf[2]
    rf[0] = u
    rf[3] = jnp.exp(bend + m_cur - m_nxt)
    rf[4] = jnp.exp(g - m_nxt)
    pm = -u
    sh = 1
    while sh < L:
        pm = jnp.maximum(pm, jnp.where(lane_nr >= sh, pltpu.roll(pm, sh, axis=1), NEG))
        sh *= 2
    d1 = -jnp.maximum(m_cur, pm)
    dcol = col_form(d1)
    ecol = col_form(jnp.exp(d1 - brow))

    ones_v = jnp.ones((L, LANE), BF16)

    def prepare(c, p):
        rows = slice(c * L, (c + 1) * L)
        pl_ = slice(p * LANE, (p + 1) * LANE)
        r0 = c * ML_HEADS + p * ML_PAIR
        kc = kab[rows, pl_]
        kt = kaf[rows, pl_].T
        wk = jnp.where(head0_rows, rf[4, r0:r0 + 1, :], rf[4, r0 + 1:r0 + 2, :])
        ktw = (kt * wk).astype(BF16)
        ktw2 = jnp.concatenate([jnp.where(in_head_row[hh], ktw, jnp.zeros_like(ktw)) for hh in range(ML_PAIR)],
                               axis=1)
        vexts = [jnp.concatenate([mv_ref[rows, (r0 % ML_HEADS + hh) * LANE:(r0 % ML_HEADS + hh + 1) * LANE],
                                  ones_v], axis=1) for hh in range(ML_PAIR)]
        z = _dot(ktw2, jnp.concatenate(vexts, axis=0))
        ss = [_dot_nt(qa[hh, rows, pl_], kc) for hh in range(ML_PAIR)]
        drow = [jnp.broadcast_to(dcol[:, r0 + hh:r0 + hh + 1], (L, L)) for hh in range(ML_PAIR)]
        erow = [jnp.broadcast_to(ecol[:, r0 + hh:r0 + hh + 1], (L, L)) for hh in range(ML_PAIR)]
        return ss, z, vexts, drow, erow

    order = [(c, p) for c in range(ML_CPT) for p in range(ML_NPAIR)]
    assert 2 * (D_FF // TL_TK) == len(order)
    cst = [cst_scr[p] for p in range(ML_NPAIR)]
    nxt = prepare(*order[0])
    acc = None
    hsq = None
    for n, (c, p) in enumerate(order):
        rows = slice(c * L, (c + 1) * L)
        pl_ = slice(p * LANE, (p + 1) * LANE)
        r0 = c * ML_HEADS + p * ML_PAIR
        ss, z, vexts, drow, erow = nxt
        if n + 1 < len(order):
            nxt = prepare(*order[n + 1])
        cst_b = cst[p].astype(BF16)
        lhs = []
        for hh in range(ML_PAIR):
            r = r0 + hh
            w = ss[hh] * jnp.exp(jnp.where(causal, drow[hh] - rf[0, r:r + 1, :], NEG))
            qi = qa[hh, rows, pl_].astype(F32) * jnp.exp(drow[hh] + rf[1, r:r + 1, :])
            lhs.append(jnp.concatenate([w.astype(BF16), qi.astype(BF16)], axis=1))
        if n % 2 == 0:
            hsq = mlp_up(u2, n // 2)
        else:
            acc = mlp_down(hsq, acc, n // 2)
        for hh in range(ML_PAIR):
            cols = slice((p * ML_PAIR + hh) * LANE, (p * ML_PAIR + hh + 1) * LANE)
            tot = _dot(lhs[hh], jnp.concatenate([vexts[hh], cst_b], axis=0))
            hval = tot[:, 0:LANE] / jnp.maximum(jnp.abs(tot[:, LANE:2 * LANE]), erow[hh])
            msq = jnp.mean(hval * hval, axis=-1, keepdims=True)
            hn = hval * lax.rsqrt(msq + EPS) * ng_ref[:, cols]
            hml_scr[rows, cols] = (hn * jax.nn.sigmoid(mo_ref[rows, cols].astype(F32))).astype(hml_scr.dtype)
        dec = jnp.where(head0_rows, rf[3, r0:r0 + 1, :], rf[3, r0 + 1:r0 + 2, :])
        cst[p] = jnp.concatenate([dec, dec], axis=1) * cst[p] + z
    for p in range(ML_NPAIR):
        cst_scr[p] = cst[p]
    o_ref[...] = x1 + ada_ref[5:6, :] * acc


def _tail(x2, ada3, att2, proj2, gates_t, bif_b, conv_w, conv_b, norm_g, g2, wa, wm, wo, w1, w2, S):
    T = x2.shape[0]
    tm = TL_TM
    tps = S // tm
    n_tiles = T // tm
    cur = lambda i: jnp.minimum(i, n_tiles - 1)
    prv = lambda i: jnp.maximum(i - 1, 0)
    gcb = CB_G * LANE // D_MODEL
    in_specs = [
        pl.BlockSpec((tm, D_MODEL), lambda i: (prv(i), 0)),
        pl.BlockSpec((None, 6, D_MODEL), lambda i: (prv(i) // tps, 0, 0)),
        pl.BlockSpec((tm, ATT_W), lambda i: (prv(i), 0)),
        pl.BlockSpec((tm, D_MODEL), lambda i: (prv(i), gcb)),
        pl.BlockSpec((tm, D_MODEL), lambda i: (prv(i), gcb + 1)),
        _resident((1, D_MODEL)),
        _resident((ATT_W, D_MODEL)), _resident((ML_V_W, D_MODEL)), _resident((D_MODEL, D_MODEL)),
        _resident((D_MODEL, D_FF)), _resident((D_FF, D_MODEL)),
        pl.BlockSpec((tm, ML_QK_W), lambda i: (cur(i), CB_MQ * LANE // ML_QK_W)),
        pl.BlockSpec((tm, ML_QK_W), lambda i: (cur(i), CB_MK * LANE // ML_QK_W)),
        pl.BlockSpec((tm, ML_V_W), lambda i: (cur(i), CB_MV * LANE // ML_V_W)),
        pl.BlockSpec((tm, ML_V_W), lambda i: (cur(i), CB_MO * LANE // ML_V_W)),
        pl.BlockSpec((None, 2 * ML_HEADS, tm), lambda i: (cur(i) // tps, 0, cur(i) % tps)),
        _resident((2 * ML_HEADS, LANE)),
        _resident((CONV_W, 2 * ML_QK_W)), _resident((1, 2 * ML_QK_W)), _resident((1, ML_V_W)),
    ]
    scratch = [pltpu.VMEM((tm, ML_V_W), BF16),
               pltpu.VMEM((tm + XS_HDR, ML_QK_W), F32), pltpu.VMEM((tm + XS_HDR, ML_QK_W), F32),
               pltpu.VMEM((ML_PAIR, tm, ML_QK_W), BF16), pltpu.VMEM((tm, ML_QK_W), BF16),
               pltpu.VMEM((tm, ML_QK_W), F32),
               pltpu.VMEM((ML_NPAIR, LANE, 2 * LANE), F32), pltpu.VMEM((ML_HEADS, LANE), F32),
               pltpu.VMEM((5, ML_NR, ML_L), F32)]
    return pl.pallas_call(
        functools.partial(_tail_kernel, n_tiles=n_tiles, tiles_per_seq=tps),
        grid=(n_tiles + 1,),
        in_specs=in_specs,
        out_specs=pl.BlockSpec((tm, D_MODEL), lambda i: (prv(i), 0)),
        out_shape=jax.ShapeDtypeStruct((T, D_MODEL), F32),
        scratch_shapes=scratch,
        compiler_params=pltpu.CompilerParams(
            dimension_semantics=("arbitrary",), vmem_limit_bytes=VMEM_LIMIT),
        name="tail",
    )(x2, ada3, att2, proj2, proj2, g2, wa, wm, wo, w1, w2,
      proj2, proj2, proj2, proj2, gates_t, bif_b, conv_w, conv_b, norm_g)


def kernel(x, c, w_ada, b_ada, norm1_g, norm2_g, w_in, b_if, conv_w, conv_b, q_norm_g, k_norm_g,
           rel_bias, mlstm_norm_g, w_att_out, w_ml_out, w_out, w_ff1, w_ff2):
    B, S, D = x.shape
    T = B * S
    depth = w_ada.shape[0]
    bias_tiles = _bias_tiles(rel_bias)
    x2 = x.reshape(T, D)
    for l in range(depth):
        ada3 = _ada(c, w_ada[l], b_ada[l]).reshape(B, 6, D)
        w_in_b = w_in[l][:, :WIN_IF[0]].astype(BF16)
        w_gates = w_in[l][:, WIN_G[0]:WIN_G[1]].astype(BF16)
        w_if_t = w_in[l][:, WIN_IF[0]:WIN_IF[1]].T.astype(BF16)
        u2d, qkv = _inproj_att(x2, ada3, norm1_g[l].reshape(1, D), w_in_b,
                               q_norm_g[l].reshape(1, ATT_DH), k_norm_g[l].reshape(1, ATT_DH), S)
        proj2, gates_t, att = _mix(u2d, w_gates, w_in_b, w_if_t, qkv.reshape(B, S, AW), bias_tiles, B, S)
        bif_b = jnp.broadcast_to(b_if[l].reshape(2 * ML_HEADS, 1), (2 * ML_HEADS, LANE))
        x2 = _tail(x2, ada3, att.reshape(T, ATT_W), proj2, gates_t, bif_b, conv_w[l], conv_b[l].reshape(1, -1),
                   mlstm_norm_g[l].reshape(1, ML_V_W), norm2_g[l].reshape(1, D),
                   w_att_out[l].astype(BF16), w_ml_out[l].astype(BF16), w_out[l].astype(BF16),
                   w_ff1[l].astype(BF16), w_ff2[l].astype(BF16), S)
    return x2.reshape(B, S, D)
```

```python
import functools
import math

import numpy as np
import jax
import jax.numpy as jnp
from jax import lax
from jax.experimental import pallas as pl
from jax.experimental.pallas import tpu as pltpu

F32 = jnp.float32
BF16 = jnp.bfloat16

D_MODEL = 1024
ATT_GROUPS = ((128, 1), (512, 4), (2048, 16))
N_ATT_GROUPS = 3
ATT_HPG = 4
ATT_DH = 128
N_ATT_HEADS = 12
ATT_W = 512
ATT_BLK = 128
ML_HEADS = 8
ML_DK = 64
ML_DV = 128
ML_QK_W = 512
ML_V_W = 1024
CONV_W = 4
D_FF = 4096
N_BUCKETS = 32
MAX_DISTANCE = 2048
EPS = 1e-6
NEG = -1e30

LANE = 128
ML_L = 128

AW = 3 * N_ATT_HEADS * ATT_DH
PW = 5120
CB_G, CB_MV, CB_MO = 0, 16, 24
CB_MQ, CB_MK = 32, 36
WIN_ATT, WIN_MQK, WIN_MVO, WIN_IF, WIN_G = (0, 4608), (4608, 5632), (5632, 7680), (7680, 7696), (7696, 9744)

VMEM_LIMIT = 56 * 1024 * 1024


def _dot(a, b):
    return jnp.dot(a, b, preferred_element_type=F32)


def _dot_nt(a, b):
    return lax.dot_general(a, b, (((1,), (1,)), ((), ())), preferred_element_type=F32)


def _split3(a):
    hi = a.astype(BF16)
    r1 = a - hi.astype(F32)
    mid = r1.astype(BF16)
    lo = (r1 - mid.astype(F32)).astype(BF16)
    return hi, mid, lo


def _dot3(a, rhs_bf16):
    hi, mid, lo = _split3(a)
    return _dot(hi, rhs_bf16) + _dot(mid, rhs_bf16) + _dot(lo, rhs_bf16)


def _resident(shape):
    return pl.BlockSpec(shape, lambda *_: (0,) * len(shape), pipeline_mode=pl.Buffered(1))


def _ada_kernel(c_ref, w_ref, b_ref, o_ref):
    c = c_ref[...]
    s = c * jax.nn.sigmoid(c)
    o_ref[...] = _dot(s.astype(BF16), w_ref[...].astype(BF16)) + b_ref[...]


def _ada(c, w, b):
    B = c.shape[0]
    n = w.shape[1]
    tn = 1024
    return pl.pallas_call(
        _ada_kernel,
        grid=(n // tn,),
        in_specs=[pl.BlockSpec((B, D_MODEL), lambda j: (0, 0)),
                  pl.BlockSpec((D_MODEL, tn), lambda j: (0, j)),
                  pl.BlockSpec((1, tn), lambda j: (0, j))],
        out_specs=pl.BlockSpec((B, tn), lambda j: (0, j)),
        out_shape=jax.ShapeDtypeStruct((B, n), F32),
        name="ada",
    )(c, w, b.reshape(1, n))


def _t5_bucket_np(dist):
    max_exact = N_BUCKETS // 2
    d = np.maximum(dist, max_exact).astype(np.float32)
    large = max_exact + (np.log(d / np.float32(max_exact)) / np.float32(math.log(MAX_DISTANCE / max_exact))
                         * np.float32(N_BUCKETS - max_exact)).astype(np.int32)
    large = np.minimum(large, N_BUCKETS - 1)
    return np.where(dist < max_exact, dist, large).astype(np.int32)


def _bucket_tiles():
    i = np.arange(ATT_BLK)[:, None]
    j = np.arange(2 * ATT_BLK)[None, :]
    delta = ATT_BLK + i - j
    return np.stack([_t5_bucket_np(np.maximum(delta, 0) * dil) for _, dil in ATT_GROUPS])


def _bias_kernel(tab_ref, bucket_ref, o_ref):
    hh = pl.program_id(0)
    bucket = bucket_ref[...]
    acc = jnp.zeros(bucket.shape, F32)
    for k in range(N_BUCKETS):
        acc = jnp.where(bucket == k, tab_ref[k, hh], acc)
    i = lax.broadcasted_iota(jnp.int32, bucket.shape, 0)
    j = lax.broadcasted_iota(jnp.int32, bucket.shape, 1)
    delta = ATT_BLK + i - j
    valid = (delta >= 0) & (delta <= ATT_BLK)
    o_ref[0] = jnp.where(valid, acc, NEG)
    o_ref[1] = jnp.where(valid & (j >= ATT_BLK), acc, NEG)


def _bias_tiles(rel_bias):
    buckets = jnp.asarray(_bucket_tiles())
    return pl.pallas_call(
        _bias_kernel,
        grid=(N_ATT_HEADS,),
        in_specs=[pl.BlockSpec(memory_space=pltpu.SMEM),
                  pl.BlockSpec((None, ATT_BLK, 2 * ATT_BLK), lambda h: (h // ATT_HPG, 0, 0))],
        out_specs=pl.BlockSpec((None, 2, ATT_BLK, 2 * ATT_BLK), lambda h: (h, 0, 0, 0)),
        out_shape=jax.ShapeDtypeStruct((N_ATT_HEADS, 2, ATT_BLK, 2 * ATT_BLK), F32),
        name="bias_tiles",
    )(rel_bias, buckets)


IP_TM = 512
IP_TN = 512
IP_NQ = N_ATT_HEADS * ATT_DH // IP_TN


def _inproj_att_kernel(x_ref, ada_ref, g1_ref, w_ref, qg_ref, kg_ref, u_ref, o_ref):
    x = x_ref[...]
    ms = jnp.mean(x * x, axis=-1, keepdims=True)
    y = x * lax.rsqrt(ms + EPS) * g1_ref[...]
    ub = (y * (1.0 + ada_ref[1:2, :]) + ada_ref[0:1, :]).astype(BF16)
    u_ref[...] = ub
    for j in range(AW // IP_TN):
        acc = _dot(ub, w_ref[:, j * IP_TN:(j + 1) * IP_TN])
        if j < 2 * IP_NQ:
            gain = qg_ref[...] if j < IP_NQ else kg_ref[...]
            for k in range(IP_TN // ATT_DH):
                a = acc[:, k * ATT_DH:(k + 1) * ATT_DH]
                ms = jnp.mean(a * a, axis=-1, keepdims=True)
                lo = j * IP_TN + k * ATT_DH
                o_ref[:, lo:lo + ATT_DH] = (a * lax.rsqrt(ms + EPS) * gain).astype(o_ref.dtype)
        else:
            o_ref[:, j * IP_TN:(j + 1) * IP_TN] = acc.astype(o_ref.dtype)


def _inproj_att(x2, ada3, g1, w_att, qg, kg, S):
    T = x2.shape[0]
    tm = IP_TM
    tiles_per_seq = S // tm
    return pl.pallas_call(
        _inproj_att_kernel,
        grid=(T // tm,),
        in_specs=[pl.BlockSpec((tm, D_MODEL), lambda i: (i, 0)),
                  pl.BlockSpec((None, 6, D_MODEL), lambda i: (i // tiles_per_seq, 0, 0)),
                  _resident((1, D_MODEL)),
                  _resident((D_MODEL, AW)),
                  _resident((1, ATT_DH)),
                  _resident((1, ATT_DH))],
        out_specs=[pl.BlockSpec((tm, D_MODEL), lambda i: (i, 0)),
                   pl.BlockSpec((tm, AW), lambda i: (i, 0))],
        out_shape=[jax.ShapeDtypeStruct((T, D_MODEL), BF16),
                   jax.ShapeDtypeStruct((T, AW), BF16)],
        compiler_params=pltpu.CompilerParams(
            dimension_semantics=("parallel",), vmem_limit_bytes=VMEM_LIMIT),
        name="inproj_att",
    )(x2, ada3, g1, w_att, qg, kg)


ATT_UNROLL = 8


def _mix_kernel(u_ref, wg_ref, *rest, S):
    n_wb = len(MIX_WBLOCKS)
    _mix_body(u_ref, wg_ref, rest[:n_wb], *rest[n_wb:], S=S)


def _mix_body(u_ref, wg_ref, wb_refs, wif_ref,
                q0_ref, k0_ref, v0_ref, q1_ref, k1_ref, v1_ref, q2_ref, k2_ref, v2_ref, bias_ref,
                p_ref, gt_ref, o_ref,
                q1f, q1c, k1c, v1c, q2f, q2c, k2c, v2c, o_scr, l_scr, o2d, l2d, *, S):
    scale = ATT_DH ** -0.5
    blk = ATT_BLK

    ub = u_ref[...]

    n_g = (WIN_G[1] - WIN_G[0]) // IP_TN

    def proj_chunk(j):
        w = wg_ref[:, j * IP_TN:(j + 1) * IP_TN] if j < n_g else wb_refs[j - n_g][...]
        p_ref[:, j * IP_TN:(j + 1) * IP_TN] = _dot(ub, w).astype(p_ref.dtype)
    pending = [functools.partial(proj_chunk, j) for j in range(PW // IP_TN)]

    def emit_proj(n):
        for _ in range(min(n, len(pending))):
            pending.pop(0)()

    plan = [(1, 0), (1, 1), (1, 0), (1, 1), (1, 1), (1, 1)]

    gt_ref[...] = _dot_nt(wif_ref[...], ub)
    emit_proj(1)

    rs = ATT_GROUPS[1][1]
    for src, dst in ((q2_ref, q2c), (k2_ref, k2c), (v2_ref, v2c)):
        q1f[...] = src[...].astype(F32)
        for a in range(rs):
            q2f[a] = q1f[pl.ds(a, S // rs, stride=rs), :]
        for rho in range(rs * rs):
            dst[rho] = q2f[rho % rs, pl.ds(rho // rs, blk, stride=rs), :].astype(BF16)
    for src, dst in ((q1_ref, q1c), (k1_ref, k1c), (v1_ref, v1c)):
        q1f[...] = src[...].astype(F32)
        for a in range(rs):
            dst[a] = q1f[pl.ds(a, S // rs, stride=rs), :].astype(BF16)

    def softmax_blocks(ops):
        n_mid, n_end = plan.pop(0)
        ss = [_dot_nt(q, kk) * scale + bias for (q, kk, _, bias) in ops]
        emit_proj(n_mid)
        ms = [jnp.max(s, axis=-1, keepdims=True) for s in ss]
        ps = [jnp.exp(s - m) for s, m in zip(ss, ms)]
        ls = [jnp.sum(p, axis=-1, keepdims=True) for p in ps]
        accs = [_dot(p.astype(BF16), op[2]) for p, op in zip(ps, ops)]
        emit_proj(n_end)
        return [(acc / l, m + jnp.log(l)) for acc, l, m in zip(accs, ls, ms)]

    def run_dilated(n_batches, fetch, store):
        for it in range(n_batches):
            fetched = [fetch(it, u) for u in range(ATT_UNROLL)]
            res = softmax_blocks([f[:4] for f in fetched])
            for (o, lse), f in zip(res, fetched):
                store(f[4], o, jnp.broadcast_to(lse, (blk, LANE)))

    r1 = ATT_GROUPS[1][1]
    nb1 = S // r1 // blk
    assert ATT_UNROLL % nb1 == 0

    def fetch1(it, u):
        rho = it * (ATT_UNROLL // nb1) + u // nb1
        n = u % nb1
        start = rho + r1 * blk * n
        q = q1c[rho, n * blk:(n + 1) * blk, :]
        if n == 0:
            return (q, k1c[rho, 0:blk, :], v1c[rho, 0:blk, :], bias_ref[1, 1][:, blk:2 * blk], start)
        rows = slice((n - 1) * blk, (n + 1) * blk)
        return (q, k1c[rho, rows, :], v1c[rho, rows, :], bias_ref[1, 0], start)

    def store1(start, o, lse):
        rows = pl.ds(start, blk, stride=r1)
        o_scr[0, rows, :] = o
        l_scr[0, rows, :] = lse
    run_dilated(r1 * nb1 // ATT_UNROLL, fetch1, store1)

    r2 = ATT_GROUPS[2][1]
    assert S // r2 == blk

    assert r2 == rs * rs

    def fetch2(it, u):
        rho = it * ATT_UNROLL + u
        return (q2c[rho], k2c[rho], v2c[rho], bias_ref[2, 1][:, blk:2 * blk], rho)

    def store2(rho, o, lse):
        rows = pl.ds(rho // rs, blk, stride=rs)
        o2d[rho % rs, rows, :] = o
        l2d[rho % rs, rows, :] = lse
    run_dilated(r2 // ATT_UNROLL, fetch2, store2)
    for a in range(rs):
        o_scr[1, pl.ds(a, S // rs, stride=rs), :] = o2d[a]
        l_scr[1, pl.ds(a, S // rs, stride=rs), :] = l2d[a]

    for it in range(S // blk // ATT_UNROLL):
        ops = []
        for u in range(ATT_UNROLL):
            n = it * ATT_UNROLL + u
            q = q0_ref[n * blk:(n + 1) * blk, :]
            if n == 0:
                ops.append((q, k0_ref[0:blk, :], v0_ref[0:blk, :], bias_ref[0, 1][:, blk:2 * blk]))
            else:
                ops.append((q, k0_ref[(n - 1) * blk:(n + 1) * blk, :], v0_ref[(n - 1) * blk:(n + 1) * blk, :],
                            bias_ref[0, 0]))
        for u, (o0, lse0) in enumerate(softmax_blocks(ops)):
            n = it * ATT_UNROLL + u
            rows = slice(n * blk, (n + 1) * blk)
            l1, l2 = l_scr[0, rows, :], l_scr[1, rows, :]
            mx = jnp.maximum(jnp.maximum(lse0, l1), l2)
            e0, e1, e2 = jnp.exp(lse0 - mx), jnp.exp(l1 - mx), jnp.exp(l2 - mx)
            att = (e0 * o0 + e1 * o_scr[0, rows, :] + e2 * o_scr[1, rows, :]) / (e0 + e1 + e2)
            o_ref[rows, :] = att.astype(o_ref.dtype)
    emit_proj(len(pending))


MIX_WBLOCKS = tuple(range(WIN_MVO[0] // IP_TN, WIN_MVO[1] // IP_TN)) + tuple(
    range(WIN_MQK[0] // IP_TN, WIN_MQK[1] // IP_TN))


def _mix(u2d, w_gates, w_in_b, w_if_t, qkv3, bias_tiles, B, S):
    T = u2d.shape[0]
    tm = IP_TM
    tps = S // tm
    assert tps == ATT_HPG
    hpp = N_ATT_HEADS
    in_specs = [pl.BlockSpec((tm, D_MODEL), lambda b, j: (b * tps + j, 0)),
                _resident((D_MODEL, WIN_G[1] - WIN_G[0]))]
    in_specs += [pl.BlockSpec((D_MODEL, IP_TN), lambda b, j, cb=cb: (0, cb), pipeline_mode=pl.Buffered(1))
                 for cb in MIX_WBLOCKS]
    in_specs.append(_resident((16, D_MODEL)))
    for g in range(N_ATT_GROUPS):
        for part in range(3):
            in_specs.append(pl.BlockSpec((None, S, LANE),
                                         lambda b, j, cb=part * hpp + g * ATT_HPG: (b, 0, cb + j)))
    in_specs.append(pl.BlockSpec((N_ATT_GROUPS, None, 2, ATT_BLK, 2 * ATT_BLK), lambda b, j: (0, j, 0, 0, 0)))
    slab = pltpu.VMEM((ATT_GROUPS[1][1], S // ATT_GROUPS[1][1], LANE), F32)
    slab_b = pltpu.VMEM((ATT_GROUPS[1][1], S // ATT_GROUPS[1][1], LANE), BF16)
    cls_b = pltpu.VMEM((ATT_GROUPS[2][1], S // ATT_GROUPS[2][1], LANE), BF16)
    scratch = [pltpu.VMEM((S, LANE), F32), slab_b, slab_b, slab_b,
               slab, cls_b, cls_b, cls_b,
               pltpu.VMEM((N_ATT_GROUPS - 1, S, LANE), F32), pltpu.VMEM((N_ATT_GROUPS - 1, S, LANE), F32),
               slab, slab]
    bias5 = bias_tiles.reshape(N_ATT_GROUPS, ATT_HPG, 2, ATT_BLK, 2 * ATT_BLK)
    return pl.pallas_call(
        functools.partial(_mix_kernel, S=S),
        grid=(B, tps),
        in_specs=in_specs,
        out_specs=[pl.BlockSpec((tm, PW), lambda b, j: (b * tps + j, 0)),
                   pl.BlockSpec((None, 16, tm), lambda b, j: (b, 0, j)),
                   pl.BlockSpec((None, S, LANE), lambda b, j: (b, 0, j))],
        out_shape=[jax.ShapeDtypeStruct((T, PW), BF16),
                   jax.ShapeDtypeStruct((B, 16, S), F32),
                   jax.ShapeDtypeStruct((B, S, ATT_W), BF16)],
        scratch_shapes=scratch,
        compiler_params=pltpu.CompilerParams(
            dimension_semantics=("parallel", "parallel"), vmem_limit_bytes=VMEM_LIMIT),
        name="mix",
    )(u2d, w_gates, *([w_in_b] * len(MIX_WBLOCKS)), w_if_t, *([qkv3] * 9), bias5)


TL_TM = 512
TL_TK = 512
ML_PAIR = 2
ML_NPAIR = ML_HEADS // ML_PAIR
ML_CPT = TL_TM // ML_L
ML_NR = ML_CPT * ML_HEADS
XS_HDR = 8


def _tail_kernel(x_ref, ada_ref, att_ref, ga_ref, gm_ref, g2_ref, wa_ref, wm_ref, wo_ref, w1_ref, w2_ref,
                 mq_ref, mk_ref, mv_ref, mo_ref, gt_ref, bif_ref, cw_ref, cb_ref, ng_ref,
                 o_ref,
                 hml_scr, xq, xk, qa, kab, kaf, cst_scr, mch_scr, rf, *, n_tiles, tiles_per_seq):
    i = pl.program_id(0)
    L = ML_L
    tm = TL_TM
    im = jnp.minimum(i, n_tiles - 1)

    @pl.when(i == 0)
    def _():
        hml_scr[...] = jnp.zeros(hml_scr.shape, hml_scr.dtype)

    @pl.when(im % tiles_per_seq == 0)
    def _():
        cst_scr[...] = jnp.zeros(cst_scr.shape, F32)
        mch_scr[...] = jnp.zeros(mch_scr.shape, F32)
        xq[tm:tm + XS_HDR, :] = jnp.zeros((XS_HDR, ML_QK_W), F32)
        xk[tm:tm + XS_HDR, :] = jnp.zeros((XS_HDR, ML_QK_W), F32)


    def merge_stage():
        y_att = _dot(att_ref[...], wa_ref[...])
        y_ml = _dot(hml_scr[...], wm_ref[...])
        ga = jax.nn.sigmoid(ga_ref[...].astype(F32))
        gm = jax.nn.sigmoid(gm_ref[...].astype(F32))
        return (ga * y_att + gm * y_ml).astype(BF16)

    def out_proj_stage(mix):
        x1 = x_ref[...] + ada_ref[2:3, :] * _dot(mix, wo_ref[...])
        ms = jnp.mean(x1 * x1, axis=-1, keepdims=True)
        y = x1 * lax.rsqrt(ms + EPS) * g2_ref[...]
        return x1, (y * (1.0 + ada_ref[4:5, :]) + ada_ref[3:4, :]).astype(BF16)

    def mlp_up(u2, k):
        hdn = jnp.maximum(_dot(u2, w1_ref[:, k * TL_TK:(k + 1) * TL_TK]), 0.0)
        return (hdn * hdn).astype(BF16)

    def mlp_down(hsq, acc, k):
        part = _dot(hsq, w2_ref[k * TL_TK:(k + 1) * TL_TK, :])
        return part if acc is None else acc + part

    mix = merge_stage()

    lane = lax.broadcasted_iota(jnp.int32, (1, LANE), 1)
    srow = lax.broadcasted_iota(jnp.int32, (LANE, 1), 0)
    in_head_lane = [(lane >= hh * ML_DK) & (lane < (hh + 1) * ML_DK) for hh in range(ML_PAIR)]
    in_head_row = [(srow >= hh * ML_DK) & (srow < (hh + 1) * ML_DK) for hh in range(ML_PAIR)]
    head0_rows = in_head_row[0]

    def conv_silu(src_ref, xs, c0):
        xs[0:XS_HDR, :] = xs[tm:tm + XS_HDR, :]
        xs[XS_HDR:XS_HDR + tm, :] = src_ref[...].astype(F32)
        yv = cb_ref[:, c0:c0 + ML_QK_W]
        for jj in range(CONV_W):
            yv = yv + cw_ref[jj:jj + 1, c0:c0 + ML_QK_W] * xs[pl.ds(XS_HDR - (CONV_W - 1) + jj, tm), :]
        return yv * jax.nn.sigmoid(yv)

    yq = conv_silu(mq_ref, xq, 0)
    for p in range(ML_NPAIR):
        for hh in range(ML_PAIR):
            qa[hh, :, p * LANE:(p + 1) * LANE] = jnp.where(in_head_lane[hh], yq[:, p * LANE:(p + 1) * LANE],
                                                           0.0).astype(BF16)
    yk = conv_silu(mk_ref, xk, ML_QK_W) * (ML_DK ** -0.5)
    kaf[...] = yk
    kab[...] = yk.astype(BF16)

    x1, u2 = out_proj_stage(mix)

    r_i = lax.broadcasted_iota(jnp.int32, (L, L), 0)
    c_i = lax.broadcasted_iota(jnp.int32, (L, L), 1)
    ones_m = jnp.ones((L, L), BF16)
    incl_upper = (r_i <= c_i).astype(BF16)
    causal = c_i <= r_i
    lane_nr = lax.broadcasted_iota(jnp.int32, (ML_NR, L), 1)
    pad_rows = jnp.zeros((L - ML_NR, L), F32)

    def col_form(row_form):
        return jnp.concatenate([row_form, pad_rows], axis=0).T

    li = jnp.concatenate([gt_ref[0:ML_HEADS, c * L:(c + 1) * L] + bif_ref[0:ML_HEADS, :]
                          for c in range(ML_CPT)], axis=0)
    zf = jnp.concatenate([gt_ref[ML_HEADS:2 * ML_HEADS, c * L:(c + 1) * L] + bif_ref[ML_HEADS:2 * ML_HEADS, :]
                          for c in range(ML_CPT)], axis=0)
    lf = jnp.minimum(zf, 0.0) - jnp.log1p(jnp.exp(-jnp.abs(zf)))
    brow = _dot3(lf, incl_upper)
    bend = _dot3(lf, ones_m)
    u = brow - li
    g = bend - u
    maxg = jnp.max(g, axis=-1, keepdims=True)
    m = mch_scr[...]
    for c in range(ML_CPT):
        rs = slice(c * ML_HEADS, (c + 1) * ML_HEADS)
        rf[1, rs, :] = m
        m = jnp.maximum(bend[rs, :] + m, maxg[rs, :])
        rf[2, rs, :] = m
    mch_scr[...] = m
    m_cur = rf[1]
    m_nxt = rf[2]
    rf[0] = u
    rf[3] = jnp.exp(bend + m_cur - m_nxt)
    rf[4] = jnp.exp(g - m_nxt)
    pm = -u
    sh = 1
    while sh < L:
        pm = jnp.maximum(pm, jnp.where(lane_nr >= sh, pltpu.roll(pm, sh, axis=1), NEG))
        sh *= 2
    d1 = -jnp.maximum(m_cur, pm)
    dcol = col_form(d1)
    ecol = col_form(jnp.exp(d1 - brow))

    ones_v = jnp.ones((L, LANE), BF16)

    def prepare(c, p):
        rows = slice(c * L, (c + 1) * L)
        pl_ = slice(p * LANE, (p + 1) * LANE)
        r0 = c * ML_HEADS + p * ML_PAIR
        kc = kab[rows, pl_]
        kt = kaf[rows, pl_].T
        wk = jnp.where(head0_rows, rf[4, r0:r0 + 1, :], rf[4, r0 + 1:r0 + 2, :])
        ktw = (kt * wk).astype(BF16)
        ktw2 = jnp.concatenate([jnp.where(in_head_row[hh], ktw, jnp.zeros_like(ktw)) for hh in range(ML_PAIR)],
                               axis=1)
        vexts = [jnp.concatenate([mv_ref[rows, (r0 % ML_HEADS + hh) * LANE:(r0 % ML_HEADS + hh + 1) * LANE],
                                  ones_v], axis=1) for hh in range(ML_PAIR)]
        z = _dot(ktw2, jnp.concatenate(vexts, axis=0))
        ss = [_dot_nt(qa[hh, rows, pl_], kc) for hh in range(ML_PAIR)]
        drow = [jnp.broadcast_to(dcol[:, r0 + hh:r0 + hh + 1], (L, L)) for hh in range(ML_PAIR)]
        erow = [jnp.broadcast_to(ecol[:, r0 + hh:r0 + hh + 1], (L, L)) for hh in range(ML_PAIR)]
        return ss, z, vexts, drow, erow

    order = [(c, p) for c in range(ML_CPT) for p in range(ML_NPAIR)]
    assert 2 * (D_FF // TL_TK) == len(order)
    cst = [cst_scr[p] for p in range(ML_NPAIR)]
    nxt = prepare(*order[0])
    acc = None
    hsq = None
    for n, (c, p) in enumerate(order):
        rows = slice(c * L, (c + 1) * L)
        pl_ = slice(p * LANE, (p + 1) * LANE)
        r0 = c * ML_HEADS + p * ML_PAIR
        ss, z, vexts, drow, erow = nxt
        if n + 1 < len(order):
            nxt = prepare(*order[n + 1])
        cst_b = cst[p].astype(BF16)
        lhs = []
        for hh in range(ML_PAIR):
            r = r0 + hh
            w = ss[hh] * jnp.exp(jnp.where(causal, drow[hh] - rf[0, r:r + 1, :], NEG))
            qi = qa[hh, rows, pl_].astype(F32) * jnp.exp(drow[hh] + rf[1, r:r + 1, :])
            lhs.append(jnp.concatenate([w.astype(BF16), qi.astype(BF16)], axis=1))
        if n % 2 == 0:
            hsq = mlp_up(u2, n // 2)
        else:
            acc = mlp_down(hsq, acc, n // 2)
        for hh in range(ML_PAIR):
            cols = slice((p * ML_PAIR + hh) * LANE, (p * ML_PAIR + hh + 1) * LANE)
            tot = _dot(lhs[hh], jnp.concatenate([vexts[hh], cst_b], axis=0))
            hval = tot[:, 0:LANE] / jnp.maximum(jnp.abs(tot[:, LANE:2 * LANE]), erow[hh])
            msq = jnp.mean(hval * hval, axis=-1, keepdims=True)
            hn = hval * lax.rsqrt(msq + EPS) * ng_ref[:, cols]
            hml_scr[rows, cols] = (hn * jax.nn.sigmoid(mo_ref[rows, cols].astype(F32))).astype(hml_scr.dtype)
        dec = jnp.where(head0_rows, rf[3, r0:r0 + 1, :], rf[3, r0 + 1:r0 + 2, :])
        cst[p] = jnp.concatenate([dec, dec], axis=1) * cst[p] + z
    for p in range(ML_NPAIR):
        cst_scr[p] = cst[p]
    o_ref[...] = x1 + ada_ref[5:6, :] * acc


def _tail(x2, ada3, att2, proj2, gates_t, bif_b, conv_w, conv_b, norm_g, g2, wa, wm, wo, w1, w2, S):
    T = x2.shape[0]
    tm = TL_TM
    tps = S // tm
    n_tiles = T // tm
    cur = lambda i: jnp.minimum(i, n_tiles - 1)
    prv = lambda i: jnp.maximum(i - 1, 0)
    gcb = CB_G * LANE // D_MODEL
    in_specs = [
        pl.BlockSpec((tm, D_MODEL), lambda i: (prv(i), 0)),
        pl.BlockSpec((None, 6, D_MODEL), lambda i: (prv(i) // tps, 0, 0)),
        pl.BlockSpec((tm, ATT_W), lambda i: (prv(i), 0)),
        pl.BlockSpec((tm, D_MODEL), lambda i: (prv(i), gcb)),
        pl.BlockSpec((tm, D_MODEL), lambda i: (prv(i), gcb + 1)),
        _resident((1, D_MODEL)),
        _resident((ATT_W, D_MODEL)), _resident((ML_V_W, D_MODEL)), _resident((D_MODEL, D_MODEL)),
        _resident((D_MODEL, D_FF)), _resident((D_FF, D_MODEL)),
        pl.BlockSpec((tm, ML_QK_W), lambda i: (cur(i), CB_MQ * LANE // ML_QK_W)),
        pl.BlockSpec((tm, ML_QK_W), lambda i: (cur(i), CB_MK * LANE // ML_QK_W)),
        pl.BlockSpec((tm, ML_V_W), lambda i: (cur(i), CB_MV * LANE // ML_V_W)),
        pl.BlockSpec((tm, ML_V_W), lambda i: (cur(i), CB_MO * LANE // ML_V_W)),
        pl.BlockSpec((None, 2 * ML_HEADS, tm), lambda i: (cur(i) // tps, 0, cur(i) % tps)),
        _resident((2 * ML_HEADS, LANE)),
        _resident((CONV_W, 2 * ML_QK_W)), _resident((1, 2 * ML_QK_W)), _resident((1, ML_V_W)),
    ]
    scratch = [pltpu.VMEM((tm, ML_V_W), BF16),
               pltpu.VMEM((tm + XS_HDR, ML_QK_W), F32), pltpu.VMEM((tm + XS_HDR, ML_QK_W), F32),
               pltpu.VMEM((ML_PAIR, tm, ML_QK_W), BF16), pltpu.VMEM((tm, ML_QK_W), BF16),
               pltpu.VMEM((tm, ML_QK_W), F32),
               pltpu.VMEM((ML_NPAIR, LANE, 2 * LANE), F32), pltpu.VMEM((ML_HEADS, LANE), F32),
               pltpu.VMEM((5, ML_NR, ML_L), F32)]
    return pl.pallas_call(
        functools.partial(_tail_kernel, n_tiles=n_tiles, tiles_per_seq=tps),
        grid=(n_tiles + 1,),
        in_specs=in_specs,
        out_specs=pl.BlockSpec((tm, D_MODEL), lambda i: (prv(i), 0)),
        out_shape=jax.ShapeDtypeStruct((T, D_MODEL), F32),
        scratch_shapes=scratch,
        compiler_params=pltpu.CompilerParams(
            dimension_semantics=("arbitrary",), vmem_limit_bytes=VMEM_LIMIT),
        name="tail",
    )(x2, ada3, att2, proj2, proj2, g2, wa, wm, wo, w1, w2,
      proj2, proj2, proj2, proj2, gates_t, bif_b, conv_w, conv_b, norm_g)


def kernel(x, c, w_ada, b_ada, norm1_g, norm2_g, w_in, b_if, conv_w, conv_b, q_norm_g, k_norm_g,
           rel_bias, mlstm_norm_g, w_att_out, w_ml_out, w_out, w_ff1, w_ff2):
    B, S, D = x.shape
    T = B * S
    depth = w_ada.shape[0]
    bias_tiles = _bias_tiles(rel_bias)
    x2 = x.reshape(T, D)
    for l in range(depth):
        ada3 = _ada(c, w_ada[l], b_ada[l]).reshape(B, 6, D)
        w_in_b = w_in[l][:, :WIN_IF[0]].astype(BF16)
        w_gates = w_in[l][:, WIN_G[0]:WIN_G[1]].astype(BF16)
        w_if_t = w_in[l][:, WIN_IF[0]:WIN_IF[1]].T.astype(BF16)
        u2d, qkv = _inproj_att(x2, ada3, norm1_g[l].reshape(1, D), w_in_b,
                               q_norm_g[l].reshape(1, ATT_DH), k_norm_g[l].reshape(1, ATT_DH), S)
        proj2, gates_t, att = _mix(u2d, w_gates, w_in_b, w_if_t, qkv.reshape(B, S, AW), bias_tiles, B, S)
        bif_b = jnp.broadcast_to(b_if[l].reshape(2 * ML_HEADS, 1), (2 * ML_HEADS, LANE))
        x2 = _tail(x2, ada3, att.reshape(T, ATT_W), proj2, gates_t, bif_b, conv_w[l], conv_b[l].reshape(1, -1),
                   mlstm_norm_g[l].reshape(1, ML_V_W), norm2_g[l].reshape(1, D),
                   w_att_out[l].astype(BF16), w_ml_out[l].astype(BF16), w_out[l].astype(BF16),
                   w_ff1[l].astype(BF16), w_ff2[l].astype(BF16), S)
    return x2.reshape(B, S, D)
```

```python
import functools
import math

import numpy as np
import jax
import jax.numpy as jnp
from jax import lax
from jax.experimental import pallas as pl
from jax.experimental.pallas import tpu as pltpu

F32 = jnp.float32
BF16 = jnp.bfloat16

D_MODEL = 1024
ATT_GROUPS = ((128, 1), (512, 4), (2048, 16))
N_ATT_GROUPS = 3
ATT_HPG = 4
ATT_DH = 128
N_ATT_HEADS = 12
ATT_W = 512
ATT_BLK = 128
ML_HEADS = 8
ML_DK = 64
ML_DV = 128
ML_QK_W = 512
ML_V_W = 1024
CONV_W = 4
D_FF = 4096
N_BUCKETS = 32
MAX_DISTANCE = 2048
EPS = 1e-6
NEG = -1e30

LANE = 128
ML_L = 128

AW = 3 * N_ATT_HEADS * ATT_DH
PW = 5120
CB_G, CB_MV, CB_MO = 0, 16, 24
CB_MQ, CB_MK = 32, 36
WIN_ATT, WIN_MQK, WIN_MVO, WIN_IF, WIN_G = (0, 4608), (4608, 5632), (5632, 7680), (7680, 7696), (7696, 9744)

VMEM_LIMIT = 56 * 1024 * 1024


def _dot(a, b):
    return jnp.dot(a, b, preferred_element_type=F32)


def _dot_nt(a, b):
    return lax.dot_general(a, b, (((1,), (1,)), ((), ())), preferred_element_type=F32)


def _split3(a):
    hi = a.astype(BF16)
    r1 = a - hi.astype(F32)
    mid = r1.astype(BF16)
    lo = (r1 - mid.astype(F32)).astype(BF16)
    return hi, mid, lo


def _dot3(a, rhs_bf16):
    hi, mid, lo = _split3(a)
    return _dot(hi, rhs_bf16) + _dot(mid, rhs_bf16) + _dot(lo, rhs_bf16)


def _resident(shape):
    return pl.BlockSpec(shape, lambda *_: (0,) * len(shape), pipeline_mode=pl.Buffered(1))


def _ada_kernel(c_ref, w_ref, b_ref, o_ref):
    c = c_ref[...]
    s = c * jax.nn.sigmoid(c)
    o_ref[...] = _dot(s.astype(BF16), w_ref[...].astype(BF16)) + b_ref[...]


def _ada(c, w, b):
    B = c.shape[0]
    n = w.shape[1]
    tn = 1024
    return pl.pallas_call(
        _ada_kernel,
        grid=(n // tn,),
        in_specs=[pl.BlockSpec((B, D_MODEL), lambda j: (0, 0)),
                  pl.BlockSpec((D_MODEL, tn), lambda j: (0, j)),
                  pl.BlockSpec((1, tn), lambda j: (0, j))],
        out_specs=pl.BlockSpec((B, tn), lambda j: (0, j)),
        out_shape=jax.ShapeDtypeStruct((B, n), F32),
        name="ada",
    )(c, w, b.reshape(1, n))


def _t5_bucket_np(dist):
    max_exact = N_BUCKETS // 2
    d = np.maximum(dist, max_exact).astype(np.float32)
    large = max_exact + (np.log(d / np.float32(max_exact)) / np.float32(math.log(MAX_DISTANCE / max_exact))
                         * np.float32(N_BUCKETS - max_exact)).astype(np.int32)
    large = np.minimum(large, N_BUCKETS - 1)
    return np.where(dist < max_exact, dist, large).astype(np.int32)


def _bucket_tiles():
    i = np.arange(ATT_BLK)[:, None]
    j = np.arange(2 * ATT_BLK)[None, :]
    delta = ATT_BLK + i - j
    return np.stack([_t5_bucket_np(np.maximum(delta, 0) * dil) for _, dil in ATT_GROUPS])


def _bias_kernel(tab_ref, bucket_ref, o_ref):
    hh = pl.program_id(0)
    bucket = bucket_ref[...]
    acc = jnp.zeros(bucket.shape, F32)
    for k in range(N_BUCKETS):
        acc = jnp.where(bucket == k, tab_ref[k, hh], acc)
    i = lax.broadcasted_iota(jnp.int32, bucket.shape, 0)
    j = lax.broadcasted_iota(jnp.int32, bucket.shape, 1)
    delta = ATT_BLK + i - j
    valid = (delta >= 0) & (delta <= ATT_BLK)
    o_ref[0] = jnp.where(valid, acc, NEG)
    o_ref[1] = jnp.where(valid & (j >= ATT_BLK), acc, NEG)


def _bias_tiles(rel_bias):
    buckets = jnp.asarray(_bucket_tiles())
    return pl.pallas_call(
        _bias_kernel,
        grid=(N_ATT_HEADS,),
        in_specs=[pl.BlockSpec(memory_space=pltpu.SMEM),
                  pl.BlockSpec((None, ATT_BLK, 2 * ATT_BLK), lambda h: (h // ATT_HPG, 0, 0))],
        out_specs=pl.BlockSpec((None, 2, ATT_BLK, 2 * ATT_BLK), lambda h: (h, 0, 0, 0)),
        out_shape=jax.ShapeDtypeStruct((N_ATT_HEADS, 2, ATT_BLK, 2 * ATT_BLK), F32),
        name="bias_tiles",
    )(rel_bias, buckets)


IP_TM = 512
IP_TN = 512
IP_NQ = N_ATT_HEADS * ATT_DH // IP_TN


def _inproj_att_kernel(x_ref, ada_ref, g1_ref, w_ref, qg_ref, kg_ref, u_ref, o_ref):
    x = x_ref[...]
    ms = jnp.mean(x * x, axis=-1, keepdims=True)
    y = x * lax.rsqrt(ms + EPS) * g1_ref[...]
    ub = (y * (1.0 + ada_ref[1:2, :]) + ada_ref[0:1, :]).astype(BF16)
    u_ref[...] = ub
    for j in range(AW // IP_TN):
        acc = _dot(ub, w_ref[:, j * IP_TN:(j + 1) * IP_TN])
        if j < 2 * IP_NQ:
            gain = qg_ref[...] if j < IP_NQ else kg_ref[...]
            for k in range(IP_TN // ATT_DH):
                a = acc[:, k * ATT_DH:(k + 1) * ATT_DH]
                ms = jnp.mean(a * a, axis=-1, keepdims=True)
                lo = j * IP_TN + k * ATT_DH
                o_ref[:, lo:lo + ATT_DH] = (a * lax.rsqrt(ms + EPS) * gain).astype(o_ref.dtype)
        else:
            o_ref[:, j * IP_TN:(j + 1) * IP_TN] = acc.astype(o_ref.dtype)


def _inproj_att(x2, ada3, g1, w_att, qg, kg, S):
    T = x2.shape[0]
    tm = IP_TM
    tiles_per_seq = S // tm
    return pl.pallas_call(
        _inproj_att_kernel,
        grid=(T // tm,),
        in_specs=[pl.BlockSpec((tm, D_MODEL), lambda i: (i, 0)),
                  pl.BlockSpec((None, 6, D_MODEL), lambda i: (i // tiles_per_seq, 0, 0)),
                  _resident((1, D_MODEL)),
                  _resident((D_MODEL, AW)),
                  _resident((1, ATT_DH)),
                  _resident((1, ATT_DH))],
        out_specs=[pl.BlockSpec((tm, D_MODEL), lambda i: (i, 0)),
                   pl.BlockSpec((tm, AW), lambda i: (i, 0))],
        out_shape=[jax.ShapeDtypeStruct((T, D_MODEL), BF16),
                   jax.ShapeDtypeStruct((T, AW), BF16)],
        compiler_params=pltpu.CompilerParams(
            dimension_semantics=("parallel",), vmem_limit_bytes=VMEM_LIMIT),
        name="inproj_att",
    )(x2, ada3, g1, w_att, qg, kg)


ATT_UNROLL = 8


def _mix_kernel(u_ref, wg_ref, *rest, S):
    n_wb = len(MIX_WBLOCKS)
    _mix_body(u_ref, wg_ref, rest[:n_wb], *rest[n_wb:], S=S)


def _mix_body(u_ref, wg_ref, wb_refs, wif_ref,
                q0_ref, k0_ref, v0_ref, q1_ref, k1_ref, v1_ref, q2_ref, k2_ref, v2_ref, bias_ref,
                p_ref, gt_ref, o_ref,
                q1f, k1f, v1f, q2f, k2f, v2f, o_scr, l_scr, o2d, l2d, *, S):
    scale = ATT_DH ** -0.5
    blk = ATT_BLK

    ub = u_ref[...]

    n_g = (WIN_G[1] - WIN_G[0]) // IP_TN

    def proj_chunk(j):
        w = wg_ref[:, j * IP_TN:(j + 1) * IP_TN] if j < n_g else wb_refs[j - n_g][...]
        p_ref[:, j * IP_TN:(j + 1) * IP_TN] = _dot(ub, w).astype(p_ref.dtype)
    pending = [functools.partial(proj_chunk, j) for j in range(PW // IP_TN)]

    def emit_proj(n):
        for _ in range(min(n, len(pending))):
            pending.pop(0)()

    plan = [(1, 0), (1, 1), (1, 0), (1, 1), (1, 1), (1, 1)]

    gt_ref[...] = _dot_nt(wif_ref[...], ub)
    emit_proj(1)

    pad1 = blk * ATT_GROUPS[1][1]
    k1f[0:pad1, :] = jnp.zeros((pad1, LANE), F32)
    v1f[0:pad1, :] = jnp.zeros((pad1, LANE), F32)
    k1f[pad1:pad1 + S, :] = k1_ref[...].astype(F32)
    v1f[pad1:pad1 + S, :] = v1_ref[...].astype(F32)
    rs = ATT_GROUPS[1][1]
    for src, dst in ((q2_ref, q2f), (k2_ref, k2f), (v2_ref, v2f)):
        q1f[...] = src[...].astype(F32)
        for a in range(rs):
            dst[a] = q1f[pl.ds(a, S // rs, stride=rs), :]
    q1f[...] = q1_ref[...].astype(F32)

    def softmax_blocks(ops):
        n_mid, n_end = plan.pop(0)
        ss = [_dot_nt(q, kk) * scale + bias for (q, kk, _, bias) in ops]
        emit_proj(n_mid)
        ms = [jnp.max(s, axis=-1, keepdims=True) for s in ss]
        ps = [jnp.exp(s - m) for s, m in zip(ss, ms)]
        exts = [_dot(p.astype(BF16), jnp.concatenate([op[2], jnp.ones(op[2].shape, BF16)], axis=1))
                for p, op in zip(ps, ops)]
        emit_proj(n_end)
        return [(e[:, 0:LANE] / e[:, LANE:2 * LANE], m + jnp.log(e[:, LANE:2 * LANE])) for e, m in zip(exts, ms)]

    def run_dilated(n_batches, fetch, store):
        for it in range(n_batches):
            fetched = [fetch(it, u) for u in range(ATT_UNROLL)]
            res = softmax_blocks([f[:4] for f in fetched])
            for (o, lse), f in zip(res, fetched):
                store(f[4], o, lse)

    r1 = ATT_GROUPS[1][1]
    nb1 = S // r1 // blk
    assert ATT_UNROLL % nb1 == 0

    def fetch1(it, u):
        rho = it * (ATT_UNROLL // nb1) + u // nb1
        n = u % nb1
        start = rho + r1 * blk * n
        return (q1f[pl.ds(start, blk, stride=r1), :].astype(BF16),
                k1f[pl.ds(start, 2 * blk, stride=r1), :].astype(BF16),
                v1f[pl.ds(start, 2 * blk, stride=r1), :].astype(BF16),
                bias_ref[1, 1 if n == 0 else 0], start)

    def store1(start, o, lse):
        rows = pl.ds(start, blk, stride=r1)
        o_scr[0, rows, :] = o
        l_scr[0, rows, :] = lse
    run_dilated(r1 * nb1 // ATT_UNROLL, fetch1, store1)

    r2 = ATT_GROUPS[2][1]
    assert S // r2 == blk

    assert r2 == rs * rs

    def fetch2(it, u):
        rho = it * ATT_UNROLL + u
        rows = pl.ds(rho // rs, blk, stride=rs)
        return (q2f[rho % rs, rows, :].astype(BF16), k2f[rho % rs, rows, :].astype(BF16),
                v2f[rho % rs, rows, :].astype(BF16), bias_ref[2, 1][:, blk:2 * blk], rho)

    def store2(rho, o, lse):
        rows = pl.ds(rho // rs, blk, stride=rs)
        o2d[rho % rs, rows, :] = o
        l2d[rho % rs, rows, :] = lse
    run_dilated(r2 // ATT_UNROLL, fetch2, store2)
    for a in range(rs):
        o_scr[1, pl.ds(a, S // rs, stride=rs), :] = o2d[a]
        l_scr[1, pl.ds(a, S // rs, stride=rs), :] = l2d[a]

    for it in range(S // blk // ATT_UNROLL):
        ops = []
        for u in range(ATT_UNROLL):
            n = it * ATT_UNROLL + u
            q = q0_ref[n * blk:(n + 1) * blk, :]
            if n == 0:
                ops.append((q, k0_ref[0:blk, :], v0_ref[0:blk, :], bias_ref[0, 1][:, blk:2 * blk]))
            else:
                ops.append((q, k0_ref[(n - 1) * blk:(n + 1) * blk, :], v0_ref[(n - 1) * blk:(n + 1) * blk, :],
                            bias_ref[0, 0]))
        for u, (o0, lse0) in enumerate(softmax_blocks(ops)):
            n = it * ATT_UNROLL + u
            rows = slice(n * blk, (n + 1) * blk)
            l1, l2 = l_scr[0, rows, :], l_scr[1, rows, :]
            mx = jnp.maximum(jnp.maximum(lse0, l1), l2)
            e0, e1, e2 = jnp.exp(lse0 - mx), jnp.exp(l1 - mx), jnp.exp(l2 - mx)
            att = (e0 * o0 + e1 * o_scr[0, rows, :] + e2 * o_scr[1, rows, :]) / (e0 + e1 + e2)
            o_ref[rows, :] = att.astype(o_ref.dtype)
    emit_proj(len(pending))


MIX_WBLOCKS = tuple(range(WIN_MVO[0] // IP_TN, WIN_MVO[1] // IP_TN)) + tuple(
    range(WIN_MQK[0] // IP_TN, WIN_MQK[1] // IP_TN))


def _mix(u2d, w_gates, w_in_b, w_if_t, qkv3, bias_tiles, B, S):
    T = u2d.shape[0]
    tm = IP_TM
    tps = S // tm
    assert tps == ATT_HPG
    hpp = N_ATT_HEADS
    in_specs = [pl.BlockSpec((tm, D_MODEL), lambda b, j: (b * tps + j, 0)),
                _resident((D_MODEL, WIN_G[1] - WIN_G[0]))]
    in_specs += [pl.BlockSpec((D_MODEL, IP_TN), lambda b, j, cb=cb: (0, cb), pipeline_mode=pl.Buffered(1))
                 for cb in MIX_WBLOCKS]
    in_specs.append(_resident((16, D_MODEL)))
    for g in range(N_ATT_GROUPS):
        for part in range(3):
            in_specs.append(pl.BlockSpec((None, S, LANE),
                                         lambda b, j, cb=part * hpp + g * ATT_HPG: (b, 0, cb + j)))
    in_specs.append(pl.BlockSpec((N_ATT_GROUPS, None, 2, ATT_BLK, 2 * ATT_BLK), lambda b, j: (0, j, 0, 0, 0)))
    pad1 = ATT_BLK * ATT_GROUPS[1][1]
    slab = pltpu.VMEM((ATT_GROUPS[1][1], S // ATT_GROUPS[1][1], LANE), F32)
    scratch = [pltpu.VMEM((S, LANE), F32), pltpu.VMEM((pad1 + S, LANE), F32), pltpu.VMEM((pad1 + S, LANE), F32),
               slab, slab, slab,
               pltpu.VMEM((N_ATT_GROUPS - 1, S, LANE), F32), pltpu.VMEM((N_ATT_GROUPS - 1, S, LANE), F32),
               slab, slab]
    bias5 = bias_tiles.reshape(N_ATT_GROUPS, ATT_HPG, 2, ATT_BLK, 2 * ATT_BLK)
    return pl.pallas_call(
        functools.partial(_mix_kernel, S=S),
        grid=(B, tps),
        in_specs=in_specs,
        out_specs=[pl.BlockSpec((tm, PW), lambda b, j: (b * tps + j, 0)),
                   pl.BlockSpec((None, 16, tm), lambda b, j: (b, 0, j)),
                   pl.BlockSpec((None, S, LANE), lambda b, j: (b, 0, j))],
        out_shape=[jax.ShapeDtypeStruct((T, PW), BF16),
                   jax.ShapeDtypeStruct((B, 16, S), F32),
                   jax.ShapeDtypeStruct((B, S, ATT_W), BF16)],
        scratch_shapes=scratch,
        compiler_params=pltpu.CompilerParams(
            dimension_semantics=("parallel", "parallel"), vmem_limit_bytes=VMEM_LIMIT),
        name="mix",
    )(u2d, w_gates, *([w_in_b] * len(MIX_WBLOCKS)), w_if_t, *([qkv3] * 9), bias5)


TL_TM = 512
TL_TK = 512
ML_PAIR = 2
ML_NPAIR = ML_HEADS // ML_PAIR
ML_CPT = TL_TM // ML_L
ML_NR = ML_CPT * ML_HEADS
XS_HDR = 8


def _tail_kernel(x_ref, ada_ref, att_ref, ga_ref, gm_ref, g2_ref, wa_ref, wm_ref, wo_ref, w1_ref, w2_ref,
                 mq_ref, mk_ref, mv_ref, mo_ref, gt_ref, bif_ref, cw_ref, cb_ref, ng_ref,
                 o_ref,
                 hml_scr, xq, xk, qa, kab, kaf, cst_scr, mch_scr, rf, *, n_tiles, tiles_per_seq):
    i = pl.program_id(0)
    L = ML_L
    tm = TL_TM
    im = jnp.minimum(i, n_tiles - 1)

    @pl.when(i == 0)
    def _():
        hml_scr[...] = jnp.zeros(hml_scr.shape, hml_scr.dtype)

    @pl.when(im % tiles_per_seq == 0)
    def _():
        cst_scr[...] = jnp.zeros(cst_scr.shape, F32)
        mch_scr[...] = jnp.zeros(mch_scr.shape, F32)
        xq[tm:tm + XS_HDR, :] = jnp.zeros((XS_HDR, ML_QK_W), F32)
        xk[tm:tm + XS_HDR, :] = jnp.zeros((XS_HDR, ML_QK_W), F32)


    def merge_stage():
        y_att = _dot(att_ref[...], wa_ref[...])
        y_ml = _dot(hml_scr[...], wm_ref[...])
        ga = jax.nn.sigmoid(ga_ref[...].astype(F32))
        gm = jax.nn.sigmoid(gm_ref[...].astype(F32))
        return (ga * y_att + gm * y_ml).astype(BF16)

    def out_proj_stage(mix):
        x1 = x_ref[...] + ada_ref[2:3, :] * _dot(mix, wo_ref[...])
        ms = jnp.mean(x1 * x1, axis=-1, keepdims=True)
        y = x1 * lax.rsqrt(ms + EPS) * g2_ref[...]
        return x1, (y * (1.0 + ada_ref[4:5, :]) + ada_ref[3:4, :]).astype(BF16)

    def mlp_up(u2, k):
        hdn = jnp.maximum(_dot(u2, w1_ref[:, k * TL_TK:(k + 1) * TL_TK]), 0.0)
        return (hdn * hdn).astype(BF16)

    def mlp_down(hsq, acc, k):
        part = _dot(hsq, w2_ref[k * TL_TK:(k + 1) * TL_TK, :])
        return part if acc is None else acc + part

    mix = merge_stage()

    lane = lax.broadcasted_iota(jnp.int32, (1, LANE), 1)
    srow = lax.broadcasted_iota(jnp.int32, (LANE, 1), 0)
    in_head_lane = [(lane >= hh * ML_DK) & (lane < (hh + 1) * ML_DK) for hh in range(ML_PAIR)]
    in_head_row = [(srow >= hh * ML_DK) & (srow < (hh + 1) * ML_DK) for hh in range(ML_PAIR)]
    head0_rows = in_head_row[0]

    def conv_silu(src_ref, xs, c0):
        xs[0:XS_HDR, :] = xs[tm:tm + XS_HDR, :]
        xs[XS_HDR:XS_HDR + tm, :] = src_ref[...].astype(F32)
        yv = cb_ref[:, c0:c0 + ML_QK_W]
        for jj in range(CONV_W):
            yv = yv + cw_ref[jj:jj + 1, c0:c0 + ML_QK_W] * xs[pl.ds(XS_HDR - (CONV_W - 1) + jj, tm), :]
        return yv * jax.nn.sigmoid(yv)

    yq = conv_silu(mq_ref, xq, 0)
    for p in range(ML_NPAIR):
        for hh in range(ML_PAIR):
            qa[hh, :, p * LANE:(p + 1) * LANE] = jnp.where(in_head_lane[hh], yq[:, p * LANE:(p + 1) * LANE],
                                                           0.0).astype(BF16)
    yk = conv_silu(mk_ref, xk, ML_QK_W) * (ML_DK ** -0.5)
    kaf[...] = yk
    kab[...] = yk.astype(BF16)

    x1, u2 = out_proj_stage(mix)

    r_i = lax.broadcasted_iota(jnp.int32, (L, L), 0)
    c_i = lax.broadcasted_iota(jnp.int32, (L, L), 1)
    ones_m = jnp.ones((L, L), BF16)
    incl_upper = (r_i <= c_i).astype(BF16)
    causal = c_i <= r_i
    lane_nr = lax.broadcasted_iota(jnp.int32, (ML_NR, L), 1)
    pad_rows = jnp.zeros((L - ML_NR, L), F32)

    def col_form(row_form):
        return jnp.concatenate([row_form, pad_rows], axis=0).T

    li = jnp.concatenate([gt_ref[0:ML_HEADS, c * L:(c + 1) * L] + bif_ref[0:ML_HEADS, :]
                          for c in range(ML_CPT)], axis=0)
    zf = jnp.concatenate([gt_ref[ML_HEADS:2 * ML_HEADS, c * L:(c + 1) * L] + bif_ref[ML_HEADS:2 * ML_HEADS, :]
                          for c in range(ML_CPT)], axis=0)
    lf = jnp.minimum(zf, 0.0) - jnp.log1p(jnp.exp(-jnp.abs(zf)))
    brow = _dot3(lf, incl_upper)
    bend = _dot3(lf, ones_m)
    u = brow - li
    g = bend - u
    maxg = jnp.max(g, axis=-1, keepdims=True)
    m = mch_scr[...]
    for c in range(ML_CPT):
        rs = slice(c * ML_HEADS, (c + 1) * ML_HEADS)
        rf[1, rs, :] = m
        m = jnp.maximum(bend[rs, :] + m, maxg[rs, :])
        rf[2, rs, :] = m
    mch_scr[...] = m
    m_cur = rf[1]
    m_nxt = rf[2]
    rf[0] = u
    rf[3] = jnp.exp(bend + m_cur - m_nxt)
    rf[4] = jnp.exp(g - m_nxt)
    pm = -u
    sh = 1
    while sh < L:
        pm = jnp.maximum(pm, jnp.where(lane_nr >= sh, pltpu.roll(pm, sh, axis=1), NEG))
        sh *= 2
    d1 = -jnp.maximum(m_cur, pm)
    dcol = col_form(d1)
    ecol = col_form(jnp.exp(d1 - brow))

    ones_v = jnp.ones((L, LANE), BF16)

    def prepare(c, p):
        rows = slice(c * L, (c + 1) * L)
        pl_ = slice(p * LANE, (p + 1) * LANE)
        r0 = c * ML_HEADS + p * ML_PAIR
        kc = kab[rows, pl_]
        kt = kaf[rows, pl_].T
        wk = jnp.where(head0_rows, rf[4, r0:r0 + 1, :], rf[4, r0 + 1:r0 + 2, :])
        ktw = (kt * wk).astype(BF16)
        ktw2 = jnp.concatenate([jnp.where(in_head_row[hh], ktw, jnp.zeros_like(ktw)) for hh in range(ML_PAIR)],
                               axis=1)
        vexts = [jnp.concatenate([mv_ref[rows, (r0 % ML_HEADS + hh) * LANE:(r0 % ML_HEADS + hh + 1) * LANE],
                                  ones_v], axis=1) for hh in range(ML_PAIR)]
        z = _dot(ktw2, jnp.concatenate(vexts, axis=0))
        ss = [_dot_nt(qa[hh, rows, pl_], kc) for hh in range(ML_PAIR)]
        drow = [jnp.broadcast_to(dcol[:, r0 + hh:r0 + hh + 1], (L, L)) for hh in range(ML_PAIR)]
        erow = [jnp.broadcast_to(ecol[:, r0 + hh:r0 + hh + 1], (L, L)) for hh in range(ML_PAIR)]
        return ss, z, vexts, drow, erow

    order = [(c, p) for c in range(ML_CPT) for p in range(ML_NPAIR)]
    assert 2 * (D_FF // TL_TK) == len(order)
    cst = [cst_scr[p] for p in range(ML_NPAIR)]
    nxt = prepare(*order[0])
    acc = None
    hsq = None
    for n, (c, p) in enumerate(order):
        rows = slice(c * L, (c + 1) * L)
        pl_ = slice(p * LANE, (p + 1) * LANE)
        r0 = c * ML_HEADS + p * ML_PAIR
        ss, z, vexts, drow, erow = nxt
        if n + 1 < len(order):
            nxt = prepare(*order[n + 1])
        cst_b = cst[p].astype(BF16)
        lhs = []
        for hh in range(ML_PAIR):
            r = r0 + hh
            w = ss[hh] * jnp.exp(jnp.where(causal, drow[hh] - rf[0, r:r + 1, :], NEG))
            qi = qa[hh, rows, pl_].astype(F32) * jnp.exp(drow[hh] + rf[1, r:r + 1, :])
            lhs.append(jnp.concatenate([w.astype(BF16), qi.astype(BF16)], axis=1))
        if n % 2 == 0:
            hsq = mlp_up(u2, n // 2)
        else:
            acc = mlp_down(hsq, acc, n // 2)
        for hh in range(ML_PAIR):
            cols = slice((p * ML_PAIR + hh) * LANE, (p * ML_PAIR + hh + 1) * LANE)
            tot = _dot(lhs[hh], jnp.concatenate([vexts[hh], cst_b], axis=0))
            hval = tot[:, 0:LANE] / jnp.maximum(jnp.abs(tot[:, LANE:2 * LANE]), erow[hh])
            msq = jnp.mean(hval * hval, axis=-1, keepdims=True)
            hn = hval * lax.rsqrt(msq + EPS) * ng_ref[:, cols]
            hml_scr[rows, cols] = (hn * jax.nn.sigmoid(mo_ref[rows, cols].astype(F32))).astype(hml_scr.dtype)
        dec = jnp.where(head0_rows, rf[3, r0:r0 + 1, :], rf[3, r0 + 1:r0 + 2, :])
        cst[p] = jnp.concatenate([dec, dec], axis=1) * cst[p] + z
    for p in range(ML_NPAIR):
        cst_scr[p] = cst[p]
    o_ref[...] = x1 + ada_ref[5:6, :] * acc


def _tail(x2, ada3, att2, proj2, gates_t, bif_b, conv_w, conv_b, norm_g, g2, wa, wm, wo, w1, w2, S):
    T = x2.shape[0]
    tm = TL_TM
    tps = S // tm
    n_tiles = T // tm
    cur = lambda i: jnp.minimum(i, n_tiles - 1)
    prv = lambda i: jnp.maximum(i - 1, 0)
    gcb = CB_G * LANE // D_MODEL
    in_specs = [
        pl.BlockSpec((tm, D_MODEL), lambda i: (prv(i), 0)),
        pl.BlockSpec((None, 6, D_MODEL), lambda i: (prv(i) // tps, 0, 0)),
        pl.BlockSpec((tm, ATT_W), lambda i: (prv(i), 0)),
        pl.BlockSpec((tm, D_MODEL), lambda i: (prv(i), gcb)),
        pl.BlockSpec((tm, D_MODEL), lambda i: (prv(i), gcb + 1)),
        _resident((1, D_MODEL)),
        _resident((ATT_W, D_MODEL)), _resident((ML_V_W, D_MODEL)), _resident((D_MODEL, D_MODEL)),
        _resident((D_MODEL, D_FF)), _resident((D_FF, D_MODEL)),
        pl.BlockSpec((tm, ML_QK_W), lambda i: (cur(i), CB_MQ * LANE // ML_QK_W)),
        pl.BlockSpec((tm, ML_QK_W), lambda i: (cur(i), CB_MK * LANE // ML_QK_W)),
        pl.BlockSpec((tm, ML_V_W), lambda i: (cur(i), CB_MV * LANE // ML_V_W)),
        pl.BlockSpec((tm, ML_V_W), lambda i: (cur(i), CB_MO * LANE // ML_V_W)),
        pl.BlockSpec((None, 2 * ML_HEADS, tm), lambda i: (cur(i) // tps, 0, cur(i) % tps)),
        _resident((2 * ML_HEADS, LANE)),
        _resident((CONV_W, 2 * ML_QK_W)), _resident((1, 2 * ML_QK_W)), _resident((1, ML_V_W)),
    ]
    scratch = [pltpu.VMEM((tm, ML_V_W), BF16),
               pltpu.VMEM((tm + XS_HDR, ML_QK_W), F32), pltpu.VMEM((tm + XS_HDR, ML_QK_W), F32),
               pltpu.VMEM((ML_PAIR, tm, ML_QK_W), BF16), pltpu.VMEM((tm, ML_QK_W), BF16),
               pltpu.VMEM((tm, ML_QK_W), F32),
               pltpu.VMEM((ML_NPAIR, LANE, 2 * LANE), F32), pltpu.VMEM((ML_HEADS, LANE), F32),
               pltpu.VMEM((5, ML_NR, ML_L), F32)]
    return pl.pallas_call(
        functools.partial(_tail_kernel, n_tiles=n_tiles, tiles_per_seq=tps),
        grid=(n_tiles + 1,),
        in_specs=in_specs,
        out_specs=pl.BlockSpec((tm, D_MODEL), lambda i: (prv(i), 0)),
        out_shape=jax.ShapeDtypeStruct((T, D_MODEL), F32),
        scratch_shapes=scratch,
        compiler_params=pltpu.CompilerParams(
            dimension_semantics=("arbitrary",), vmem_limit_bytes=VMEM_LIMIT),
        name="tail",
    )(x2, ada3, att2, proj2, proj2, g2, wa, wm, wo, w1, w2,
      proj2, proj2, proj2, proj2, gates_t, bif_b, conv_w, conv_b, norm_g)


def kernel(x, c, w_ada, b_ada, norm1_g, norm2_g, w_in, b_if, conv_w, conv_b, q_norm_g, k_norm_g,
           rel_bias, mlstm_norm_g, w_att_out, w_ml_out, w_out, w_ff1, w_ff2):
    B, S, D = x.shape
    T = B * S
    depth = w_ada.shape[0]
    bias_tiles = _bias_tiles(rel_bias)
    x2 = x.reshape(T, D)
    for l in range(depth):
        ada3 = _ada(c, w_ada[l], b_ada[l]).reshape(B, 6, D)
        w_in_b = w_in[l][:, :WIN_IF[0]].astype(BF16)
        w_gates = w_in[l][:, WIN_G[0]:WIN_G[1]].astype(BF16)
        w_if_t = w_in[l][:, WIN_IF[0]:WIN_IF[1]].T.astype(BF16)
        u2d, qkv = _inproj_att(x2, ada3, norm1_g[l].reshape(1, D), w_in_b,
                               q_norm_g[l].reshape(1, ATT_DH), k_norm_g[l].reshape(1, ATT_DH), S)
        proj2, gates_t, att = _mix(u2d, w_gates, w_in_b, w_if_t, qkv.reshape(B, S, AW), bias_tiles, B, S)
        bif_b = jnp.broadcast_to(b_if[l].reshape(2 * ML_HEADS, 1), (2 * ML_HEADS, LANE))
        x2 = _tail(x2, ada3, att.reshape(T, ATT_W), proj2, gates_t, bif_b, conv_w[l], conv_b[l].reshape(1, -1),
                   mlstm_norm_g[l].reshape(1, ML_V_W), norm2_g[l].reshape(1, D),
                   w_att_out[l].astype(BF16), w_ml_out[l].astype(BF16), w_out[l].astype(BF16),
                   w_ff1[l].astype(BF16), w_ff2[l].astype(BF16), S)
    return x2.reshape(B, S, D)
```

```python
import functools
import math

import numpy as np
import jax
import jax.numpy as jnp
from jax import lax
from jax.experimental import pallas as pl
from jax.experimental.pallas import tpu as pltpu

F32 = jnp.float32
BF16 = jnp.bfloat16

D_MODEL = 1024
ATT_GROUPS = ((128, 1), (512, 4), (2048, 16))
N_ATT_GROUPS = 3
ATT_HPG = 4
ATT_DH = 128
N_ATT_HEADS = 12
ATT_W = 512
ATT_BLK = 128
ML_HEADS = 8
ML_DK = 64
ML_DV = 128
ML_QK_W = 512
ML_V_W = 1024
CONV_W = 4
D_FF = 4096
N_BUCKETS = 32
MAX_DISTANCE = 2048
EPS = 1e-6
NEG = -1e30

LANE = 128
ML_L = 128

AW = 3 * N_ATT_HEADS * ATT_DH
PW = 5120
CB_G, CB_MV, CB_MO = 0, 16, 24
CB_MQ, CB_MK = 32, 36
WIN_ATT, WIN_MQK, WIN_MVO, WIN_IF, WIN_G = (0, 4608), (4608, 5632), (5632, 7680), (7680, 7696), (7696, 9744)

VMEM_LIMIT = 56 * 1024 * 1024


def _dot(a, b):
    return jnp.dot(a, b, preferred_element_type=F32)


def _dot_nt(a, b):
    return lax.dot_general(a, b, (((1,), (1,)), ((), ())), preferred_element_type=F32)


def _split3(a):
    hi = a.astype(BF16)
    r1 = a - hi.astype(F32)
    mid = r1.astype(BF16)
    lo = (r1 - mid.astype(F32)).astype(BF16)
    return hi, mid, lo


def _dot3(a, rhs_bf16):
    hi, mid, lo = _split3(a)
    return _dot(hi, rhs_bf16) + _dot(mid, rhs_bf16) + _dot(lo, rhs_bf16)


def _resident(shape):
    return pl.BlockSpec(shape, lambda *_: (0,) * len(shape), pipeline_mode=pl.Buffered(1))


def _ada_kernel(c_ref, w_ref, b_ref, o_ref):
    c = c_ref[...]
    s = c * jax.nn.sigmoid(c)
    o_ref[...] = _dot(s.astype(BF16), w_ref[...].astype(BF16)) + b_ref[...]


def _ada(c, w, b):
    B = c.shape[0]
    n = w.shape[1]
    tn = 1024
    return pl.pallas_call(
        _ada_kernel,
        grid=(n // tn,),
        in_specs=[pl.BlockSpec((B, D_MODEL), lambda j: (0, 0)),
                  pl.BlockSpec((D_MODEL, tn), lambda j: (0, j)),
                  pl.BlockSpec((1, tn), lambda j: (0, j))],
        out_specs=pl.BlockSpec((B, tn), lambda j: (0, j)),
        out_shape=jax.ShapeDtypeStruct((B, n), F32),
        name="ada",
    )(c, w, b.reshape(1, n))


def _t5_bucket_np(dist):
    max_exact = N_BUCKETS // 2
    d = np.maximum(dist, max_exact).astype(np.float32)
    large = max_exact + (np.log(d / np.float32(max_exact)) / np.float32(math.log(MAX_DISTANCE / max_exact))
                         * np.float32(N_BUCKETS - max_exact)).astype(np.int32)
    large = np.minimum(large, N_BUCKETS - 1)
    return np.where(dist < max_exact, dist, large).astype(np.int32)


def _bucket_tiles():
    i = np.arange(ATT_BLK)[:, None]
    j = np.arange(2 * ATT_BLK)[None, :]
    delta = ATT_BLK + i - j
    return np.stack([_t5_bucket_np(np.maximum(delta, 0) * dil) for _, dil in ATT_GROUPS])


def _bias_kernel(tab_ref, bucket_ref, o_ref):
    hh = pl.program_id(0)
    bucket = bucket_ref[...]
    acc = jnp.zeros(bucket.shape, F32)
    for k in range(N_BUCKETS):
        acc = jnp.where(bucket == k, tab_ref[k, hh], acc)
    i = lax.broadcasted_iota(jnp.int32, bucket.shape, 0)
    j = lax.broadcasted_iota(jnp.int32, bucket.shape, 1)
    delta = ATT_BLK + i - j
    valid = (delta >= 0) & (delta <= ATT_BLK)
    o_ref[0] = jnp.where(valid, acc, NEG)
    o_ref[1] = jnp.where(valid & (j >= ATT_BLK), acc, NEG)


def _bias_tiles(rel_bias):
    buckets = jnp.asarray(_bucket_tiles())
    return pl.pallas_call(
        _bias_kernel,
        grid=(N_ATT_HEADS,),
        in_specs=[pl.BlockSpec(memory_space=pltpu.SMEM),
                  pl.BlockSpec((None, ATT_BLK, 2 * ATT_BLK), lambda h: (h // ATT_HPG, 0, 0))],
        out_specs=pl.BlockSpec((None, 2, ATT_BLK, 2 * ATT_BLK), lambda h: (h, 0, 0, 0)),
        out_shape=jax.ShapeDtypeStruct((N_ATT_HEADS, 2, ATT_BLK, 2 * ATT_BLK), F32),
        name="bias_tiles",
    )(rel_bias, buckets)


IP_TM = 512
IP_TN = 512
IP_NQ = N_ATT_HEADS * ATT_DH // IP_TN


def _inproj_att_kernel(x_ref, ada_ref, g1_ref, w_ref, qg_ref, kg_ref, u_ref, o_ref):
    x = x_ref[...]
    ms = jnp.mean(x * x, axis=-1, keepdims=True)
    y = x * lax.rsqrt(ms + EPS) * g1_ref[...]
    ub = (y * (1.0 + ada_ref[1:2, :]) + ada_ref[0:1, :]).astype(BF16)
    u_ref[...] = ub
    for j in range(AW // IP_TN):
        acc = _dot(ub, w_ref[:, j * IP_TN:(j + 1) * IP_TN])
        if j < 2 * IP_NQ:
            gain = qg_ref[...] * (ATT_DH ** -0.5) if j < IP_NQ else kg_ref[...]
            for k in range(IP_TN // ATT_DH):
                a = acc[:, k * ATT_DH:(k + 1) * ATT_DH]
                ms = jnp.mean(a * a, axis=-1, keepdims=True)
                lo = j * IP_TN + k * ATT_DH
                o_ref[:, lo:lo + ATT_DH] = (a * lax.rsqrt(ms + EPS) * gain).astype(o_ref.dtype)
        else:
            o_ref[:, j * IP_TN:(j + 1) * IP_TN] = acc.astype(o_ref.dtype)


def _inproj_att(x2, ada3, g1, w_att, qg, kg, S):
    T = x2.shape[0]
    tm = IP_TM
    tiles_per_seq = S // tm
    return pl.pallas_call(
        _inproj_att_kernel,
        grid=(T // tm,),
        in_specs=[pl.BlockSpec((tm, D_MODEL), lambda i: (i, 0)),
                  pl.BlockSpec((None, 6, D_MODEL), lambda i: (i // tiles_per_seq, 0, 0)),
                  _resident((1, D_MODEL)),
                  _resident((D_MODEL, AW)),
                  _resident((1, ATT_DH)),
                  _resident((1, ATT_DH))],
        out_specs=[pl.BlockSpec((tm, D_MODEL), lambda i: (i, 0)),
                   pl.BlockSpec((tm, AW), lambda i: (i, 0))],
        out_shape=[jax.ShapeDtypeStruct((T, D_MODEL), BF16),
                   jax.ShapeDtypeStruct((T, AW), BF16)],
        compiler_params=pltpu.CompilerParams(
            dimension_semantics=("parallel",), vmem_limit_bytes=VMEM_LIMIT),
        name="inproj_att",
    )(x2, ada3, g1, w_att, qg, kg)


ATT_UNROLL = 8


def _mix_kernel(u_ref, wg_ref, *rest, S):
    n_wb = len(MIX_WBLOCKS)
    _mix_body(u_ref, wg_ref, rest[:n_wb], *rest[n_wb:], S=S)


def _mix_body(u_ref, wg_ref, wb_refs, wif_ref,
                q0_ref, k0_ref, v0_ref, q1_ref, k1_ref, v1_ref, q2_ref, k2_ref, v2_ref, bias_ref,
                p_ref, gt_ref, o_ref,
                q1f, k1f, v1f, q2f, k2f, v2f, o_scr, l_scr, o2d, l2d, *, S):
    blk = ATT_BLK

    ub = u_ref[...]

    n_g = (WIN_G[1] - WIN_G[0]) // IP_TN

    def proj_chunk(j):
        w = wg_ref[:, j * IP_TN:(j + 1) * IP_TN] if j < n_g else wb_refs[j - n_g][...]
        p_ref[:, j * IP_TN:(j + 1) * IP_TN] = _dot(ub, w).astype(p_ref.dtype)
    pending = [functools.partial(proj_chunk, j) for j in range(PW // IP_TN)]

    def emit_proj(n):
        for _ in range(min(n, len(pending))):
            pending.pop(0)()

    plan = [(1, 0), (1, 1), (1, 0), (1, 1), (1, 1), (1, 1)]

    gt_ref[...] = _dot_nt(wif_ref[...], ub)
    emit_proj(1)

    pad1 = blk * ATT_GROUPS[1][1]
    k1f[0:pad1, :] = jnp.zeros((pad1, LANE), F32)
    v1f[0:pad1, :] = jnp.zeros((pad1, LANE), F32)
    k1f[pad1:pad1 + S, :] = k1_ref[...].astype(F32)
    v1f[pad1:pad1 + S, :] = v1_ref[...].astype(F32)
    rs = ATT_GROUPS[1][1]
    for src, dst in ((q2_ref, q2f), (k2_ref, k2f), (v2_ref, v2f)):
        q1f[...] = src[...].astype(F32)
        for a in range(rs):
            dst[a] = q1f[pl.ds(a, S // rs, stride=rs), :]
    q1f[...] = q1_ref[...].astype(F32)

    def softmax_blocks(ops):
        n_mid, n_end = plan.pop(0)
        ss = [_dot_nt(q, kk) + bias for (q, kk, _, bias) in ops]
        emit_proj(n_mid)
        ms = [jnp.max(s, axis=-1, keepdims=True) for s in ss]
        ps = [jnp.exp(s - m) for s, m in zip(ss, ms)]
        exts = [_dot(p.astype(BF16), jnp.concatenate([op[2], jnp.ones(op[2].shape, BF16)], axis=1))
                for p, op in zip(ps, ops)]
        emit_proj(n_end)
        return [(e[:, 0:LANE] / e[:, LANE:2 * LANE], m + jnp.log(e[:, LANE:2 * LANE])) for e, m in zip(exts, ms)]

    def run_dilated(n_batches, fetch, store):
        for it in range(n_batches):
            fetched = [fetch(it, u) for u in range(ATT_UNROLL)]
            res = softmax_blocks([f[:4] for f in fetched])
            for (o, lse), f in zip(res, fetched):
                store(f[4], o, lse)

    r1 = ATT_GROUPS[1][1]
    nb1 = S // r1 // blk
    assert ATT_UNROLL % nb1 == 0

    def fetch1(it, u):
        rho = it * (ATT_UNROLL // nb1) + u // nb1
        n = u % nb1
        start = rho + r1 * blk * n
        return (q1f[pl.ds(start, blk, stride=r1), :].astype(BF16),
                k1f[pl.ds(start, 2 * blk, stride=r1), :].astype(BF16),
                v1f[pl.ds(start, 2 * blk, stride=r1), :].astype(BF16),
                bias_ref[1, 1 if n == 0 else 0], start)

    def store1(start, o, lse):
        rows = pl.ds(start, blk, stride=r1)
        o_scr[0, rows, :] = o
        l_scr[0, rows, :] = lse
    run_dilated(r1 * nb1 // ATT_UNROLL, fetch1, store1)

    r2 = ATT_GROUPS[2][1]
    assert S // r2 == blk

    assert r2 == rs * rs

    def fetch2(it, u):
        rho = it * ATT_UNROLL + u
        rows = pl.ds(rho // rs, blk, stride=rs)
        return (q2f[rho % rs, rows, :].astype(BF16), k2f[rho % rs, rows, :].astype(BF16),
                v2f[rho % rs, rows, :].astype(BF16), bias_ref[2, 1][:, blk:2 * blk], rho)

    def store2(rho, o, lse):
        rows = pl.ds(rho // rs, blk, stride=rs)
        o2d[rho % rs, rows, :] = o
        l2d[rho % rs, rows, :] = lse
    run_dilated(r2 // ATT_UNROLL, fetch2, store2)
    for a in range(rs):
        o_scr[1, pl.ds(a, S // rs, stride=rs), :] = o2d[a]
        l_scr[1, pl.ds(a, S // rs, stride=rs), :] = l2d[a]

    for it in range(S // blk // ATT_UNROLL):
        ops = []
        for u in range(ATT_UNROLL):
            n = it * ATT_UNROLL + u
            q = q0_ref[n * blk:(n + 1) * blk, :]
            if n == 0:
                ops.append((q, k0_ref[0:blk, :], v0_ref[0:blk, :], bias_ref[0, 1][:, blk:2 * blk]))
            else:
                ops.append((q, k0_ref[(n - 1) * blk:(n + 1) * blk, :], v0_ref[(n - 1) * blk:(n + 1) * blk, :],
                            bias_ref[0, 0]))
        for u, (o0, lse0) in enumerate(softmax_blocks(ops)):
            n = it * ATT_UNROLL + u
            rows = slice(n * blk, (n + 1) * blk)
            l1, l2 = l_scr[0, rows, :], l_scr[1, rows, :]
            mx = jnp.maximum(jnp.maximum(lse0, l1), l2)
            e0, e1, e2 = jnp.exp(lse0 - mx), jnp.exp(l1 - mx), jnp.exp(l2 - mx)
            att = (e0 * o0 + e1 * o_scr[0, rows, :] + e2 * o_scr[1, rows, :]) / (e0 + e1 + e2)
            o_ref[rows, :] = att.astype(o_ref.dtype)
    emit_proj(len(pending))


MIX_WBLOCKS = tuple(range(WIN_MVO[0] // IP_TN, WIN_MVO[1] // IP_TN)) + tuple(
    range(WIN_MQK[0] // IP_TN, WIN_MQK[1] // IP_TN))


def _mix(u2d, w_gates, w_in_b, w_if_t, qkv3, bias_tiles, B, S):
    T = u2d.shape[0]
    tm = IP_TM
    tps = S // tm
    assert tps == ATT_HPG
    hpp = N_ATT_HEADS
    in_specs = [pl.BlockSpec((tm, D_MODEL), lambda b, j: (b * tps + j, 0)),
                _resident((D_MODEL, WIN_G[1] - WIN_G[0]))]
    in_specs += [pl.BlockSpec((D_MODEL, IP_TN), lambda b, j, cb=cb: (0, cb), pipeline_mode=pl.Buffered(1))
                 for cb in MIX_WBLOCKS]
    in_specs.append(_resident((16, D_MODEL)))
    for g in range(N_ATT_GROUPS):
        for part in range(3):
            in_specs.append(pl.BlockSpec((None, S, LANE),
                                         lambda b, j, cb=part * hpp + g * ATT_HPG: (b, 0, cb + j)))
    in_specs.append(pl.BlockSpec((N_ATT_GROUPS, None, 2, ATT_BLK, 2 * ATT_BLK), lambda b, j: (0, j, 0, 0, 0)))
    pad1 = ATT_BLK * ATT_GROUPS[1][1]
    slab = pltpu.VMEM((ATT_GROUPS[1][1], S // ATT_GROUPS[1][1], LANE), F32)
    scratch = [pltpu.VMEM((S, LANE), F32), pltpu.VMEM((pad1 + S, LANE), F32), pltpu.VMEM((pad1 + S, LANE), F32),
               slab, slab, slab,
               pltpu.VMEM((N_ATT_GROUPS - 1, S, LANE), F32), pltpu.VMEM((N_ATT_GROUPS - 1, S, LANE), F32),
               slab, slab]
    bias5 = bias_tiles.reshape(N_ATT_GROUPS, ATT_HPG, 2, ATT_BLK, 2 * ATT_BLK)
    return pl.pallas_call(
        functools.partial(_mix_kernel, S=S),
        grid=(B, tps),
        in_specs=in_specs,
        out_specs=[pl.BlockSpec((tm, PW), lambda b, j: (b * tps + j, 0)),
                   pl.BlockSpec((None, 16, tm), lambda b, j: (b, 0, j)),
                   pl.BlockSpec((None, S, LANE), lambda b, j: (b, 0, j))],
        out_shape=[jax.ShapeDtypeStruct((T, PW), BF16),
                   jax.ShapeDtypeStruct((B, 16, S), F32),
                   jax.ShapeDtypeStruct((B, S, ATT_W), BF16)],
        scratch_shapes=scratch,
        compiler_params=pltpu.CompilerParams(
            dimension_semantics=("parallel", "parallel"), vmem_limit_bytes=VMEM_LIMIT),
        name="mix",
    )(u2d, w_gates, *([w_in_b] * len(MIX_WBLOCKS)), w_if_t, *([qkv3] * 9), bias5)


TL_TM = 512
TL_TK = 512
ML_PAIR = 2
ML_NPAIR = ML_HEADS // ML_PAIR
ML_CPT = TL_TM // ML_L
ML_NR = ML_CPT * ML_HEADS
XS_HDR = 8


def _tail_kernel(x_ref, ada_ref, att_ref, ga_ref, gm_ref, g2_ref, wa_ref, wm_ref, wo_ref, w1_ref, w2_ref,
                 mq_ref, mk_ref, mv_ref, mo_ref, gt_ref, bif_ref, cw_ref, cb_ref, ng_ref,
                 o_ref,
                 hml_scr, xq, xk, qa, kab, kaf, cst_scr, mch_scr, rf, *, n_tiles, tiles_per_seq):
    i = pl.program_id(0)
    L = ML_L
    tm = TL_TM
    im = jnp.minimum(i, n_tiles - 1)

    @pl.when(i == 0)
    def _():
        hml_scr[...] = jnp.zeros(hml_scr.shape, hml_scr.dtype)

    @pl.when(im % tiles_per_seq == 0)
    def _():
        cst_scr[...] = jnp.zeros(cst_scr.shape, F32)
        mch_scr[...] = jnp.zeros(mch_scr.shape, F32)
        xq[tm:tm + XS_HDR, :] = jnp.zeros((XS_HDR, ML_QK_W), F32)
        xk[tm:tm + XS_HDR, :] = jnp.zeros((XS_HDR, ML_QK_W), F32)


    def merge_stage():
        y_att = _dot(att_ref[...], wa_ref[...])
        y_ml = _dot(hml_scr[...], wm_ref[...])
        ga = jax.nn.sigmoid(ga_ref[...].astype(F32))
        gm = jax.nn.sigmoid(gm_ref[...].astype(F32))
        return (ga * y_att + gm * y_ml).astype(BF16)

    def out_proj_stage(mix):
        x1 = x_ref[...] + ada_ref[2:3, :] * _dot(mix, wo_ref[...])
        ms = jnp.mean(x1 * x1, axis=-1, keepdims=True)
        y = x1 * lax.rsqrt(ms + EPS) * g2_ref[...]
        return x1, (y * (1.0 + ada_ref[4:5, :]) + ada_ref[3:4, :]).astype(BF16)

    def mlp_up(u2, k):
        hdn = jnp.maximum(_dot(u2, w1_ref[:, k * TL_TK:(k + 1) * TL_TK]), 0.0)
        return (hdn * hdn).astype(BF16)

    def mlp_down(hsq, acc, k):
        part = _dot(hsq, w2_ref[k * TL_TK:(k + 1) * TL_TK, :])
        return part if acc is None else acc + part

    mix = merge_stage()

    lane = lax.broadcasted_iota(jnp.int32, (1, LANE), 1)
    srow = lax.broadcasted_iota(jnp.int32, (LANE, 1), 0)
    in_head_lane = [(lane >= hh * ML_DK) & (lane < (hh + 1) * ML_DK) for hh in range(ML_PAIR)]
    in_head_row = [(srow >= hh * ML_DK) & (srow < (hh + 1) * ML_DK) for hh in range(ML_PAIR)]
    head0_rows = in_head_row[0]

    def conv_silu(src_ref, xs, c0):
        xs[0:XS_HDR, :] = xs[tm:tm + XS_HDR, :]
        xs[XS_HDR:XS_HDR + tm, :] = src_ref[...].astype(F32)
        yv = cb_ref[:, c0:c0 + ML_QK_W]
        for jj in range(CONV_W):
            yv = yv + cw_ref[jj:jj + 1, c0:c0 + ML_QK_W] * xs[pl.ds(XS_HDR - (CONV_W - 1) + jj, tm), :]
        return yv * jax.nn.sigmoid(yv)

    yq = conv_silu(mq_ref, xq, 0)
    for p in range(ML_NPAIR):
        for hh in range(ML_PAIR):
            qa[hh, :, p * LANE:(p + 1) * LANE] = jnp.where(in_head_lane[hh], yq[:, p * LANE:(p + 1) * LANE],
                                                           0.0).astype(BF16)
    yk = conv_silu(mk_ref, xk, ML_QK_W) * (ML_DK ** -0.5)
    kaf[...] = yk
    kab[...] = yk.astype(BF16)

    x1, u2 = out_proj_stage(mix)

    r_i = lax.broadcasted_iota(jnp.int32, (L, L), 0)
    c_i = lax.broadcasted_iota(jnp.int32, (L, L), 1)
    ones_m = jnp.ones((L, L), BF16)
    incl_upper = (r_i <= c_i).astype(BF16)
    causal = c_i <= r_i
    lane_nr = lax.broadcasted_iota(jnp.int32, (ML_NR, L), 1)
    pad_rows = jnp.zeros((L - ML_NR, L), F32)

    def col_form(row_form):
        return jnp.concatenate([row_form, pad_rows], axis=0).T

    li = jnp.concatenate([gt_ref[0:ML_HEADS, c * L:(c + 1) * L] + bif_ref[0:ML_HEADS, :]
                          for c in range(ML_CPT)], axis=0)
    zf = jnp.concatenate([gt_ref[ML_HEADS:2 * ML_HEADS, c * L:(c + 1) * L] + bif_ref[ML_HEADS:2 * ML_HEADS, :]
                          for c in range(ML_CPT)], axis=0)
    lf = jnp.minimum(zf, 0.0) - jnp.log1p(jnp.exp(-jnp.abs(zf)))
    brow = _dot3(lf, incl_upper)
    bend = _dot3(lf, ones_m)
    u = brow - li
    g = bend - u
    maxg = jnp.max(g, axis=-1, keepdims=True)
    m = mch_scr[...]
    for c in range(ML_CPT):
        rs = slice(c * ML_HEADS, (c + 1) * ML_HEADS)
        rf[1, rs, :] = m
        m = jnp.maximum(bend[rs, :] + m, maxg[rs, :])
        rf[2, rs, :] = m
    mch_scr[...] = m
    m_cur = rf[1]
    m_nxt = rf[2]
    rf[0] = u
    rf[3] = jnp.exp(bend + m_cur - m_nxt)
    rf[4] = jnp.exp(g - m_nxt)
    pm = -u
    sh = 1
    while sh < L:
        pm = jnp.maximum(pm, jnp.where(lane_nr >= sh, pltpu.roll(pm, sh, axis=1), NEG))
        sh *= 2
    d1 = -jnp.maximum(m_cur, pm)
    dcol = col_form(d1)
    ecol = col_form(jnp.exp(d1 - brow))

    ones_v = jnp.ones((L, LANE), BF16)

    def prepare(c, p):
        rows = slice(c * L, (c + 1) * L)
        pl_ = slice(p * LANE, (p + 1) * LANE)
        r0 = c * ML_HEADS + p * ML_PAIR
        kc = kab[rows, pl_]
        kt = kaf[rows, pl_].T
        wk = jnp.where(head0_rows, rf[4, r0:r0 + 1, :], rf[4, r0 + 1:r0 + 2, :])
        ktw = (kt * wk).astype(BF16)
        ktw2 = jnp.concatenate([jnp.where(in_head_row[hh], ktw, jnp.zeros_like(ktw)) for hh in range(ML_PAIR)],
                               axis=1)
        vexts = [jnp.concatenate([mv_ref[rows, (r0 % ML_HEADS + hh) * LANE:(r0 % ML_HEADS + hh + 1) * LANE],
                                  ones_v], axis=1) for hh in range(ML_PAIR)]
        z = _dot(ktw2, jnp.concatenate(vexts, axis=0))
        ss = [_dot_nt(qa[hh, rows, pl_], kc) for hh in range(ML_PAIR)]
        drow = [jnp.broadcast_to(dcol[:, r0 + hh:r0 + hh + 1], (L, L)) for hh in range(ML_PAIR)]
        erow = [jnp.broadcast_to(ecol[:, r0 + hh:r0 + hh + 1], (L, L)) for hh in range(ML_PAIR)]
        return ss, z, vexts, drow, erow

    order = [(c, p) for c in range(ML_CPT) for p in range(ML_NPAIR)]
    assert 2 * (D_FF // TL_TK) == len(order)
    cst = [cst_scr[p] for p in range(ML_NPAIR)]
    nxt = prepare(*order[0])
    acc = None
    hsq = None
    for n, (c, p) in enumerate(order):
        rows = slice(c * L, (c + 1) * L)
        pl_ = slice(p * LANE, (p + 1) * LANE)
        r0 = c * ML_HEADS + p * ML_PAIR
        ss, z, vexts, drow, erow = nxt
        if n + 1 < len(order):
            nxt = prepare(*order[n + 1])
        cst_b = cst[p].astype(BF16)
        lhs = []
        for hh in range(ML_PAIR):
            r = r0 + hh
            w = ss[hh] * jnp.exp(jnp.where(causal, drow[hh] - rf[0, r:r + 1, :], NEG))
            qi = qa[hh, rows, pl_].astype(F32) * jnp.exp(drow[hh] + rf[1, r:r + 1, :])
            lhs.append(jnp.concatenate([w.astype(BF16), qi.astype(BF16)], axis=1))
        if n % 2 == 0:
            hsq = mlp_up(u2, n // 2)
        else:
            acc = mlp_down(hsq, acc, n // 2)
        for hh in range(ML_PAIR):
            cols = slice((p * ML_PAIR + hh) * LANE, (p * ML_PAIR + hh + 1) * LANE)
            tot = _dot(lhs[hh], jnp.concatenate([vexts[hh], cst_b], axis=0))
            hval = tot[:, 0:LANE] / jnp.maximum(jnp.abs(tot[:, LANE:2 * LANE]), erow[hh])
            msq = jnp.mean(hval * hval, axis=-1, keepdims=True)
            hn = hval * lax.rsqrt(msq + EPS) * ng_ref[:, cols]
            hml_scr[rows, cols] = (hn * jax.nn.sigmoid(mo_ref[rows, cols].astype(F32))).astype(hml_scr.dtype)
        dec = jnp.where(head0_rows, rf[3, r0:r0 + 1, :], rf[3, r0 + 1:r0 + 2, :])
        cst[p] = jnp.concatenate([dec, dec], axis=1) * cst[p] + z
    for p in range(ML_NPAIR):
        cst_scr[p] = cst[p]
    o_ref[...] = x1 + ada_ref[5:6, :] * acc


def _tail(x2, ada3, att2, proj2, gates_t, bif_b, conv_w, conv_b, norm_g, g2, wa, wm, wo, w1, w2, S):
    T = x2.shape[0]
    tm = TL_TM
    tps = S // tm
    n_tiles = T // tm
    cur = lambda i: jnp.minimum(i, n_tiles - 1)
    prv = lambda i: jnp.maximum(i - 1, 0)
    gcb = CB_G * LANE // D_MODEL
    in_specs = [
        pl.BlockSpec((tm, D_MODEL), lambda i: (prv(i), 0)),
        pl.BlockSpec((None, 6, D_MODEL), lambda i: (prv(i) // tps, 0, 0)),
        pl.BlockSpec((tm, ATT_W), lambda i: (prv(i), 0)),
        pl.BlockSpec((tm, D_MODEL), lambda i: (prv(i), gcb)),
        pl.BlockSpec((tm, D_MODEL), lambda i: (prv(i), gcb + 1)),
        _resident((1, D_MODEL)),
        _resident((ATT_W, D_MODEL)), _resident((ML_V_W, D_MODEL)), _resident((D_MODEL, D_MODEL)),
        _resident((D_MODEL, D_FF)), _resident((D_FF, D_MODEL)),
        pl.BlockSpec((tm, ML_QK_W), lambda i: (cur(i), CB_MQ * LANE // ML_QK_W)),
        pl.BlockSpec((tm, ML_QK_W), lambda i: (cur(i), CB_MK * LANE // ML_QK_W)),
        pl.BlockSpec((tm, ML_V_W), lambda i: (cur(i), CB_MV * LANE // ML_V_W)),
        pl.BlockSpec((tm, ML_V_W), lambda i: (cur(i), CB_MO * LANE // ML_V_W)),
        pl.BlockSpec((None, 2 * ML_HEADS, tm), lambda i: (cur(i) // tps, 0, cur(i) % tps)),
        _resident((2 * ML_HEADS, LANE)),
        _resident((CONV_W, 2 * ML_QK_W)), _resident((1, 2 * ML_QK_W)), _resident((1, ML_V_W)),
    ]
    scratch = [pltpu.VMEM((tm, ML_V_W), BF16),
               pltpu.VMEM((tm + XS_HDR, ML_QK_W), F32), pltpu.VMEM((tm + XS_HDR, ML_QK_W), F32),
               pltpu.VMEM((ML_PAIR, tm, ML_QK_W), BF16), pltpu.VMEM((tm, ML_QK_W), BF16),
               pltpu.VMEM((tm, ML_QK_W), F32),
               pltpu.VMEM((ML_NPAIR, LANE, 2 * LANE), F32), pltpu.VMEM((ML_HEADS, LANE), F32),
               pltpu.VMEM((5, ML_NR, ML_L), F32)]
    return pl.pallas_call(
        functools.partial(_tail_kernel, n_tiles=n_tiles, tiles_per_seq=tps),
        grid=(n_tiles + 1,),
        in_specs=in_specs,
        out_specs=pl.BlockSpec((tm, D_MODEL), lambda i: (prv(i), 0)),
        out_shape=jax.ShapeDtypeStruct((T, D_MODEL), F32),
        scratch_shapes=scratch,
        compiler_params=pltpu.CompilerParams(
            dimension_semantics=("arbitrary",), vmem_limit_bytes=VMEM_LIMIT),
        name="tail",
    )(x2, ada3, att2, proj2, proj2, g2, wa, wm, wo, w1, w2,
      proj2, proj2, proj2, proj2, gates_t, bif_b, conv_w, conv_b, norm_g)


def kernel(x, c, w_ada, b_ada, norm1_g, norm2_g, w_in, b_if, conv_w, conv_b, q_norm_g, k_norm_g,
           rel_bias, mlstm_norm_g, w_att_out, w_ml_out, w_out, w_ff1, w_ff2):
    B, S, D = x.shape
    T = B * S
    depth = w_ada.shape[0]
    bias_tiles = _bias_tiles(rel_bias)
    x2 = x.reshape(T, D)
    for l in range(depth):
        ada3 = _ada(c, w_ada[l], b_ada[l]).reshape(B, 6, D)
        w_in_b = w_in[l][:, :WIN_IF[0]].astype(BF16)
        w_gates = w_in[l][:, WIN_G[0]:WIN_G[1]].astype(BF16)
        w_if_t = w_in[l][:, WIN_IF[0]:WIN_IF[1]].T.astype(BF16)
        u2d, qkv = _inproj_att(x2, ada3, norm1_g[l].reshape(1, D), w_in_b,
                               q_norm_g[l].reshape(1, ATT_DH), k_norm_g[l].reshape(1, ATT_DH), S)
        proj2, gates_t, att = _mix(u2d, w_gates, w_in_b, w_if_t, qkv.reshape(B, S, AW), bias_tiles, B, S)
        bif_b = jnp.broadcast_to(b_if[l].reshape(2 * ML_HEADS, 1), (2 * ML_HEADS, LANE))
        x2 = _tail(x2, ada3, att.reshape(T, ATT_W), proj2, gates_t, bif_b, conv_w[l], conv_b[l].reshape(1, -1),
                   mlstm_norm_g[l].reshape(1, ML_V_W), norm2_g[l].reshape(1, D),
                   w_att_out[l].astype(BF16), w_ml_out[l].astype(BF16), w_out[l].astype(BF16),
                   w_ff1[l].astype(BF16), w_ff2[l].astype(BF16), S)
    return x2.reshape(B, S, D)
```

```python
import functools
import math

import numpy as np
import jax
import jax.numpy as jnp
from jax import lax
from jax.experimental import pallas as pl
from jax.experimental.pallas import tpu as pltpu

F32 = jnp.float32
BF16 = jnp.bfloat16

D_MODEL = 1024
ATT_GROUPS = ((128, 1), (512, 4), (2048, 16))
N_ATT_GROUPS = 3
ATT_HPG = 4
ATT_DH = 128
N_ATT_HEADS = 12
ATT_W = 512
ATT_BLK = 128
ML_HEADS = 8
ML_DK = 64
ML_DV = 128
ML_QK_W = 512
ML_V_W = 1024
CONV_W = 4
D_FF = 4096
N_BUCKETS = 32
MAX_DISTANCE = 2048
EPS = 1e-6
NEG = -1e30

LANE = 128
ML_L = 128

AW = 3 * N_ATT_HEADS * ATT_DH
PW = 5120
CB_G, CB_MV, CB_MO = 0, 16, 24
CB_MQ, CB_MK = 32, 36
WIN_ATT, WIN_MQK, WIN_MVO, WIN_IF, WIN_G = (0, 4608), (4608, 5632), (5632, 7680), (7680, 7696), (7696, 9744)

VMEM_LIMIT = 56 * 1024 * 1024


def _dot(a, b):
    return jnp.dot(a, b, preferred_element_type=F32)


def _dot_nt(a, b):
    return lax.dot_general(a, b, (((1,), (1,)), ((), ())), preferred_element_type=F32)


def _split3(a):
    hi = a.astype(BF16)
    r1 = a - hi.astype(F32)
    mid = r1.astype(BF16)
    lo = (r1 - mid.astype(F32)).astype(BF16)
    return hi, mid, lo


def _dot3(a, rhs_bf16):
    hi, mid, lo = _split3(a)
    return _dot(hi, rhs_bf16) + _dot(mid, rhs_bf16) + _dot(lo, rhs_bf16)


def _resident(shape):
    return pl.BlockSpec(shape, lambda *_: (0,) * len(shape), pipeline_mode=pl.Buffered(1))


def _ada_kernel(c_ref, w_ref, b_ref, o_ref):
    c = c_ref[...]
    s = c * jax.nn.sigmoid(c)
    o_ref[...] = _dot(s.astype(BF16), w_ref[...].astype(BF16)) + b_ref[...]


def _ada(c, w, b):
    B = c.shape[0]
    n = w.shape[1]
    tn = 1024
    return pl.pallas_call(
        _ada_kernel,
        grid=(n // tn,),
        in_specs=[pl.BlockSpec((B, D_MODEL), lambda j: (0, 0)),
                  pl.BlockSpec((D_MODEL, tn), lambda j: (0, j)),
                  pl.BlockSpec((1, tn), lambda j: (0, j))],
        out_specs=pl.BlockSpec((B, tn), lambda j: (0, j)),
        out_shape=jax.ShapeDtypeStruct((B, n), F32),
        name="ada",
    )(c, w, b.reshape(1, n))


def _t5_bucket_np(dist):
    max_exact = N_BUCKETS // 2
    d = np.maximum(dist, max_exact).astype(np.float32)
    large = max_exact + (np.log(d / np.float32(max_exact)) / np.float32(math.log(MAX_DISTANCE / max_exact))
                         * np.float32(N_BUCKETS - max_exact)).astype(np.int32)
    large = np.minimum(large, N_BUCKETS - 1)
    return np.where(dist < max_exact, dist, large).astype(np.int32)


def _bucket_tiles():
    i = np.arange(ATT_BLK)[:, None]
    j = np.arange(2 * ATT_BLK)[None, :]
    delta = ATT_BLK + i - j
    return np.stack([_t5_bucket_np(np.maximum(delta, 0) * dil) for _, dil in ATT_GROUPS])


def _bias_kernel(tab_ref, bucket_ref, o_ref):
    hh = pl.program_id(0)
    bucket = bucket_ref[...]
    acc = jnp.zeros(bucket.shape, F32)
    for k in range(N_BUCKETS):
        acc = jnp.where(bucket == k, tab_ref[k, hh], acc)
    i = lax.broadcasted_iota(jnp.int32, bucket.shape, 0)
    j = lax.broadcasted_iota(jnp.int32, bucket.shape, 1)
    delta = ATT_BLK + i - j
    valid = (delta >= 0) & (delta <= ATT_BLK)
    o_ref[0] = jnp.where(valid, acc, NEG)
    o_ref[1] = jnp.where(valid & (j >= ATT_BLK), acc, NEG)


def _bias_tiles(rel_bias):
    buckets = jnp.asarray(_bucket_tiles())
    return pl.pallas_call(
        _bias_kernel,
        grid=(N_ATT_HEADS,),
        in_specs=[pl.BlockSpec(memory_space=pltpu.SMEM),
                  pl.BlockSpec((None, ATT_BLK, 2 * ATT_BLK), lambda h: (h // ATT_HPG, 0, 0))],
        out_specs=pl.BlockSpec((None, 2, ATT_BLK, 2 * ATT_BLK), lambda h: (h, 0, 0, 0)),
        out_shape=jax.ShapeDtypeStruct((N_ATT_HEADS, 2, ATT_BLK, 2 * ATT_BLK), F32),
        name="bias_tiles",
    )(rel_bias, buckets)


IP_TM = 512
IP_TN = 512
IP_NQ = N_ATT_HEADS * ATT_DH // IP_TN


def _inproj_att_kernel(x_ref, ada_ref, g1_ref, w_ref, qg_ref, kg_ref, u_ref, o_ref):
    x = x_ref[...]
    ms = jnp.mean(x * x, axis=-1, keepdims=True)
    y = x * lax.rsqrt(ms + EPS) * g1_ref[...]
    ub = (y * (1.0 + ada_ref[1:2, :]) + ada_ref[0:1, :]).astype(BF16)
    u_ref[...] = ub
    for j in range(AW // IP_TN):
        acc = _dot(ub, w_ref[:, j * IP_TN:(j + 1) * IP_TN])
        if j < 2 * IP_NQ:
            gain = qg_ref[...] if j < IP_NQ else kg_ref[...]
            for k in range(IP_TN // ATT_DH):
                a = acc[:, k * ATT_DH:(k + 1) * ATT_DH]
                ms = jnp.mean(a * a, axis=-1, keepdims=True)
                lo = j * IP_TN + k * ATT_DH
                o_ref[:, lo:lo + ATT_DH] = (a * lax.rsqrt(ms + EPS) * gain).astype(o_ref.dtype)
        else:
            o_ref[:, j * IP_TN:(j + 1) * IP_TN] = acc.astype(o_ref.dtype)


def _inproj_att(x2, ada3, g1, w_att, qg, kg, S):
    T = x2.shape[0]
    tm = IP_TM
    tiles_per_seq = S // tm
    return pl.pallas_call(
        _inproj_att_kernel,
        grid=(T // tm,),
        in_specs=[pl.BlockSpec((tm, D_MODEL), lambda i: (i, 0)),
                  pl.BlockSpec((None, 6, D_MODEL), lambda i: (i // tiles_per_seq, 0, 0)),
                  _resident((1, D_MODEL)),
                  _resident((D_MODEL, AW)),
                  _resident((1, ATT_DH)),
                  _resident((1, ATT_DH))],
        out_specs=[pl.BlockSpec((tm, D_MODEL), lambda i: (i, 0)),
                   pl.BlockSpec((tm, AW), lambda i: (i, 0))],
        out_shape=[jax.ShapeDtypeStruct((T, D_MODEL), BF16),
                   jax.ShapeDtypeStruct((T, AW), BF16)],
        compiler_params=pltpu.CompilerParams(
            dimension_semantics=("parallel",), vmem_limit_bytes=VMEM_LIMIT),
        name="inproj_att",
    )(x2, ada3, g1, w_att, qg, kg)


ATT_UNROLL = 8


def _mix_kernel(u_ref, wg_ref, *rest, S):
    n_wb = len(MIX_WBLOCKS)
    _mix_body(u_ref, wg_ref, rest[:n_wb], *rest[n_wb:], S=S)


def _mix_body(u_ref, wg_ref, wb_refs, wif_ref,
                q0_ref, k0_ref, v0_ref, q1_ref, k1_ref, v1_ref, q2_ref, k2_ref, v2_ref, bias_ref,
                p_ref, gt_ref, o_ref,
                q1f, k1f, v1f, q2f, k2f, v2f, o_scr, l_scr, o2d, l2d, *, S):
    scale = ATT_DH ** -0.5
    blk = ATT_BLK

    ub = u_ref[...]

    n_g = (WIN_G[1] - WIN_G[0]) // IP_TN

    def proj_chunk(j):
        w = wg_ref[:, j * IP_TN:(j + 1) * IP_TN] if j < n_g else wb_refs[j - n_g][...]
        p_ref[:, j * IP_TN:(j + 1) * IP_TN] = _dot(ub, w).astype(p_ref.dtype)
    pending = [functools.partial(proj_chunk, j) for j in range(PW // IP_TN)]

    def emit_proj(n):
        for _ in range(min(n, len(pending))):
            pending.pop(0)()

    plan = [(1, 0), (1, 1), (1, 0), (1, 1), (1, 1), (1, 1)]

    gt_ref[...] = _dot_nt(wif_ref[...], ub)
    emit_proj(1)

    pad1 = blk * ATT_GROUPS[1][1]
    k1f[0:pad1, :] = jnp.zeros((pad1, LANE), F32)
    v1f[0:pad1, :] = jnp.zeros((pad1, LANE), F32)
    k1f[pad1:pad1 + S, :] = k1_ref[...].astype(F32)
    v1f[pad1:pad1 + S, :] = v1_ref[...].astype(F32)
    rs = ATT_GROUPS[1][1]
    for src, dst in ((q2_ref, q2f), (k2_ref, k2f), (v2_ref, v2f)):
        q1f[...] = src[...].astype(F32)
        for a in range(rs):
            dst[a] = q1f[pl.ds(a, S // rs, stride=rs), :]
    q1f[...] = q1_ref[...].astype(F32)

    def softmax_blocks(ops):
        n_mid, n_end = plan.pop(0)
        ss = [_dot_nt(q, kk) * scale + bias for (q, kk, _, bias) in ops]
        emit_proj(n_mid)
        ms = [jnp.max(s, axis=-1, keepdims=True) for s in ss]
        ps = [jnp.exp(s - m) for s, m in zip(ss, ms)]
        exts = [_dot(p.astype(BF16), jnp.concatenate([op[2], jnp.ones(op[2].shape, BF16)], axis=1))
                for p, op in zip(ps, ops)]
        emit_proj(n_end)
        return [(e[:, 0:LANE] / e[:, LANE:2 * LANE], m + jnp.log(e[:, LANE:2 * LANE])) for e, m in zip(exts, ms)]

    def run_dilated(n_batches, fetch, store):
        for it in range(n_batches):
            fetched = [fetch(it, u) for u in range(ATT_UNROLL)]
            res = softmax_blocks([f[:4] for f in fetched])
            for (o, lse), f in zip(res, fetched):
                store(f[4], o, lse)

    r1 = ATT_GROUPS[1][1]
    nb1 = S // r1 // blk
    assert ATT_UNROLL % nb1 == 0

    def fetch1(it, u):
        rho = it * (ATT_UNROLL // nb1) + u // nb1
        n = u % nb1
        start = rho + r1 * blk * n
        return (q1f[pl.ds(start, blk, stride=r1), :].astype(BF16),
                k1f[pl.ds(start, 2 * blk, stride=r1), :].astype(BF16),
                v1f[pl.ds(start, 2 * blk, stride=r1), :].astype(BF16),
                bias_ref[1, 1 if n == 0 else 0], start)

    def store1(start, o, lse):
        rows = pl.ds(start, blk, stride=r1)
        o_scr[0, rows, :] = o
        l_scr[0, rows, :] = lse
    run_dilated(r1 * nb1 // ATT_UNROLL, fetch1, store1)

    r2 = ATT_GROUPS[2][1]
    assert S // r2 == blk

    assert r2 == rs * rs

    def fetch2(it, u):
        rho = it * ATT_UNROLL + u
        rows = pl.ds(rho // rs, blk, stride=rs)
        return (q2f[rho % rs, rows, :].astype(BF16), k2f[rho % rs, rows, :].astype(BF16),
                v2f[rho % rs, rows, :].astype(BF16), bias_ref[2, 1][:, blk:2 * blk], rho)

    def store2(rho, o, lse):
        rows = pl.ds(rho // rs, blk, stride=rs)
        o2d[rho % rs, rows, :] = o
        l2d[rho % rs, rows, :] = lse
    run_dilated(r2 // ATT_UNROLL, fetch2, store2)
    for a in range(rs):
        o_scr[1, pl.ds(a, S // rs, stride=rs), :] = o2d[a]
        l_scr[1, pl.ds(a, S // rs, stride=rs), :] = l2d[a]

    for it in range(S // blk // ATT_UNROLL):
        ops = []
        for u in range(ATT_UNROLL):
            n = it * ATT_UNROLL + u
            q = q0_ref[n * blk:(n + 1) * blk, :]
            if n == 0:
                ops.append((q, k0_ref[0:blk, :], v0_ref[0:blk, :], bias_ref[0, 1][:, blk:2 * blk]))
            else:
                ops.append((q, k0_ref[(n - 1) * blk:(n + 1) * blk, :], v0_ref[(n - 1) * blk:(n + 1) * blk, :],
                            bias_ref[0, 0]))
        for u, (o0, lse0) in enumerate(softmax_blocks(ops)):
            n = it * ATT_UNROLL + u
            rows = slice(n * blk, (n + 1) * blk)
            l1, l2 = l_scr[0, rows, :], l_scr[1, rows, :]
            mx = jnp.maximum(jnp.maximum(lse0, l1), l2)
            e0, e1, e2 = jnp.exp(lse0 - mx), jnp.exp(l1 - mx), jnp.exp(l2 - mx)
            att = (e0 * o0 + e1 * o_scr[0, rows, :] + e2 * o_scr[1, rows, :]) / (e0 + e1 + e2)
            o_ref[rows, :] = att.astype(o_ref.dtype)
    emit_proj(len(pending))


MIX_WBLOCKS = tuple(range(WIN_MVO[0] // IP_TN, WIN_MVO[1] // IP_TN)) + tuple(
    range(WIN_MQK[0] // IP_TN, WIN_MQK[1] // IP_TN))


def _mix(u2d, w_gates, w_in_b, w_if_t, qkv3, bias_tiles, B, S):
    T = u2d.shape[0]
    tm = IP_TM
    tps = S // tm
    assert tps == ATT_HPG
    hpp = N_ATT_HEADS
    in_specs = [pl.BlockSpec((tm, D_MODEL), lambda b, j: (b * tps + j, 0)),
                _resident((D_MODEL, WIN_G[1] - WIN_G[0]))]
    in_specs += [pl.BlockSpec((D_MODEL, IP_TN), lambda b, j, cb=cb: (0, cb), pipeline_mode=pl.Buffered(1))
                 for cb in MIX_WBLOCKS]
    in_specs.append(_resident((16, D_MODEL)))
    for g in range(N_ATT_GROUPS):
        for part in range(3):
            in_specs.append(pl.BlockSpec((None, S, LANE),
                                         lambda b, j, cb=part * hpp + g * ATT_HPG: (b, 0, cb + j)))
    in_specs.append(pl.BlockSpec((N_ATT_GROUPS, None, 2, ATT_BLK, 2 * ATT_BLK), lambda b, j: (0, j, 0, 0, 0)))
    pad1 = ATT_BLK * ATT_GROUPS[1][1]
    slab = pltpu.VMEM((ATT_GROUPS[1][1], S // ATT_GROUPS[1][1], LANE), F32)
    scratch = [pltpu.VMEM((S, LANE), F32), pltpu.VMEM((pad1 + S, LANE), F32), pltpu.VMEM((pad1 + S, LANE), F32),
               slab, slab, slab,
               pltpu.VMEM((N_ATT_GROUPS - 1, S, LANE), F32), pltpu.VMEM((N_ATT_GROUPS - 1, S, LANE), F32),
               slab, slab]
    bias5 = bias_tiles.reshape(N_ATT_GROUPS, ATT_HPG, 2, ATT_BLK, 2 * ATT_BLK)
    return pl.pallas_call(
        functools.partial(_mix_kernel, S=S),
        grid=(B, tps),
        in_specs=in_specs,
        out_specs=[pl.BlockSpec((tm, PW), lambda b, j: (b * tps + j, 0)),
                   pl.BlockSpec((None, 16, tm), lambda b, j: (b, 0, j)),
                   pl.BlockSpec((None, S, LANE), lambda b, j: (b, 0, j))],
        out_shape=[jax.ShapeDtypeStruct((T, PW), BF16),
                   jax.ShapeDtypeStruct((B, 16, S), F32),
                   jax.ShapeDtypeStruct((B, S, ATT_W), BF16)],
        scratch_shapes=scratch,
        compiler_params=pltpu.CompilerParams(
            dimension_semantics=("parallel", "parallel"), vmem_limit_bytes=VMEM_LIMIT),
        name="mix",
    )(u2d, w_gates, *([w_in_b] * len(MIX_WBLOCKS)), w_if_t, *([qkv3] * 9), bias5)


TL_TM = 512
TL_TK = 512
ML_PAIR = 2
ML_NPAIR = ML_HEADS // ML_PAIR
ML_CPT = TL_TM // ML_L
ML_NR = ML_CPT * ML_HEADS
XS_HDR = 8


def _tail_kernel(x_ref, ada_ref, att_ref, ga_ref, gm_ref, g2_ref, wa_ref, wm_ref, wo_ref, w1_ref, w2_ref,
                 mq_ref, mk_ref, mv_ref, mo_ref, gt_ref, bif_ref, cw_ref, cb_ref, ng_ref,
                 o_ref,
                 hml_scr, xq, xk, qa, kab, kaf, cst_scr, mch_scr, rf, *, n_tiles, tiles_per_seq):
    i = pl.program_id(0)
    L = ML_L
    tm = TL_TM
    im = jnp.minimum(i, n_tiles - 1)

    @pl.when(i == 0)
    def _():
        hml_scr[...] = jnp.zeros(hml_scr.shape, hml_scr.dtype)

    @pl.when(im % tiles_per_seq == 0)
    def _():
        cst_scr[...] = jnp.zeros(cst_scr.shape, F32)
        mch_scr[...] = jnp.zeros(mch_scr.shape, F32)
        xq[tm:tm + XS_HDR, :] = jnp.zeros((XS_HDR, ML_QK_W), F32)
        xk[tm:tm + XS_HDR, :] = jnp.zeros((XS_HDR, ML_QK_W), F32)


    def merge_stage():
        y_att = _dot(att_ref[...], wa_ref[...])
        y_ml = _dot(hml_scr[...], wm_ref[...])
        ga = jax.nn.sigmoid(ga_ref[...].astype(F32))
        gm = jax.nn.sigmoid(gm_ref[...].astype(F32))
        return (ga * y_att + gm * y_ml).astype(BF16)

    def out_proj_stage(mix):
        x1 = x_ref[...] + ada_ref[2:3, :] * _dot(mix, wo_ref[...])
        ms = jnp.mean(x1 * x1, axis=-1, keepdims=True)
        y = x1 * lax.rsqrt(ms + EPS) * g2_ref[...]
        return x1, (y * (1.0 + ada_ref[4:5, :]) + ada_ref[3:4, :]).astype(BF16)

    def mlp_up(u2, k):
        hdn = jnp.maximum(_dot(u2, w1_ref[:, k * TL_TK:(k + 1) * TL_TK]), 0.0)
        return (hdn * hdn).astype(BF16)

    def mlp_down(hsq, acc, k):
        part = _dot(hsq, w2_ref[k * TL_TK:(k + 1) * TL_TK, :])
        return part if acc is None else acc + part

    lane = lax.broadcasted_iota(jnp.int32, (1, LANE), 1)
    srow = lax.broadcasted_iota(jnp.int32, (LANE, 1), 0)
    in_head_lane = [(lane >= hh * ML_DK) & (lane < (hh + 1) * ML_DK) for hh in range(ML_PAIR)]
    in_head_row = [(srow >= hh * ML_DK) & (srow < (hh + 1) * ML_DK) for hh in range(ML_PAIR)]
    head0_rows = in_head_row[0]

    def conv_silu(src_ref, xs, c0):
        xs[0:XS_HDR, :] = xs[tm:tm + XS_HDR, :]
        xs[XS_HDR:XS_HDR + tm, :] = src_ref[...].astype(F32)
        yv = cb_ref[:, c0:c0 + ML_QK_W]
        for jj in range(CONV_W):
            yv = yv + cw_ref[jj:jj + 1, c0:c0 + ML_QK_W] * xs[pl.ds(XS_HDR - (CONV_W - 1) + jj, tm), :]
        return yv * jax.nn.sigmoid(yv)

    def conv_qk():
        yq = conv_silu(mq_ref, xq, 0)
        for p in range(ML_NPAIR):
            for hh in range(ML_PAIR):
                qa[hh, :, p * LANE:(p + 1) * LANE] = jnp.where(in_head_lane[hh], yq[:, p * LANE:(p + 1) * LANE],
                                                               0.0).astype(BF16)
        yk = conv_silu(mk_ref, xk, ML_QK_W) * (ML_DK ** -0.5)
        kaf[...] = yk
        kab[...] = yk.astype(BF16)

    r_i = lax.broadcasted_iota(jnp.int32, (L, L), 0)
    c_i = lax.broadcasted_iota(jnp.int32, (L, L), 1)
    ones_m = jnp.ones((L, L), BF16)
    incl_upper = (r_i <= c_i).astype(BF16)
    causal = c_i <= r_i
    lane_nr = lax.broadcasted_iota(jnp.int32, (ML_NR, L), 1)
    pad_rows = jnp.zeros((L - ML_NR, L), F32)

    def col_form(row_form):
        return jnp.concatenate([row_form, pad_rows], axis=0).T

    li = jnp.concatenate([gt_ref[0:ML_HEADS, c * L:(c + 1) * L] + bif_ref[0:ML_HEADS, :]
                          for c in range(ML_CPT)], axis=0)
    zf = jnp.concatenate([gt_ref[ML_HEADS:2 * ML_HEADS, c * L:(c + 1) * L] + bif_ref[ML_HEADS:2 * ML_HEADS, :]
                          for c in range(ML_CPT)], axis=0)
    lf = jnp.minimum(zf, 0.0) - jnp.log1p(jnp.exp(-jnp.abs(zf)))
    brow = _dot3(lf, incl_upper)
    bend = _dot3(lf, ones_m)
    u = brow - li
    g = bend - u
    maxg = jnp.max(g, axis=-1, keepdims=True)
    m = mch_scr[...]
    for c in range(ML_CPT):
        rs = slice(c * ML_HEADS, (c + 1) * ML_HEADS)
        rf[1, rs, :] = m
        m = jnp.maximum(bend[rs, :] + m, maxg[rs, :])
        rf[2, rs, :] = m
    mch_scr[...] = m
    m_cur = rf[1]
    m_nxt = rf[2]
    rf[0] = u
    rf[3] = jnp.exp(bend + m_cur - m_nxt)
    rf[4] = jnp.exp(g - m_nxt)
    pm = -u
    sh = 1
    while sh < L:
        pm = jnp.maximum(pm, jnp.where(lane_nr >= sh, pltpu.roll(pm, sh, axis=1), NEG))
        sh *= 2
    d1 = -jnp.maximum(m_cur, pm)
    dcol = col_form(d1)
    ecol = col_form(jnp.exp(d1 - brow))

    mix = merge_stage()
    conv_qk()
    x1, u2 = out_proj_stage(mix)

    ones_v = jnp.ones((L, LANE), BF16)

    def prepare(c, p):
        rows = slice(c * L, (c + 1) * L)
        pl_ = slice(p * LANE, (p + 1) * LANE)
        r0 = c * ML_HEADS + p * ML_PAIR
        kc = kab[rows, pl_]
        kt = kaf[rows, pl_].T
        wk = jnp.where(head0_rows, rf[4, r0:r0 + 1, :], rf[4, r0 + 1:r0 + 2, :])
        ktw = (kt * wk).astype(BF16)
        ktw2 = jnp.concatenate([jnp.where(in_head_row[hh], ktw, jnp.zeros_like(ktw)) for hh in range(ML_PAIR)],
                               axis=1)
        vexts = [jnp.concatenate([mv_ref[rows, (r0 % ML_HEADS + hh) * LANE:(r0 % ML_HEADS + hh + 1) * LANE],
                                  ones_v], axis=1) for hh in range(ML_PAIR)]
        z = _dot(ktw2, jnp.concatenate(vexts, axis=0))
        ss = [_dot_nt(qa[hh, rows, pl_], kc) for hh in range(ML_PAIR)]
        drow = [jnp.broadcast_to(dcol[:, r0 + hh:r0 + hh + 1], (L, L)) for hh in range(ML_PAIR)]
        erow = [jnp.broadcast_to(ecol[:, r0 + hh:r0 + hh + 1], (L, L)) for hh in range(ML_PAIR)]
        return ss, z, vexts, drow, erow

    order = [(c, p) for c in range(ML_CPT) for p in range(ML_NPAIR)]
    assert 2 * (D_FF // TL_TK) == len(order)
    cst = [cst_scr[p] for p in range(ML_NPAIR)]
    nxt = prepare(*order[0])
    acc = None
    hsq = None
    for n, (c, p) in enumerate(order):
        rows = slice(c * L, (c + 1) * L)
        pl_ = slice(p * LANE, (p + 1) * LANE)
        r0 = c * ML_HEADS + p * ML_PAIR
        ss, z, vexts, drow, erow = nxt
        if n + 1 < len(order):
            nxt = prepare(*order[n + 1])
        cst_b = cst[p].astype(BF16)
        lhs = []
        for hh in range(ML_PAIR):
            r = r0 + hh
            w = ss[hh] * jnp.exp(jnp.where(causal, drow[hh] - rf[0, r:r + 1, :], NEG))
            qi = qa[hh, rows, pl_].astype(F32) * jnp.exp(drow[hh] + rf[1, r:r + 1, :])
            lhs.append(jnp.concatenate([w.astype(BF16), qi.astype(BF16)], axis=1))
        if n % 2 == 0:
            hsq = mlp_up(u2, n // 2)
        else:
            acc = mlp_down(hsq, acc, n // 2)
        for hh in range(ML_PAIR):
            cols = slice((p * ML_PAIR + hh) * LANE, (p * ML_PAIR + hh + 1) * LANE)
            tot = _dot(lhs[hh], jnp.concatenate([vexts[hh], cst_b], axis=0))
            hval = tot[:, 0:LANE] / jnp.maximum(jnp.abs(tot[:, LANE:2 * LANE]), erow[hh])
            msq = jnp.mean(hval * hval, axis=-1, keepdims=True)
            hn = hval * lax.rsqrt(msq + EPS) * ng_ref[:, cols]
            hml_scr[rows, cols] = (hn * jax.nn.sigmoid(mo_ref[rows, cols].astype(F32))).astype(hml_scr.dtype)
        dec = jnp.where(head0_rows, rf[3, r0:r0 + 1, :], rf[3, r0 + 1:r0 + 2, :])
        cst[p] = jnp.concatenate([dec, dec], axis=1) * cst[p] + z
    for p in range(ML_NPAIR):
        cst_scr[p] = cst[p]
    o_ref[...] = x1 + ada_ref[5:6, :] * acc


def _tail(x2, ada3, att2, proj2, gates_t, bif_b, conv_w, conv_b, norm_g, g2, wa, wm, wo, w1, w2, S):
    T = x2.shape[0]
    tm = TL_TM
    tps = S // tm
    n_tiles = T // tm
    cur = lambda i: jnp.minimum(i, n_tiles - 1)
    prv = lambda i: jnp.maximum(i - 1, 0)
    gcb = CB_G * LANE // D_MODEL
    in_specs = [
        pl.BlockSpec((tm, D_MODEL), lambda i: (prv(i), 0)),
        pl.BlockSpec((None, 6, D_MODEL), lambda i: (prv(i) // tps, 0, 0)),
        pl.BlockSpec((tm, ATT_W), lambda i: (prv(i), 0)),
        pl.BlockSpec((tm, D_MODEL), lambda i: (prv(i), gcb)),
        pl.BlockSpec((tm, D_MODEL), lambda i: (prv(i), gcb + 1)),
        _resident((1, D_MODEL)),
        _resident((ATT_W, D_MODEL)), _resident((ML_V_W, D_MODEL)), _resident((D_MODEL, D_MODEL)),
        _resident((D_MODEL, D_FF)), _resident((D_FF, D_MODEL)),
        pl.BlockSpec((tm, ML_QK_W), lambda i: (cur(i), CB_MQ * LANE // ML_QK_W)),
        pl.BlockSpec((tm, ML_QK_W), lambda i: (cur(i), CB_MK * LANE // ML_QK_W)),
        pl.BlockSpec((tm, ML_V_W), lambda i: (cur(i), CB_MV * LANE // ML_V_W)),
        pl.BlockSpec((tm, ML_V_W), lambda i: (cur(i), CB_MO * LANE // ML_V_W)),
        pl.BlockSpec((None, 2 * ML_HEADS, tm), lambda i: (cur(i) // tps, 0, cur(i) % tps)),
        _resident((2 * ML_HEADS, LANE)),
        _resident((CONV_W, 2 * ML_QK_W)), _resident((1, 2 * ML_QK_W)), _resident((1, ML_V_W)),
    ]
    scratch = [pltpu.VMEM((tm, ML_V_W), BF16),
               pltpu.VMEM((tm + XS_HDR, ML_QK_W), F32), pltpu.VMEM((tm + XS_HDR, ML_QK_W), F32),
               pltpu.VMEM((ML_PAIR, tm, ML_QK_W), BF16), pltpu.VMEM((tm, ML_QK_W), BF16),
               pltpu.VMEM((tm, ML_QK_W), F32),
               pltpu.VMEM((ML_NPAIR, LANE, 2 * LANE), F32), pltpu.VMEM((ML_HEADS, LANE), F32),
               pltpu.VMEM((5, ML_NR, ML_L), F32)]
    return pl.pallas_call(
        functools.partial(_tail_kernel, n_tiles=n_tiles, tiles_per_seq=tps),
        grid=(n_tiles + 1,),
        in_specs=in_specs,
        out_specs=pl.BlockSpec((tm, D_MODEL), lambda i: (prv(i), 0)),
        out_shape=jax.ShapeDtypeStruct((T, D_MODEL), F32),
        scratch_shapes=scratch,
        compiler_params=pltpu.CompilerParams(
            dimension_semantics=("arbitrary",), vmem_limit_bytes=VMEM_LIMIT),
        name="tail",
    )(x2, ada3, att2, proj2, proj2, g2, wa, wm, wo, w1, w2,
      proj2, proj2, proj2, proj2, gates_t, bif_b, conv_w, conv_b, norm_g)


def kernel(x, c, w_ada, b_ada, norm1_g, norm2_g, w_in, b_if, conv_w, conv_b, q_norm_g, k_norm_g,
           rel_bias, mlstm_norm_g, w_att_out, w_ml_out, w_out, w_ff1, w_ff2):
    B, S, D = x.shape
    T = B * S
    depth = w_ada.shape[0]
    bias_tiles = _bias_tiles(rel_bias)
    x2 = x.reshape(T, D)
    for l in range(depth):
        ada3 = _ada(c, w_ada[l], b_ada[l]).reshape(B, 6, D)
        w_in_b = w_in[l][:, :WIN_IF[0]].astype(BF16)
        w_gates = w_in[l][:, WIN_G[0]:WIN_G[1]].astype(BF16)
        w_if_t = w_in[l][:, WIN_IF[0]:WIN_IF[1]].T.astype(BF16)
        u2d, qkv = _inproj_att(x2, ada3, norm1_g[l].reshape(1, D), w_in_b,
                               q_norm_g[l].reshape(1, ATT_DH), k_norm_g[l].reshape(1, ATT_DH), S)
        proj2, gates_t, att = _mix(u2d, w_gates, w_in_b, w_if_t, qkv.reshape(B, S, AW), bias_tiles, B, S)
        bif_b = jnp.broadcast_to(b_if[l].reshape(2 * ML_HEADS, 1), (2 * ML_HEADS, LANE))
        x2 = _tail(x2, ada3, att.reshape(T, ATT_W), proj2, gates_t, bif_b, conv_w[l], conv_b[l].reshape(1, -1),
                   mlstm_norm_g[l].reshape(1, ML_V_W), norm2_g[l].reshape(1, D),
                   w_att_out[l].astype(BF16), w_ml_out[l].astype(BF16), w_out[l].astype(BF16),
                   w_ff1[l].astype(BF16), w_ff2[l].astype(BF16), S)
    return x2.reshape(B, S, D)
```

```python
import functools
import math

import numpy as np
import jax
import jax.numpy as jnp
from jax import lax
from jax.experimental import pallas as pl
from jax.experimental.pallas import tpu as pltpu

F32 = jnp.float32
BF16 = jnp.bfloat16

D_MODEL = 1024
ATT_GROUPS = ((128, 1), (512, 4), (2048, 16))
N_ATT_GROUPS = 3
ATT_HPG = 4
ATT_DH = 128
N_ATT_HEADS = 12
ATT_W = 512
ATT_BLK = 128
ML_HEADS = 8
ML_DK = 64
ML_DV = 128
ML_QK_W = 512
ML_V_W = 1024
CONV_W = 4
D_FF = 4096
N_BUCKETS = 32
MAX_DISTANCE = 2048
EPS = 1e-6
NEG = -1e30

LANE = 128
ML_L = 128

AW = 3 * N_ATT_HEADS * ATT_DH
PW = 5120
CB_G, CB_MV, CB_MO = 0, 16, 24
CB_MQ, CB_MK = 32, 36
WIN_ATT, WIN_MQK, WIN_MVO, WIN_IF, WIN_G = (0, 4608), (4608, 5632), (5632, 7680), (7680, 7696), (7696, 9744)

VMEM_LIMIT = 56 * 1024 * 1024


def _dot(a, b):
    return jnp.dot(a, b, preferred_element_type=F32)


def _dot_nt(a, b):
    return lax.dot_general(a, b, (((1,), (1,)), ((), ())), preferred_element_type=F32)


def _split3(a):
    hi = a.astype(BF16)
    r1 = a - hi.astype(F32)
    mid = r1.astype(BF16)
    lo = (r1 - mid.astype(F32)).astype(BF16)
    return hi, mid, lo


def _dot3(a, rhs_bf16):
    hi, mid, lo = _split3(a)
    return _dot(hi, rhs_bf16) + _dot(mid, rhs_bf16) + _dot(lo, rhs_bf16)


def _resident(shape):
    return pl.BlockSpec(shape, lambda *_: (0,) * len(shape), pipeline_mode=pl.Buffered(1))


def _ada_kernel(c_ref, w_ref, b_ref, o_ref):
    c = c_ref[...]
    s = c * jax.nn.sigmoid(c)
    o_ref[...] = _dot(s.astype(BF16), w_ref[...].astype(BF16)) + b_ref[...]


def _ada(c, w, b):
    B = c.shape[0]
    n = w.shape[1]
    tn = 1024
    return pl.pallas_call(
        _ada_kernel,
        grid=(n // tn,),
        in_specs=[pl.BlockSpec((B, D_MODEL), lambda j: (0, 0)),
                  pl.BlockSpec((D_MODEL, tn), lambda j: (0, j)),
                  pl.BlockSpec((1, tn), lambda j: (0, j))],
        out_specs=pl.BlockSpec((B, tn), lambda j: (0, j)),
        out_shape=jax.ShapeDtypeStruct((B, n), F32),
        name="ada",
    )(c, w, b.reshape(1, n))


def _t5_bucket_np(dist):
    max_exact = N_BUCKETS // 2
    d = np.maximum(dist, max_exact).astype(np.float32)
    large = max_exact + (np.log(d / np.float32(max_exact)) / np.float32(math.log(MAX_DISTANCE / max_exact))
                         * np.float32(N_BUCKETS - max_exact)).astype(np.int32)
    large = np.minimum(large, N_BUCKETS - 1)
    return np.where(dist < max_exact, dist, large).astype(np.int32)


def _bucket_tiles():
    i = np.arange(ATT_BLK)[:, None]
    j = np.arange(2 * ATT_BLK)[None, :]
    delta = ATT_BLK + i - j
    return np.stack([_t5_bucket_np(np.maximum(delta, 0) * dil) for _, dil in ATT_GROUPS])


def _bias_kernel(tab_ref, bucket_ref, o_ref):
    hh = pl.program_id(0)
    bucket = bucket_ref[...]
    acc = jnp.zeros(bucket.shape, F32)
    for k in range(N_BUCKETS):
        acc = jnp.where(bucket == k, tab_ref[k, hh], acc)
    i = lax.broadcasted_iota(jnp.int32, bucket.shape, 0)
    j = lax.broadcasted_iota(jnp.int32, bucket.shape, 1)
    delta = ATT_BLK + i - j
    valid = (delta >= 0) & (delta <= ATT_BLK)
    o_ref[0] = jnp.where(valid, acc, NEG)
    o_ref[1] = jnp.where(valid & (j >= ATT_BLK), acc, NEG)


def _bias_tiles(rel_bias):
    buckets = jnp.asarray(_bucket_tiles())
    return pl.pallas_call(
        _bias_kernel,
        grid=(N_ATT_HEADS,),
        in_specs=[pl.BlockSpec(memory_space=pltpu.SMEM),
                  pl.BlockSpec((None, ATT_BLK, 2 * ATT_BLK), lambda h: (h // ATT_HPG, 0, 0))],
        out_specs=pl.BlockSpec((None, 2, ATT_BLK, 2 * ATT_BLK), lambda h: (h, 0, 0, 0)),
        out_shape=jax.ShapeDtypeStruct((N_ATT_HEADS, 2, ATT_BLK, 2 * ATT_BLK), F32),
        name="bias_tiles",
    )(rel_bias, buckets)


IP_TM = 512
IP_TN = 512
IP_NQ = N_ATT_HEADS * ATT_DH // IP_TN


def _inproj_att_kernel(x_ref, ada_ref, g1_ref, w_ref, qg_ref, kg_ref, u_ref, o_ref):
    x = x_ref[...]
    ms = jnp.mean(x * x, axis=-1, keepdims=True)
    y = x * lax.rsqrt(ms + EPS) * g1_ref[...]
    ub = (y * (1.0 + ada_ref[1:2, :]) + ada_ref[0:1, :]).astype(BF16)
    u_ref[...] = ub
    for j in range(AW // IP_TN):
        acc = _dot(ub, w_ref[:, j * IP_TN:(j + 1) * IP_TN])
        if j < 2 * IP_NQ:
            gain = qg_ref[...] if j < IP_NQ else kg_ref[...]
            for k in range(IP_TN // ATT_DH):
                a = acc[:, k * ATT_DH:(k + 1) * ATT_DH]
                ms = jnp.mean(a * a, axis=-1, keepdims=True)
                lo = j * IP_TN + k * ATT_DH
                o_ref[lo // ATT_DH] = (a * lax.rsqrt(ms + EPS) * gain).astype(o_ref.dtype)
        else:
            for k in range(IP_TN // ATT_DH):
                o_ref[(j * IP_TN) // ATT_DH + k] = acc[:, k * ATT_DH:(k + 1) * ATT_DH].astype(o_ref.dtype)


def _inproj_att(x2, ada3, g1, w_att, qg, kg, S):
    T = x2.shape[0]
    tm = IP_TM
    tiles_per_seq = S // tm
    return pl.pallas_call(
        _inproj_att_kernel,
        grid=(T // tm,),
        in_specs=[pl.BlockSpec((tm, D_MODEL), lambda i: (i, 0)),
                  pl.BlockSpec((None, 6, D_MODEL), lambda i: (i // tiles_per_seq, 0, 0)),
                  _resident((1, D_MODEL)),
                  _resident((D_MODEL, AW)),
                  _resident((1, ATT_DH)),
                  _resident((1, ATT_DH))],
        out_specs=[pl.BlockSpec((tm, D_MODEL), lambda i: (i, 0)),
                   pl.BlockSpec((None, AW // ATT_DH, tm, ATT_DH),
                                lambda i: (i // tiles_per_seq, 0, i % tiles_per_seq, 0))],
        out_shape=[jax.ShapeDtypeStruct((T, D_MODEL), BF16),
                   jax.ShapeDtypeStruct((T // S, AW // ATT_DH, S, ATT_DH), BF16)],
        compiler_params=pltpu.CompilerParams(
            dimension_semantics=("parallel",), vmem_limit_bytes=VMEM_LIMIT),
        name="inproj_att",
    )(x2, ada3, g1, w_att, qg, kg)


ATT_UNROLL = 8


def _mix_kernel(u_ref, wg_ref, *rest, S):
    n_wb = len(MIX_WBLOCKS)
    _mix_body(u_ref, wg_ref, rest[:n_wb], *rest[n_wb:], S=S)


def _mix_body(u_ref, wg_ref, wb_refs, wif_ref,
                q0_ref, k0_ref, v0_ref, q1_ref, k1_ref, v1_ref, q2_ref, k2_ref, v2_ref, bias_ref,
                p_ref, gt_ref, o_ref,
                q1f, k1f, v1f, q2f, k2f, v2f, o_scr, l_scr, o2d, l2d, *, S):
    scale = ATT_DH ** -0.5
    blk = ATT_BLK

    ub = u_ref[...]

    n_g = (WIN_G[1] - WIN_G[0]) // IP_TN

    def proj_chunk(j):
        w = wg_ref[:, j * IP_TN:(j + 1) * IP_TN] if j < n_g else wb_refs[j - n_g][...]
        p_ref[:, j * IP_TN:(j + 1) * IP_TN] = _dot(ub, w).astype(p_ref.dtype)
    pending = [functools.partial(proj_chunk, j) for j in range(PW // IP_TN)]

    def emit_proj(n):
        for _ in range(min(n, len(pending))):
            pending.pop(0)()

    plan = [(1, 0), (1, 1), (1, 0), (1, 1), (1, 1), (1, 1)]

    gt_ref[...] = _dot_nt(wif_ref[...], ub)
    emit_proj(1)

    pad1 = blk * ATT_GROUPS[1][1]
    k1f[0:pad1, :] = jnp.zeros((pad1, LANE), F32)
    v1f[0:pad1, :] = jnp.zeros((pad1, LANE), F32)
    k1f[pad1:pad1 + S, :] = k1_ref[...].astype(F32)
    v1f[pad1:pad1 + S, :] = v1_ref[...].astype(F32)
    rs = ATT_GROUPS[1][1]
    for src, dst in ((q2_ref, q2f), (k2_ref, k2f), (v2_ref, v2f)):
        q1f[...] = src[...].astype(F32)
        for a in range(rs):
            dst[a] = q1f[pl.ds(a, S // rs, stride=rs), :]
    q1f[...] = q1_ref[...].astype(F32)

    def softmax_blocks(ops):
        n_mid, n_end = plan.pop(0)
        ss = [_dot_nt(q, kk) * scale + bias for (q, kk, _, bias) in ops]
        emit_proj(n_mid)
        ms = [jnp.max(s, axis=-1, keepdims=True) for s in ss]
        ps = [jnp.exp(s - m) for s, m in zip(ss, ms)]
        exts = [_dot(p.astype(BF16), jnp.concatenate([op[2], jnp.ones(op[2].shape, BF16)], axis=1))
                for p, op in zip(ps, ops)]
        emit_proj(n_end)
        return [(e[:, 0:LANE] / e[:, LANE:2 * LANE], m + jnp.log(e[:, LANE:2 * LANE])) for e, m in zip(exts, ms)]

    def run_dilated(n_batches, fetch, store):
        for it in range(n_batches):
            fetched = [fetch(it, u) for u in range(ATT_UNROLL)]
            res = softmax_blocks([f[:4] for f in fetched])
            for (o, lse), f in zip(res, fetched):
                store(f[4], o, lse)

    r1 = ATT_GROUPS[1][1]
    nb1 = S // r1 // blk
    assert ATT_UNROLL % nb1 == 0

    def fetch1(it, u):
        rho = it * (ATT_UNROLL // nb1) + u // nb1
        n = u % nb1
        start = rho + r1 * blk * n
        return (q1f[pl.ds(start, blk, stride=r1), :].astype(BF16),
                k1f[pl.ds(start, 2 * blk, stride=r1), :].astype(BF16),
                v1f[pl.ds(start, 2 * blk, stride=r1), :].astype(BF16),
                bias_ref[1, 1 if n == 0 else 0], start)

    def store1(start, o, lse):
        rows = pl.ds(start, blk, stride=r1)
        o_scr[0, rows, :] = o
        l_scr[0, rows, :] = lse
    run_dilated(r1 * nb1 // ATT_UNROLL, fetch1, store1)

    r2 = ATT_GROUPS[2][1]
    assert S // r2 == blk

    assert r2 == rs * rs

    def fetch2(it, u):
        rho = it * ATT_UNROLL + u
        rows = pl.ds(rho // rs, blk, stride=rs)
        return (q2f[rho % rs, rows, :].astype(BF16), k2f[rho % rs, rows, :].astype(BF16),
                v2f[rho % rs, rows, :].astype(BF16), bias_ref[2, 1][:, blk:2 * blk], rho)

    def store2(rho, o, lse):
        rows = pl.ds(rho // rs, blk, stride=rs)
        o2d[rho % rs, rows, :] = o
        l2d[rho % rs, rows, :] = lse
    run_dilated(r2 // ATT_UNROLL, fetch2, store2)
    for a in range(rs):
        o_scr[1, pl.ds(a, S // rs, stride=rs), :] = o2d[a]
        l_scr[1, pl.ds(a, S // rs, stride=rs), :] = l2d[a]

    for it in range(S // blk // ATT_UNROLL):
        ops = []
        for u in range(ATT_UNROLL):
            n = it * ATT_UNROLL + u
            q = q0_ref[n * blk:(n + 1) * blk, :]
            if n == 0:
                ops.append((q, k0_ref[0:blk, :], v0_ref[0:blk, :], bias_ref[0, 1][:, blk:2 * blk]))
            else:
                ops.append((q, k0_ref[(n - 1) * blk:(n + 1) * blk, :], v0_ref[(n - 1) * blk:(n + 1) * blk, :],
                            bias_ref[0, 0]))
        for u, (o0, lse0) in enumerate(softmax_blocks(ops)):
            n = it * ATT_UNROLL + u
            rows = slice(n * blk, (n + 1) * blk)
            l1, l2 = l_scr[0, rows, :], l_scr[1, rows, :]
            mx = jnp.maximum(jnp.maximum(lse0, l1), l2)
            e0, e1, e2 = jnp.exp(lse0 - mx), jnp.exp(l1 - mx), jnp.exp(l2 - mx)
            att = (e0 * o0 + e1 * o_scr[0, rows, :] + e2 * o_scr[1, rows, :]) / (e0 + e1 + e2)
            o_ref[rows, :] = att.astype(o_ref.dtype)
    emit_proj(len(pending))


MIX_WBLOCKS = tuple(range(WIN_MVO[0] // IP_TN, WIN_MVO[1] // IP_TN)) + tuple(
    range(WIN_MQK[0] // IP_TN, WIN_MQK[1] // IP_TN))


def _mix(u2d, w_gates, w_in_b, w_if_t, qkv3, bias_tiles, B, S):
    T = u2d.shape[0]
    tm = IP_TM
    tps = S // tm
    assert tps == ATT_HPG
    hpp = N_ATT_HEADS
    in_specs = [pl.BlockSpec((tm, D_MODEL), lambda b, j: (b * tps + j, 0)),
                _resident((D_MODEL, WIN_G[1] - WIN_G[0]))]
    in_specs += [pl.BlockSpec((D_MODEL, IP_TN), lambda b, j, cb=cb: (0, cb), pipeline_mode=pl.Buffered(1))
                 for cb in MIX_WBLOCKS]
    in_specs.append(_resident((16, D_MODEL)))
    for g in range(N_ATT_GROUPS):
        for part in range(3):
            in_specs.append(pl.BlockSpec((None, None, S, LANE),
                                         lambda b, j, cb=part * hpp + g * ATT_HPG: (b, cb + j, 0, 0)))
    in_specs.append(pl.BlockSpec((N_ATT_GROUPS, None, 2, ATT_BLK, 2 * ATT_BLK), lambda b, j: (0, j, 0, 0, 0)))
    pad1 = ATT_BLK * ATT_GROUPS[1][1]
    slab = pltpu.VMEM((ATT_GROUPS[1][1], S // ATT_GROUPS[1][1], LANE), F32)
    scratch = [pltpu.VMEM((S, LANE), F32), pltpu.VMEM((pad1 + S, LANE), F32), pltpu.VMEM((pad1 + S, LANE), F32),
               slab, slab, slab,
               pltpu.VMEM((N_ATT_GROUPS - 1, S, LANE), F32), pltpu.VMEM((N_ATT_GROUPS - 1, S, LANE), F32),
               slab, slab]
    bias5 = bias_tiles.reshape(N_ATT_GROUPS, ATT_HPG, 2, ATT_BLK, 2 * ATT_BLK)
    return pl.pallas_call(
        functools.partial(_mix_kernel, S=S),
        grid=(B, tps),
        in_specs=in_specs,
        out_specs=[pl.BlockSpec((tm, PW), lambda b, j: (b * tps + j, 0)),
                   pl.BlockSpec((None, 16, tm), lambda b, j: (b, 0, j)),
                   pl.BlockSpec((None, S, LANE), lambda b, j: (b, 0, j))],
        out_shape=[jax.ShapeDtypeStruct((T, PW), BF16),
                   jax.ShapeDtypeStruct((B, 16, S), F32),
                   jax.ShapeDtypeStruct((B, S, ATT_W), BF16)],
        scratch_shapes=scratch,
        compiler_params=pltpu.CompilerParams(
            dimension_semantics=("parallel", "parallel"), vmem_limit_bytes=VMEM_LIMIT),
        name="mix",
    )(u2d, w_gates, *([w_in_b] * len(MIX_WBLOCKS)), w_if_t, *([qkv3] * 9), bias5)


TL_TM = 512
TL_TK = 512
ML_PAIR = 2
ML_NPAIR = ML_HEADS // ML_PAIR
ML_CPT = TL_TM // ML_L
ML_NR = ML_CPT * ML_HEADS
XS_HDR = 8


def _tail_kernel(x_ref, ada_ref, att_ref, ga_ref, gm_ref, g2_ref, wa_ref, wm_ref, wo_ref, w1_ref, w2_ref,
                 mq_ref, mk_ref, mv_ref, mo_ref, gt_ref, bif_ref, cw_ref, cb_ref, ng_ref,
                 o_ref,
                 hml_scr, xq, xk, qa, kab, kaf, cst_scr, mch_scr, rf, *, n_tiles, tiles_per_seq):
    i = pl.program_id(0)
    L = ML_L
    tm = TL_TM
    im = jnp.minimum(i, n_tiles - 1)

    @pl.when(i == 0)
    def _():
        hml_scr[...] = jnp.zeros(hml_scr.shape, hml_scr.dtype)

    @pl.when(im % tiles_per_seq == 0)
    def _():
        cst_scr[...] = jnp.zeros(cst_scr.shape, F32)
        mch_scr[...] = jnp.zeros(mch_scr.shape, F32)
        xq[tm:tm + XS_HDR, :] = jnp.zeros((XS_HDR, ML_QK_W), F32)
        xk[tm:tm + XS_HDR, :] = jnp.zeros((XS_HDR, ML_QK_W), F32)


    def merge_stage():
        y_att = _dot(att_ref[...], wa_ref[...])
        y_ml = _dot(hml_scr[...], wm_ref[...])
        ga = jax.nn.sigmoid(ga_ref[...].astype(F32))
        gm = jax.nn.sigmoid(gm_ref[...].astype(F32))
        return (ga * y_att + gm * y_ml).astype(BF16)

    def out_proj_stage(mix):
        x1 = x_ref[...] + ada_ref[2:3, :] * _dot(mix, wo_ref[...])
        ms = jnp.mean(x1 * x1, axis=-1, keepdims=True)
        y = x1 * lax.rsqrt(ms + EPS) * g2_ref[...]
        return x1, (y * (1.0 + ada_ref[4:5, :]) + ada_ref[3:4, :]).astype(BF16)

    def mlp_up(u2, k):
        hdn = jnp.maximum(_dot(u2, w1_ref[:, k * TL_TK:(k + 1) * TL_TK]), 0.0)
        return (hdn * hdn).astype(BF16)

    def mlp_down(hsq, acc, k):
        part = _dot(hsq, w2_ref[k * TL_TK:(k + 1) * TL_TK, :])
        return part if acc is None else acc + part

    mix = merge_stage()

    lane = lax.broadcasted_iota(jnp.int32, (1, LANE), 1)
    srow = lax.broadcasted_iota(jnp.int32, (LANE, 1), 0)
    in_head_lane = [(lane >= hh * ML_DK) & (lane < (hh + 1) * ML_DK) for hh in range(ML_PAIR)]
    in_head_row = [(srow >= hh * ML_DK) & (srow < (hh + 1) * ML_DK) for hh in range(ML_PAIR)]
    head0_rows = in_head_row[0]

    def conv_silu(src_ref, xs, c0):
        xs[0:XS_HDR, :] = xs[tm:tm + XS_HDR, :]
        xs[XS_HDR:XS_HDR + tm, :] = src_ref[...].astype(F32)
        yv = cb_ref[:, c0:c0 + ML_QK_W]
        for jj in range(CONV_W):
            yv = yv + cw_ref[jj:jj + 1, c0:c0 + ML_QK_W] * xs[pl.ds(XS_HDR - (CONV_W - 1) + jj, tm), :]
        return yv * jax.nn.sigmoid(yv)

    yq = conv_silu(mq_ref, xq, 0)
    for p in range(ML_NPAIR):
        for hh in range(ML_PAIR):
            qa[hh, :, p * LANE:(p + 1) * LANE] = jnp.where(in_head_lane[hh], yq[:, p * LANE:(p + 1) * LANE],
                                                           0.0).astype(BF16)
    yk = conv_silu(mk_ref, xk, ML_QK_W) * (ML_DK ** -0.5)
    kaf[...] = yk
    kab[...] = yk.astype(BF16)

    x1, u2 = out_proj_stage(mix)

    r_i = lax.broadcasted_iota(jnp.int32, (L, L), 0)
    c_i = lax.broadcasted_iota(jnp.int32, (L, L), 1)
    ones_m = jnp.ones((L, L), BF16)
    incl_upper = (r_i <= c_i).astype(BF16)
    causal = c_i <= r_i
    lane_nr = lax.broadcasted_iota(jnp.int32, (ML_NR, L), 1)
    pad_rows = jnp.zeros((L - ML_NR, L), F32)

    def col_form(row_form):
        return jnp.concatenate([row_form, pad_rows], axis=0).T

    li = jnp.concatenate([gt_ref[0:ML_HEADS, c * L:(c + 1) * L] + bif_ref[0:ML_HEADS, :]
                          for c in range(ML_CPT)], axis=0)
    zf = jnp.concatenate([gt_ref[ML_HEADS:2 * ML_HEADS, c * L:(c + 1) * L] + bif_ref[ML_HEADS:2 * ML_HEADS, :]
                          for c in range(ML_CPT)], axis=0)
    lf = jnp.minimum(zf, 0.0) - jnp.log1p(jnp.exp(-jnp.abs(zf)))
    brow = _dot3(lf, incl_upper)
    bend = _dot3(lf, ones_m)
    u = brow - li
    g = bend - u
    maxg = jnp.max(g, axis=-1, keepdims=True)
    m = mch_scr[...]
    for c in range(ML_CPT):
        rs = slice(c * ML_HEADS, (c + 1) * ML_HEADS)
        rf[1, rs, :] = m
        m = jnp.maximum(bend[rs, :] + m, maxg[rs, :])
        rf[2, rs, :] = m
    mch_scr[...] = m
    m_cur = rf[1]
    m_nxt = rf[2]
    rf[0] = u
    rf[3] = jnp.exp(bend + m_cur - m_nxt)
    rf[4] = jnp.exp(g - m_nxt)
    pm = -u
    sh = 1
    while sh < L:
        pm = jnp.maximum(pm, jnp.where(lane_nr >= sh, pltpu.roll(pm, sh, axis=1), NEG))
        sh *= 2
    d1 = -jnp.maximum(m_cur, pm)
    dcol = col_form(d1)
    ecol = col_form(jnp.exp(d1 - brow))

    ones_v = jnp.ones((L, LANE), BF16)

    def prepare(c, p):
        rows = slice(c * L, (c + 1) * L)
        pl_ = slice(p * LANE, (p + 1) * LANE)
        r0 = c * ML_HEADS + p * ML_PAIR
        kc = kab[rows, pl_]
        kt = kaf[rows, pl_].T
        wk = jnp.where(head0_rows, rf[4, r0:r0 + 1, :], rf[4, r0 + 1:r0 + 2, :])
        ktw = (kt * wk).astype(BF16)
        ktw2 = jnp.concatenate([jnp.where(in_head_row[hh], ktw, jnp.zeros_like(ktw)) for hh in range(ML_PAIR)],
                               axis=1)
        vexts = [jnp.concatenate([mv_ref[rows, (r0 % ML_HEADS + hh) * LANE:(r0 % ML_HEADS + hh + 1) * LANE],
                                  ones_v], axis=1) for hh in range(ML_PAIR)]
        z = _dot(ktw2, jnp.concatenate(vexts, axis=0))
        ss = [_dot_nt(qa[hh, rows, pl_], kc) for hh in range(ML_PAIR)]
        drow = [jnp.broadcast_to(dcol[:, r0 + hh:r0 + hh + 1], (L, L)) for hh in range(ML_PAIR)]
        erow = [jnp.broadcast_to(ecol[:, r0 + hh:r0 + hh + 1], (L, L)) for hh in range(ML_PAIR)]
        return ss, z, vexts, drow, erow

    order = [(c, p) for c in range(ML_CPT) for p in range(ML_NPAIR)]
    assert 2 * (D_FF // TL_TK) == len(order)
    cst = [cst_scr[p] for p in range(ML_NPAIR)]
    nxt = prepare(*order[0])
    acc = None
    hsq = None
    for n, (c, p) in enumerate(order):
        rows = slice(c * L, (c + 1) * L)
        pl_ = slice(p * LANE, (p + 1) * LANE)
        r0 = c * ML_HEADS + p * ML_PAIR
        ss, z, vexts, drow, erow = nxt
        if n + 1 < len(order):
            nxt = prepare(*order[n + 1])
        cst_b = cst[p].astype(BF16)
        lhs = []
        for hh in range(ML_PAIR):
            r = r0 + hh
            w = ss[hh] * jnp.exp(jnp.where(causal, drow[hh] - rf[0, r:r + 1, :], NEG))
            qi = qa[hh, rows, pl_].astype(F32) * jnp.exp(drow[hh] + rf[1, r:r + 1, :])
            lhs.append(jnp.concatenate([w.astype(BF16), qi.astype(BF16)], axis=1))
        if n % 2 == 0:
            hsq = mlp_up(u2, n // 2)
        else:
            acc = mlp_down(hsq, acc, n // 2)
        for hh in range(ML_PAIR):
            cols = slice((p * ML_PAIR + hh) * LANE, (p * ML_PAIR + hh + 1) * LANE)
            tot = _dot(lhs[hh], jnp.concatenate([vexts[hh], cst_b], axis=0))
            hval = tot[:, 0:LANE] / jnp.maximum(jnp.abs(tot[:, LANE:2 * LANE]), erow[hh])
            msq = jnp.mean(hval * hval, axis=-1, keepdims=True)
            hn = hval * lax.rsqrt(msq + EPS) * ng_ref[:, cols]
            hml_scr[rows, cols] = (hn * jax.nn.sigmoid(mo_ref[rows, cols].astype(F32))).astype(hml_scr.dtype)
        dec = jnp.where(head0_rows, rf[3, r0:r0 + 1, :], rf[3, r0 + 1:r0 + 2, :])
        cst[p] = jnp.concatenate([dec, dec], axis=1) * cst[p] + z
    for p in range(ML_NPAIR):
        cst_scr[p] = cst[p]
    o_ref[...] = x1 + ada_ref[5:6, :] * acc


def _tail(x2, ada3, att2, proj2, gates_t, bif_b, conv_w, conv_b, norm_g, g2, wa, wm, wo, w1, w2, S):
    T = x2.shape[0]
    tm = TL_TM
    tps = S // tm
    n_tiles = T // tm
    cur = lambda i: jnp.minimum(i, n_tiles - 1)
    prv = lambda i: jnp.maximum(i - 1, 0)
    gcb = CB_G * LANE // D_MODEL
    in_specs = [
        pl.BlockSpec((tm, D_MODEL), lambda i: (prv(i), 0)),
        pl.BlockSpec((None, 6, D_MODEL), lambda i: (prv(i) // tps, 0, 0)),
        pl.BlockSpec((tm, ATT_W), lambda i: (prv(i), 0)),
        pl.BlockSpec((tm, D_MODEL), lambda i: (prv(i), gcb)),
        pl.BlockSpec((tm, D_MODEL), lambda i: (prv(i), gcb + 1)),
        _resident((1, D_MODEL)),
        _resident((ATT_W, D_MODEL)), _resident((ML_V_W, D_MODEL)), _resident((D_MODEL, D_MODEL)),
        _resident((D_MODEL, D_FF)), _resident((D_FF, D_MODEL)),
        pl.BlockSpec((tm, ML_QK_W), lambda i: (cur(i), CB_MQ * LANE // ML_QK_W)),
        pl.BlockSpec((tm, ML_QK_W), lambda i: (cur(i), CB_MK * LANE // ML_QK_W)),
        pl.BlockSpec((tm, ML_V_W), lambda i: (cur(i), CB_MV * LANE // ML_V_W)),
        pl.BlockSpec((tm, ML_V_W), lambda i: (cur(i), CB_MO * LANE // ML_V_W)),
        pl.BlockSpec((None, 2 * ML_HEADS, tm), lambda i: (cur(i) // tps, 0, cur(i) % tps)),
        _resident((2 * ML_HEADS, LANE)),
        _resident((CONV_W, 2 * ML_QK_W)), _resident((1, 2 * ML_QK_W)), _resident((1, ML_V_W)),
    ]
    scratch = [pltpu.VMEM((tm, ML_V_W), BF16),
               pltpu.VMEM((tm + XS_HDR, ML_QK_W), F32), pltpu.VMEM((tm + XS_HDR, ML_QK_W), F32),
               pltpu.VMEM((ML_PAIR, tm, ML_QK_W), BF16), pltpu.VMEM((tm, ML_QK_W), BF16),
               pltpu.VMEM((tm, ML_QK_W), F32),
               pltpu.VMEM((ML_NPAIR, LANE, 2 * LANE), F32), pltpu.VMEM((ML_HEADS, LANE), F32),
               pltpu.VMEM((5, ML_NR, ML_L), F32)]
    return pl.pallas_call(
        functools.partial(_tail_kernel, n_tiles=n_tiles, tiles_per_seq=tps),
        grid=(n_tiles + 1,),
        in_specs=in_specs,
        out_specs=pl.BlockSpec((tm, D_MODEL), lambda i: (prv(i), 0)),
        out_shape=jax.ShapeDtypeStruct((T, D_MODEL), F32),
        scratch_shapes=scratch,
        compiler_params=pltpu.CompilerParams(
            dimension_semantics=("arbitrary",), vmem_limit_bytes=VMEM_LIMIT),
        name="tail",
    )(x2, ada3, att2, proj2, proj2, g2, wa, wm, wo, w1, w2,
      proj2, proj2, proj2, proj2, gates_t, bif_b, conv_w, conv_b, norm_g)


def kernel(x, c, w_ada, b_ada, norm1_g, norm2_g, w_in, b_if, conv_w, conv_b, q_norm_g, k_norm_g,
           rel_bias, mlstm_norm_g, w_att_out, w_ml_out, w_out, w_ff1, w_ff2):
    B, S, D = x.shape
    T = B * S
    depth = w_ada.shape[0]
    bias_tiles = _bias_tiles(rel_bias)
    x2 = x.reshape(T, D)
    for l in range(depth):
        ada3 = _ada(c, w_ada[l], b_ada[l]).reshape(B, 6, D)
        w_in_b = w_in[l][:, :WIN_IF[0]].astype(BF16)
        w_gates = w_in[l][:, WIN_G[0]:WIN_G[1]].astype(BF16)
        w_if_t = w_in[l][:, WIN_IF[0]:WIN_IF[1]].T.astype(BF16)
        u2d, qkv = _inproj_att(x2, ada3, norm1_g[l].reshape(1, D), w_in_b,
                               q_norm_g[l].reshape(1, ATT_DH), k_norm_g[l].reshape(1, ATT_DH), S)
        proj2, gates_t, att = _mix(u2d, w_gates, w_in_b, w_if_t, qkv, bias_tiles, B, S)
        bif_b = jnp.broadcast_to(b_if[l].reshape(2 * ML_HEADS, 1), (2 * ML_HEADS, LANE))
        x2 = _tail(x2, ada3, att.reshape(T, ATT_W), proj2, gates_t, bif_b, conv_w[l], conv_b[l].reshape(1, -1),
                   mlstm_norm_g[l].reshape(1, ML_V_W), norm2_g[l].reshape(1, D),
                   w_att_out[l].astype(BF16), w_ml_out[l].astype(BF16), w_out[l].astype(BF16),
                   w_ff1[l].astype(BF16), w_ff2[l].astype(BF16), S)
    return x2.reshape(B, S, D)
```
